```python
import math
import jax
import jax.numpy as jnp
from jax import lax
import numpy as np

D_MODEL = 1024
BATCH = 4
SEQ = 4096
DEPTH = 2
DEC_BATCH = 128
DEC_SEQ = 8
PAST_LEN = 16384
PAGE_SIZE = 128

N_MIXERS = 2
N_A_LAYERS = (DEPTH + 1) // 2
N_B_LAYERS = DEPTH // 2
EPS = 1e-6
Q_BLOCK = 128

MLA_HEADS = 12
Q_LORA = 256
KV_LORA = 256
NOPE_DIM = 64
ROPE_DIM = 32
MLA_V_DIM = 64
ROPE_THETA = 10000.0
MLA_SCALE = 1.0 / math.sqrt(NOPE_DIM + ROPE_DIM)
MLA_ROW = KV_LORA + ROPE_DIM

DIL_WINDOWS = (128, 512, 2048)
DIL_RATES = (1, 4, 16)
N_DIL_GROUPS = 3
DIL_HEADS = 4
DIL_QK_DIM = 64
DIL_V_DIM = 192
DIL_KEYS = DIL_WINDOWS[0] // DIL_RATES[0] + 1
DIL_SCALE = 1.0 / math.sqrt(DIL_QK_DIM)
DIL_ROW = DIL_QK_DIM + DIL_V_DIM

N_MEM = 256
MEM_HEADS = 4
MEM_HEAD_DIM = 64
MEM_Q = MEM_HEADS * MEM_HEAD_DIM
MEM_SCALE = 1.0 / math.sqrt(MEM_HEAD_DIM)

N_BUCKETS = 32
MAX_DISTANCE = 2048

D_FF = 2816
CONV_WIDTH = 3

MLA_IN = Q_LORA + KV_LORA + ROPE_DIM + MEM_Q
DIL_QK_W = N_DIL_GROUPS * DIL_HEADS * DIL_QK_DIM
DIL_V_W = N_DIL_GROUPS * DIL_HEADS * DIL_V_DIM
DIL_IN = 2 * DIL_QK_W + DIL_V_W + MEM_Q
MIX_OUT = MLA_HEADS * MLA_V_DIM + MEM_Q

kernel_name = "hybrid_mla_dilated_memory_convffn_step"


def rmsnorm(x, g):
    xf = x.astype(jnp.float32)
    y = xf * lax.rsqrt(jnp.mean(xf * xf, axis=-1, keepdims=True) + EPS)
    return (y * g.astype(jnp.float32)).astype(x.dtype)


def rope_tables(pos):
    inv = ROPE_THETA ** (-jnp.arange(0, ROPE_DIM, 2, dtype=jnp.float32) / ROPE_DIM)
    ang = pos.astype(jnp.float32)[:, None] * inv[None, :]
    return jnp.cos(ang), jnp.sin(ang)


def rope(x, cos, sin):
    xf = x.astype(jnp.float32)
    half = xf.shape[-1] // 2
    x1, x2 = xf[..., :half], xf[..., half:]
    return jnp.concatenate([x1 * cos - x2 * sin, x2 * cos + x1 * sin], axis=-1).astype(x.dtype)


def rel_bucket(dist):
    max_exact = N_BUCKETS // 2
    d = jnp.maximum(dist.astype(jnp.float32), 1.0)
    large = max_exact + (jnp.log(d / max_exact) / math.log(MAX_DISTANCE / max_exact)
                         * (N_BUCKETS - max_exact)).astype(jnp.int32)
    large = jnp.minimum(large, N_BUCKETS - 1)
    return jnp.where(dist < max_exact, dist, large)


def dilated_bias(rel_bias):
    out = []
    for g in range(N_DIL_GROUPS):
        bk = rel_bucket(DIL_RATES[g] * jnp.arange(DIL_KEYS, dtype=jnp.int32))
        out.append(rel_bias[bk][:, g * DIL_HEADS:(g + 1) * DIL_HEADS].T.astype(jnp.float32))
    return out


def mla_project(z, g_q_a, w_q_b, g_kv_a, g_qn_nope, g_qn_pe, g_kn_pe, cos, sin):
    n, t = z.shape[:2]
    o1, o2, o3 = Q_LORA, Q_LORA + KV_LORA, Q_LORA + KV_LORA + ROPE_DIM
    q = (rmsnorm(z[..., :o1], g_q_a) @ w_q_b).reshape(n, t, MLA_HEADS, NOPE_DIM + ROPE_DIM)
    q_nope = rmsnorm(q[..., :NOPE_DIM], g_qn_nope)
    q_pe = rope(rmsnorm(q[..., NOPE_DIM:], g_qn_pe), cos[:, None, :], sin[:, None, :])
    c = rmsnorm(z[..., o1:o2], g_kv_a)
    k_pe = rope(rmsnorm(z[..., o2:o3], g_kn_pe), cos, sin)
    rows = jnp.concatenate([c, k_pe], axis=-1)
    return q_nope, q_pe, rows, z[..., o3:]


def mla_prompt_attn(q_nope, q_pe, rows, w_uk, w_uv):
    n, t = q_nope.shape[:2]
    c, k_pe = rows[..., :KV_LORA], rows[..., KV_LORA:]
    k_nope = jnp.einsum('ntc,chd->nthd', c, w_uk)
    v = jnp.einsum('ntc,chd->nthd', c, w_uv)
    nb = t // Q_BLOCK
    qn = q_nope.reshape(n, nb, Q_BLOCK, MLA_HEADS, NOPE_DIM).transpose(1, 0, 2, 3, 4)
    qp = q_pe.reshape(n, nb, Q_BLOCK, MLA_HEADS, ROPE_DIM).transpose(1, 0, 2, 3, 4)
    kpos = jnp.arange(t)

    def one_block(args):
        qn_b, qp_b, b = args
        qpos = b * Q_BLOCK + jnp.arange(Q_BLOCK)
        s = (jnp.einsum('nqhd,nkhd->nhqk', qn_b, k_nope)
             + jnp.einsum('nqhr,nkr->nhqk', qp_b, k_pe)).astype(jnp.float32) * MLA_SCALE
        s = jnp.where(kpos[None, :] <= qpos[:, None], s, -jnp.inf)
        p = jax.nn.softmax(s, axis=-1).astype(v.dtype)
        return jnp.einsum('nhqk,nkhd->nqhd', p, v)

    o = lax.map(one_block, (qn, qp, jnp.arange(nb)))
    return o.transpose(1, 0, 2, 3, 4).reshape(n, t, MLA_HEADS * MLA_V_DIM)


def mla_sample_attn(q_nope, q_pe, rows, pool, page_table, w_uk, w_uv):
    n, t = q_nope.shape[:2]
    past = page_table.shape[1] * PAGE_SIZE
    q_abs = jnp.einsum('nqhd,chd->nqhc', q_nope, w_uk)
    kpos = jnp.arange(past + t)
    qpos = past + jnp.arange(t)
    causal = kpos[None, :] <= qpos[:, None]

    def one_seq(args):
        qa, qp, new_rows, pt = args
        kv = jnp.concatenate([pool[pt].reshape(past, MLA_ROW), new_rows], axis=0)
        c, k_pe = kv[:, :KV_LORA], kv[:, KV_LORA:]
        s = (jnp.einsum('qhc,kc->hqk', qa, c)
             + jnp.einsum('qhr,kr->hqk', qp, k_pe)).astype(jnp.float32) * MLA_SCALE
        s = jnp.where(causal[None], s, -jnp.inf)
        p = jax.nn.softmax(s, axis=-1).astype(c.dtype)
        return jnp.einsum('hqk,kc->qhc', p, c)

    ctx = lax.map(one_seq, (q_abs, q_pe, rows, page_table))
    return jnp.einsum('nqhc,chd->nqhd', ctx, w_uv).reshape(n, t, MLA_HEADS * MLA_V_DIM)


def dilated_project(z, g_qn, g_kn):
    n, t = z.shape[:2]
    q = z[..., :DIL_QK_W].reshape(n, t, N_DIL_GROUPS, DIL_HEADS, DIL_QK_DIM)
    k = z[..., DIL_QK_W:2 * DIL_QK_W].reshape(n, t, N_DIL_GROUPS, DIL_HEADS, DIL_QK_DIM)
    v = z[..., 2 * DIL_QK_W:2 * DIL_QK_W + DIL_V_W].reshape(n, t, N_DIL_GROUPS, DIL_HEADS, DIL_V_DIM)
    return rmsnorm(q, g_qn[:, None, :]), rmsnorm(k, g_kn[:, None, :]), v, z[..., 2 * DIL_QK_W + DIL_V_W:]


def dilated_core(q, k, v, q_idx, dil, bias):
    idx = q_idx[:, None] - dil * jnp.arange(DIL_KEYS, dtype=jnp.int32)[None, :]
    valid = idx >= 0
    idxc = jnp.maximum(idx, 0)
    kg = jnp.take(k, idxc, axis=1)
    vg = jnp.take(v, idxc, axis=1)
    s = jnp.einsum('nqhd,nqjhd->nhqj', q, kg).astype(jnp.float32) * DIL_SCALE + bias[None, :, None, :]
    s = jnp.where(valid[None, None], s, -jnp.inf)
    m = jnp.max(s, axis=-1)
    p = jnp.exp(s - m[..., None])
    l = jnp.sum(p, axis=-1)
    o = jnp.einsum('nhqj,nqjhe->nqhe', (p / l[..., None]).astype(v.dtype), vg)
    return o, m.transpose(0, 2, 1), l.transpose(0, 2, 1)


def dilated_prompt(q, k, v, dil, bias):
    n, t = q.shape[:2]
    nb = t // Q_BLOCK
    qb = q.reshape(n, nb, Q_BLOCK, DIL_HEADS, DIL_QK_DIM).transpose(1, 0, 2, 3, 4)

    def one_block(args):
        q_blk, b = args
        return dilated_core(q_blk, k, v, b * Q_BLOCK + jnp.arange(Q_BLOCK, dtype=jnp.int32), dil, bias)

    o, m, l = lax.map(one_block, (qb, jnp.arange(nb, dtype=jnp.int32)))
    o = o.transpose(1, 0, 2, 3, 4).reshape(n, t, DIL_HEADS, DIL_V_DIM)
    m = m.transpose(1, 0, 2, 3).reshape(n, t, DIL_HEADS)
    l = l.transpose(1, 0, 2, 3).reshape(n, t, DIL_HEADS)
    return o, m, l


def merge_groups(results):
    o = jnp.stack([r[0] for r in results])
    m = jnp.stack([r[1] for r in results])
    l = jnp.stack([r[2] for r in results])
    w = l * jnp.exp(m - jnp.max(m, axis=0, keepdims=True))
    w = w / jnp.sum(w, axis=0, keepdims=True)
    out = jnp.sum(o.astype(jnp.float32) * w[..., None], axis=0).astype(o.dtype)
    return out.reshape(out.shape[0], out.shape[1], DIL_HEADS * DIL_V_DIM)


def mem_kv_from(mem, g, w, g_kn):
    n, m = mem.shape[:2]
    kv = (rmsnorm(mem, g) @ w).reshape(n, m, MEM_HEADS, 2 * MEM_HEAD_DIM)
    return jnp.concatenate([rmsnorm(kv[..., :MEM_HEAD_DIM], g_kn), kv[..., MEM_HEAD_DIM:]], axis=-1)


def mem_attn(mq, kv, g_qn):
    n, t = mq.shape[:2]
    q = rmsnorm(mq.reshape(n, t, MEM_HEADS, MEM_HEAD_DIM), g_qn)
    s = jnp.einsum('nthd,nmhd->nhtm', q, kv[..., :MEM_HEAD_DIM]).astype(jnp.float32) * MEM_SCALE
    p = jax.nn.softmax(s, axis=-1).astype(kv.dtype)
    return jnp.einsum('nhtm,nmhd->nthd', p, kv[..., MEM_HEAD_DIM:]).reshape(n, t, MEM_Q)


def conv_ffn(x, prev, g, w_up, conv_w, conv_b, w_down):
    t = x.shape[1]
    u = rmsnorm(x, g) @ w_up
    gate, val = u[..., :D_FF], u[..., D_FF:]
    gp = jnp.concatenate([prev, gate], axis=1)
    conv = conv_b + sum(gp[:, j:j + t] * conv_w[j] for j in range(CONV_WIDTH))
    y = (jax.nn.silu(conv) * val) @ w_down
    return y, gp[:, -(CONV_WIDTH - 1):]


def setup_inputs(seed: int = 0) -> dict:
    key = jax.random.key(seed)
    ks = iter(jax.random.split(key, 48))
    f32 = jnp.float32

    def nrm(shape, scale):
        return scale * jax.random.normal(next(ks), shape, f32)

    def gain(shape):
        return 1.0 + 0.02 * jax.random.normal(next(ks), shape, f32)

    n_pages = PAST_LEN // PAGE_SIZE
    n_phys = (DEC_BATCH * n_pages * 5) // 4
    page_table = jax.random.permutation(next(ks), n_phys)[:DEC_BATCH * n_pages]
    page_table = page_table.reshape(DEC_BATCH, n_pages).astype(jnp.int32)
    d = {}
    d['x_prompt'] = nrm((BATCH, SEQ, D_MODEL), 1.0)
    d['x_sample'] = nrm((DEC_BATCH, DEC_SEQ, D_MODEL), 1.0)
    d['cache_mla'] = nrm((N_A_LAYERS, n_phys, PAGE_SIZE, MLA_ROW), 1.0)
    for g in range(N_DIL_GROUPS):
        d['state_win%d' % g] = nrm((N_B_LAYERS, DEC_BATCH, min(DIL_WINDOWS[g], PAST_LEN), DIL_HEADS, DIL_ROW), 1.0)
    d['cache_mem'] = nrm((DEPTH, DEC_BATCH, N_MEM, MEM_HEADS, 2 * MEM_HEAD_DIM), 1.0)
    d['state_conv'] = nrm((DEPTH, DEC_BATCH, CONV_WIDTH - 1, D_FF), 1.0)
    d['page_table'] = page_table
    d['mem_prompt'] = nrm((BATCH, N_MEM, D_MODEL), 1.0)
    d['rel_bias'] = nrm((N_BUCKETS, N_DIL_GROUPS * DIL_HEADS), 0.1)
    d['g_attn'] = gain((DEPTH, D_MODEL))
    d['w_o'] = nrm((DEPTH, MIX_OUT, D_MODEL), MIX_OUT ** -0.5)
    d['w_in_a'] = nrm((N_A_LAYERS, D_MODEL, MLA_IN), D_MODEL ** -0.5)
    d['g_q_a'] = gain((N_A_LAYERS, Q_LORA))
    d['w_q_b'] = nrm((N_A_LAYERS, Q_LORA, MLA_HEADS * (NOPE_DIM + ROPE_DIM)), Q_LORA ** -0.5)
    d['g_kv_a'] = gain((N_A_LAYERS, KV_LORA))
    d['w_uk'] = nrm((N_A_LAYERS, KV_LORA, MLA_HEADS, NOPE_DIM), KV_LORA ** -0.5)
    d['w_uv'] = nrm((N_A_LAYERS, KV_LORA, MLA_HEADS, MLA_V_DIM), KV_LORA ** -0.5)
    d['g_qn_nope'] = gain((N_A_LAYERS, NOPE_DIM))
    d['g_qn_pe'] = gain((N_A_LAYERS, ROPE_DIM))
    d['g_kn_pe'] = gain((N_A_LAYERS, ROPE_DIM))
    d['w_in_b'] = nrm((N_B_LAYERS, D_MODEL, DIL_IN), D_MODEL ** -0.5)
    d['g_qn_b'] = gain((N_B_LAYERS, N_DIL_GROUPS, DIL_QK_DIM))
    d['g_kn_b'] = gain((N_B_LAYERS, N_DIL_GROUPS, DIL_QK_DIM))
    d['g_mem'] = gain((DEPTH, D_MODEL))
    d['w_mem_kv'] = nrm((DEPTH, D_MODEL, 2 * MEM_Q), D_MODEL ** -0.5)
    d['g_qn_mem'] = gain((DEPTH, MEM_HEAD_DIM))
    d['g_kn_mem'] = gain((DEPTH, MEM_HEAD_DIM))
    d['g_ffn'] = gain((DEPTH, D_MODEL))
    d['w_up'] = nrm((DEPTH, D_MODEL, 2 * D_FF), D_MODEL ** -0.5)
    d['conv_w'] = nrm((DEPTH, CONV_WIDTH, D_FF), CONV_WIDTH ** -0.5)
    d['conv_b'] = nrm((DEPTH, D_FF), 0.01)
    d['w_down'] = nrm((DEPTH, D_FF, D_MODEL), D_FF ** -0.5)
    return d


def reference(x_prompt, x_sample, cache_mla, state_win0, state_win1, state_win2, cache_mem, state_conv,
              page_table, mem_prompt, rel_bias, g_attn, w_o, w_in_a, g_q_a, w_q_b, g_kv_a, w_uk, w_uv,
              g_qn_nope, g_qn_pe, g_kn_pe, w_in_b, g_qn_b, g_kn_b, g_mem, w_mem_kv, g_qn_mem, g_kn_mem,
              g_ffn, w_up, conv_w, conv_b, w_down):
    state_wins = (state_win0, state_win1, state_win2)
    t_p, t_s = x_prompt.shape[1], x_sample.shape[1]
    cos_p, sin_p = rope_tables(jnp.arange(t_p))
    cos_s, sin_s = rope_tables(PAST_LEN + jnp.arange(t_s))
    dil_b = dilated_bias(rel_bias)

    xp, xs = x_prompt, x_sample
    mla_p, mla_s = [], []
    win_p = [[] for _ in range(N_DIL_GROUPS)]
    win_s = [[] for _ in range(N_DIL_GROUPS)]
    mem_p, conv_p, conv_s = [], [], []
    for i in range(DEPTH):
        j = i // N_MIXERS
        hp = rmsnorm(xp, g_attn[i])
        hs = rmsnorm(xs, g_attn[i])
        if i % N_MIXERS == 0:
            qn_p, qp_p, rows_p, mq_p = mla_project(hp @ w_in_a[j], g_q_a[j], w_q_b[j], g_kv_a[j],
                                                   g_qn_nope[j], g_qn_pe[j], g_kn_pe[j], cos_p, sin_p)
            qn_s, qp_s, rows_s, mq_s = mla_project(hs @ w_in_a[j], g_q_a[j], w_q_b[j], g_kv_a[j],
                                                   g_qn_nope[j], g_qn_pe[j], g_kn_pe[j], cos_s, sin_s)
            tok_p = mla_prompt_attn(qn_p, qp_p, rows_p, w_uk[j], w_uv[j])
            tok_s = mla_sample_attn(qn_s, qp_s, rows_s, cache_mla[j], page_table, w_uk[j], w_uv[j])
            mla_p.append(rows_p)
            mla_s.append(rows_s)
        else:
            q_p, k_p, v_p, mq_p = dilated_project(hp @ w_in_b[j], g_qn_b[j], g_kn_b[j])
            q_s, k_s, v_s, mq_s = dilated_project(hs @ w_in_b[j], g_qn_b[j], g_kn_b[j])
            res_p, res_s = [], []
            for g in range(N_DIL_GROUPS):
                res_p.append(dilated_prompt(q_p[:, :, g], k_p[:, :, g], v_p[:, :, g], DIL_RATES[g], dil_b[g]))
                buf = state_wins[g][j]
                wb = buf.shape[1]
                kf = jnp.concatenate([buf[..., :DIL_QK_DIM], k_s[:, :, g]], axis=1)
                vf = jnp.concatenate([buf[..., DIL_QK_DIM:], v_s[:, :, g]], axis=1)
                res_s.append(dilated_core(q_s[:, :, g], kf, vf, wb + jnp.arange(t_s, dtype=jnp.int32),
                                          DIL_RATES[g], dil_b[g]))
                rows_pg = jnp.concatenate([k_p[:, :, g], v_p[:, :, g]], axis=-1)
                win_p[g].append(rows_pg[:, -min(DIL_WINDOWS[g], t_p):])
                rows_sg = jnp.concatenate([k_s[:, :, g], v_s[:, :, g]], axis=-1)
                win_s[g].append(jnp.concatenate([buf, rows_sg], axis=1)[:, -wb:])
            tok_p = merge_groups(res_p)
            tok_s = merge_groups(res_s)
        mkv_p = mem_kv_from(mem_prompt, g_mem[i], w_mem_kv[i], g_kn_mem[i])
        mem_p.append(mkv_p)
        mo_p = mem_attn(mq_p, mkv_p, g_qn_mem[i])
        mo_s = mem_attn(mq_s, cache_mem[i], g_qn_mem[i])
        xp = xp + jnp.concatenate([tok_p, mo_p], axis=-1) @ w_o[i]
        xs = xs + jnp.concatenate([tok_s, mo_s], axis=-1) @ w_o[i]
        f_p, c_p = conv_ffn(xp, jnp.zeros((xp.shape[0], CONV_WIDTH - 1, D_FF), xp.dtype),
                            g_ffn[i], w_up[i], conv_w[i], conv_b[i], w_down[i])
        f_s, c_s = conv_ffn(xs, state_conv[i], g_ffn[i], w_up[i], conv_w[i], conv_b[i], w_down[i])
        xp = xp + f_p
        xs = xs + f_s
        conv_p.append(c_p)
        conv_s.append(c_s)
    return (xp, xs, jnp.stack(mla_p), jnp.stack(mla_s),
            jnp.stack(win_p[0]), jnp.stack(win_p[1]), jnp.stack(win_p[2]),
            jnp.stack(win_s[0]), jnp.stack(win_s[1]), jnp.stack(win_s[2]),
            jnp.stack(mem_p), jnp.stack(conv_p), jnp.stack(conv_s))
```

```python
import functools
import math

import jax
import jax.numpy as jnp
import numpy as np
from jax import lax
from jax.experimental import pallas as pl
from jax.experimental.pallas import tpu as pltpu

F32 = jnp.float32
BF16 = jnp.bfloat16

EPS = 1e-6
LANES = 128
VMEM_LIMIT = 48 * 1024 * 1024

MLA_HEADS = 12
NOPE_DIM = 64
ROPE_DIM = 32
MLA_V_DIM = 64
Q_LORA = 256
KV_LORA = 256
MLA_ROW = KV_LORA + ROPE_DIM
ROPE_THETA = 10000.0
MLA_SCALE = 1.0 / math.sqrt(NOPE_DIM + ROPE_DIM)
MLA_QCAT = 384

DIL_WINDOWS = (128, 512, 2048)
DIL_RATES = (1, 4, 16)
N_GROUPS = 3
DIL_HEADS = 4
DIL_QK = 64
DIL_V = 192
DIL_KEYS = DIL_WINDOWS[0] // DIL_RATES[0] + 1
DIL_SCALE = 1.0 / math.sqrt(DIL_QK)
DIL_ROW = DIL_QK + DIL_V
DIL_SLAB = DIL_HEADS * DIL_ROW

MEM_HEADS = 4
MEM_DIM = 64
MEM_SCALE = 1.0 / math.sqrt(MEM_DIM)
MEM_W = MEM_HEADS * LANES

N_BUCKETS = 32
MAX_DISTANCE = 2048
NEG = -1e30


def _cparams(*sem):
    return pltpu.CompilerParams(dimension_semantics=sem, vmem_limit_bytes=VMEM_LIMIT)


def _dot(a, b):
    return jnp.dot(a, b, preferred_element_type=F32)


def _dot_nt(a, b):
    return lax.dot_general(a, b, (((1,), (1,)), ((), ())), preferred_element_type=F32)


def _split_dot(x, w):
    hi = x.astype(BF16)
    lo = (x - hi.astype(F32)).astype(BF16)
    return _dot(hi, w) + _dot(lo, w)


def _rms_rows(x, g):
    return x * lax.rsqrt(jnp.mean(x * x, axis=-1, keepdims=True) + EPS) * g


def _seg_scale(x, s_ref, st_ref, inv_ref):
    ss = _split_dot(x * x, s_ref[...])
    r = lax.rsqrt(ss * inv_ref[...] + EPS)
    return _split_dot(r, st_ref[...])


def _cols(vals, width):
    rows = vals[0].shape[0]
    lane = lax.broadcasted_iota(jnp.int32, (rows, width), 1)
    out = jnp.zeros((rows, width), F32)
    for k, v in enumerate(vals):
        out = jnp.where(lane == k, v, out)
    return out


def _full(shape):
    return pl.BlockSpec(shape, lambda *_: (0,) * len(shape))


def _proj_mla_kernel(prompt, x_ref, ga_ref, win_ref, gqa_ref, wq_ref, wqs_ref, gkv_ref,
                     cq_ref, sq_ref, tk_ref, s_ref, st_ref, inv_ref,
                     sm_ref, smt_ref, invm_ref, gm_ref, *rest):
    if prompt:
        wkc_ref, wkp_ref, wv_ref, q_out, rows_out, mq_out, k_out, v_out = rest
    else:
        wcat_ref, q_out, rows_out, mq_out = rest
    h = _rms_rows(x_ref[...], ga_ref[...]).astype(BF16)
    z = _dot(h, win_ref[...])
    qa = _rms_rows(z[:, :Q_LORA], gqa_ref[...]).astype(BF16)
    q = _dot(qa, wq_ref[...])
    qs = _dot(qa, wqs_ref[...])
    scale = _seg_scale(q, s_ref, st_ref, inv_ref)
    qf = ((q * cq_ref[...] + qs * sq_ref[...]) * scale).astype(BF16)
    c = _rms_rows(z[:, Q_LORA:Q_LORA + KV_LORA], gkv_ref[...])
    zk = z[:, 1024:1152]
    kp = zk[:, :ROPE_DIM]
    rk = lax.rsqrt(jnp.mean(kp * kp, axis=-1, keepdims=True) + EPS)
    t = zk * tk_ref[...] * rk
    kr = t + pltpu.roll(t, LANES - ROPE_DIM, 1)
    rows_out[:, :KV_LORA] = c
    rows_out[:, KV_LORA:] = kr[:, :ROPE_DIM]
    mq = z[:, 512:1024]
    mscale = _seg_scale(mq, sm_ref, smt_ref, invm_ref)
    mq_out[...] = (mq * mscale * gm_ref[...]).astype(BF16)
    if prompt:
        q_out[...] = qf
        cb = c.astype(BF16)
        k_out[...] = (_dot(cb, wkc_ref[...]) + _dot(kr.astype(BF16), wkp_ref[...])).astype(BF16)
        v_out[...] = _dot(cb, wv_ref[...]).astype(BF16)
    else:
        for hd in range(MLA_HEADS):
            q_out[:, hd * MLA_QCAT:(hd + 1) * MLA_QCAT] = _dot(
                qf[:, hd * LANES:(hd + 1) * LANES], wcat_ref[hd]).astype(BF16)


def _proj_mla(x, prompt, tm, w, cq, sq, tk):
    m, d = x.shape
    hw = MLA_HEADS * LANES
    row = lambda width: pl.BlockSpec((tm, width), lambda i: (i, 0))
    args = [x, w['g_attn'], w['w_in'], w['g_q_a'], w['wq'], w['wq_sw'], w['g_kv_a'],
            cq, sq, tk, w['S'], w['ST'], w['inv'], w['Sm'], w['SmT'], w['invm'], w['gm']]
    specs = [row(d)] + [_full(a.shape) for a in args[1:7]] + [row(hw), row(hw), row(LANES)] \
        + [_full(a.shape) for a in args[10:]]
    if prompt:
        extra = [w['wk_c'], w['wk_p'], w['wv']]
        outs = [jax.ShapeDtypeStruct((m, hw), BF16), jax.ShapeDtypeStruct((m, MLA_ROW), F32),
                jax.ShapeDtypeStruct((m, MEM_W), BF16), jax.ShapeDtypeStruct((m, hw), BF16),
                jax.ShapeDtypeStruct((m, hw), BF16)]
        ospecs = [row(hw), row(MLA_ROW), row(MEM_W), row(hw), row(hw)]
    else:
        extra = [w['wcat']]
        outs = [jax.ShapeDtypeStruct((m, MLA_HEADS * MLA_QCAT), BF16),
                jax.ShapeDtypeStruct((m, MLA_ROW), F32), jax.ShapeDtypeStruct((m, MEM_W), BF16)]
        ospecs = [row(MLA_HEADS * MLA_QCAT), row(MLA_ROW), row(MEM_W)]
    return pl.pallas_call(
        functools.partial(_proj_mla_kernel, prompt),
        grid=(m // tm,),
        in_specs=specs + [_full(a.shape) for a in extra],
        out_specs=ospecs, out_shape=outs,
        compiler_params=_cparams("parallel"),
        name="proj_mla_prompt" if prompt else "proj_mla_sample",
    )(*args, *extra)


def _flash_kernel(tq, q_ref, k_ref, v_ref, o_ref, m_scr, l_scr, acc_scr):
    i, j = pl.program_id(1), pl.program_id(2)

    @pl.when(j == 0)
    def _():
        m_scr[...] = jnp.full(m_scr.shape, NEG, F32)
        l_scr[...] = jnp.zeros(l_scr.shape, F32)
        acc_scr[...] = jnp.zeros(acc_scr.shape, F32)

    def step(masked):
        if masked:
            keep = lax.broadcasted_iota(jnp.int32, (tq, tq), 1) <= lax.broadcasted_iota(jnp.int32, (tq, tq), 0)
        for hd in range(MLA_HEADS):
            sl = slice(hd * LANES, (hd + 1) * LANES)
            s = _dot_nt(q_ref[0, :, sl], k_ref[0, :, sl])
            if masked:
                s = jnp.where(keep, s, NEG)
            m_old = m_scr[hd]
            m_new = jnp.maximum(m_old, jnp.max(s, axis=-1, keepdims=True))
            alpha = jnp.exp(m_old - m_new)
            p = jnp.exp(s - m_new)
            l_scr[hd] = alpha * l_scr[hd] + jnp.sum(p, axis=-1, keepdims=True)
            acc_scr[hd] = alpha * acc_scr[hd] + _dot(p.astype(BF16), v_ref[0, :, sl])
            m_scr[hd] = m_new

    @pl.when(j < i)
    def _():
        step(False)

    @pl.when(j == i)
    def _():
        step(True)
        for pr in range(MLA_HEADS // 2):
            e, o = 2 * pr, 2 * pr + 1
            o_ref[0, :, pr * LANES:(pr + 1) * LANES] = (
                acc_scr[e] / l_scr[e] + acc_scr[o] / l_scr[o]).astype(BF16)


def _mla_flash(q, k, v, tq):
    n, t, hw = q.shape
    nq = t // tq
    return pl.pallas_call(
        functools.partial(_flash_kernel, tq),
        grid=(n, nq, nq),
        in_specs=[pl.BlockSpec((1, tq, hw), lambda b, i, j: (b, i, 0)),
                  pl.BlockSpec((1, tq, hw), lambda b, i, j: (b, jnp.minimum(i, j), 0)),
                  pl.BlockSpec((1, tq, hw), lambda b, i, j: (b, jnp.minimum(i, j), 0))],
        out_specs=pl.BlockSpec((1, tq, MLA_HEADS * MLA_V_DIM), lambda b, i, j: (b, i, 0)),
        out_shape=jax.ShapeDtypeStruct((n, t, MLA_HEADS * MLA_V_DIM), BF16),
        scratch_shapes=[pltpu.VMEM((MLA_HEADS, tq, 1), F32), pltpu.VMEM((MLA_HEADS, tq, 1), F32),
                        pltpu.VMEM((MLA_HEADS, tq, LANES), F32)],
        compiler_params=_cparams("parallel", "parallel", "arbitrary"),
        name="mla_flash_prompt",
    )(q, k, v)


def _mla_sample_kernel(npg, t_s, pt_ref, q_ref, new_ref, *rest):
    pages, (o_ref, m_scr, l_scr, acc_scr) = rest[:npg], rest[npg:]
    c = pl.program_id(1)

    @pl.when(c == 0)
    def _():
        m_scr[...] = jnp.full(m_scr.shape, NEG, F32)
        l_scr[...] = jnp.zeros(l_scr.shape, F32)
        acc_scr[...] = jnp.zeros(acc_scr.shape, F32)

    q = q_ref[0][:, :MLA_ROW]

    def update(kb, s):
        m_old = m_scr[...]
        m_new = jnp.maximum(m_old, jnp.max(s, axis=-1, keepdims=True))
        alpha = jnp.exp(m_old - m_new)
        p = jnp.exp(s - m_new)
        l_scr[...] = alpha * l_scr[...] + jnp.sum(p, axis=-1, keepdims=True)
        acc_scr[...] = alpha * acc_scr[...] + _dot(p.astype(BF16), kb[:, :KV_LORA])
        m_scr[...] = m_new

    kb = jnp.concatenate([pg[0].astype(BF16) for pg in pages], axis=0)
    update(kb, _dot_nt(q, kb))

    @pl.when(c == pl.num_programs(1) - 1)
    def _():
        nb = new_ref[0].astype(BF16)
        s = _dot_nt(q, nb)
        rows = q.shape[0]
        tq = lax.broadcasted_iota(jnp.int32, (rows, t_s), 0) % t_s
        s = jnp.where(lax.broadcasted_iota(jnp.int32, (rows, t_s), 1) <= tq, s, NEG)
        update(nb, s)
        o_ref[0] = (acc_scr[...] / l_scr[...]).astype(BF16)


def _mla_sample_attn(q3, rows3, pool, page_table, npg):
    b, rows, _ = q3.shape
    t_s = rows3.shape[1]
    n_pages = page_table.shape[1]
    psz = pool.shape[1]

    def page_spec(k):
        return pl.BlockSpec((1, psz, MLA_ROW), lambda s, c, pt: (pt[s * n_pages + c * npg + k], 0, 0))

    return pl.pallas_call(
        functools.partial(_mla_sample_kernel, npg, t_s),
        grid_spec=pltpu.PrefetchScalarGridSpec(
            num_scalar_prefetch=1,
            grid=(b, n_pages // npg),
            in_specs=[pl.BlockSpec((1, rows, MLA_QCAT), lambda s, c, pt: (s, 0, 0)),
                      pl.BlockSpec((1, t_s, MLA_ROW), lambda s, c, pt: (s, 0, 0))]
            + [page_spec(k) for k in range(npg)],
            out_specs=pl.BlockSpec((1, rows, KV_LORA), lambda s, c, pt: (s, 0, 0)),
            scratch_shapes=[pltpu.VMEM((rows, 1), F32), pltpu.VMEM((rows, 1), F32),
                            pltpu.VMEM((rows, KV_LORA), F32)]),
        out_shape=jax.ShapeDtypeStruct((b, rows, KV_LORA), BF16),
        compiler_params=_cparams("parallel", "arbitrary"),
        name="mla_sample_attn",
    )(page_table.reshape(-1), q3, rows3, *([pool] * npg))


def _uv_kernel(ctx_ref, w_ref, o_ref):
    for pr in range(MLA_HEADS // 2):
        o_ref[:, pr * LANES:(pr + 1) * LANES] = _dot(
            ctx_ref[:, pr * 2 * KV_LORA:(pr + 1) * 2 * KV_LORA], w_ref[pr]).astype(BF16)


def _mla_uv(ctx, w_pairs, tm):
    m = ctx.shape[0]
    return pl.pallas_call(
        _uv_kernel, grid=(m // tm,),
        in_specs=[pl.BlockSpec((tm, ctx.shape[1]), lambda i: (i, 0)), _full(w_pairs.shape)],
        out_specs=pl.BlockSpec((tm, MLA_HEADS * MLA_V_DIM), lambda i: (i, 0)),
        out_shape=jax.ShapeDtypeStruct((m, MLA_HEADS * MLA_V_DIM), BF16),
        compiler_params=_cparams("parallel"), name="mla_sample_uv",
    )(ctx, w_pairs)


def _mem_kv_kernel(x_ref, g_ref, w_ref, s_ref, st_ref, inv_ref, gk_ref, vm_ref, o_ref):
    h = _rms_rows(x_ref[...], g_ref[...]).astype(BF16)
    kv = _dot(h, w_ref[...])
    scale = _seg_scale(kv, s_ref, st_ref, inv_ref) * gk_ref[...] + vm_ref[...]
    o_ref[...] = kv * scale


def _mem_kv(mem2, w, tm):
    m, d = mem2.shape
    args = [mem2, w['g_mem'], w['w_mem'], w['Sk'], w['SkT'], w['invk'], w['gk'], w['vmask']]
    return pl.pallas_call(
        _mem_kv_kernel, grid=(m // tm,),
        in_specs=[pl.BlockSpec((tm, d), lambda i: (i, 0))] + [_full(a.shape) for a in args[1:]],
        out_specs=pl.BlockSpec((tm, MEM_W), lambda i: (i, 0)),
        out_shape=jax.ShapeDtypeStruct((m, MEM_W), F32),
        compiler_params=_cparams("parallel"), name="mem_kv",
    )(*args)


def _mem_attn_body(q, kv):
    outs = []
    for hd in range(MEM_HEADS):
        sl = slice(hd * LANES, (hd + 1) * LANES)
        s = _dot_nt(q[:, sl], kv[:, sl])
        p = jnp.exp(s - jnp.max(s, axis=-1, keepdims=True))
        p = p / jnp.sum(p, axis=-1, keepdims=True)
        outs.append(_dot(p.astype(BF16), kv[:, sl]))
    return jnp.concatenate(outs, axis=-1)


def _mem_attn_kernel(q_ref, kv_ref, o_ref):
    o_ref[0] = _mem_attn_body(q_ref[0], kv_ref[0].astype(BF16)).astype(BF16)


def _mem_attn(q3, kv3, tq):
    n, t, _ = q3.shape
    return pl.pallas_call(
        _mem_attn_kernel, grid=(n, t // tq),
        in_specs=[pl.BlockSpec((1, tq, MEM_W), lambda b, i: (b, i, 0)),
                  pl.BlockSpec((1, kv3.shape[1], MEM_W), lambda b, i: (b, 0, 0))],
        out_specs=pl.BlockSpec((1, tq, MEM_W), lambda b, i: (b, i, 0)),
        out_shape=jax.ShapeDtypeStruct((n, t, MEM_W), BF16),
        compiler_params=_cparams("parallel", "parallel"), name="mem_attn",
    )(q3, kv3)


def _out_proj_kernel(x_ref, a_ref, b_ref, wa_ref, wb_ref, o_ref):
    o_ref[...] = x_ref[...] + _dot(a_ref[...], wa_ref[...]) + _dot(b_ref[...], wb_ref[...])


def _out_proj(x, a, b, wa, wb, tm):
    m, d = x.shape
    row = lambda width: pl.BlockSpec((tm, width), lambda i: (i, 0))
    return pl.pallas_call(
        _out_proj_kernel, grid=(m // tm,),
        in_specs=[row(d), row(a.shape[1]), row(b.shape[1]), _full(wa.shape), _full(wb.shape)],
        out_specs=row(d), out_shape=jax.ShapeDtypeStruct((m, d), F32),
        compiler_params=_cparams("parallel"), name="out_proj",
    )(x, a, b, wa, wb)


def _ffn_kernel(seq_tiles, t_s, d_ff, x_ref, g_ref, wu_ref, cw_ref, cb_ref, wd_ref, *rest):
    if seq_tiles:
        o_ref, st_ref, g_scr = rest
    else:
        p1_ref, p2_ref, o_ref, st_ref = rest
    x = x_ref[...]
    tm = x.shape[0]
    h = _rms_rows(x, g_ref[...]).astype(BF16)
    u = _dot(h, wu_ref[...])
    gate, val = u[:, :d_ff], u[:, d_ff:]
    if seq_tiles:
        @pl.when(pl.program_id(0) % seq_tiles == 0)
        def _():
            g_scr[0:8, :] = jnp.zeros((8, d_ff), F32)
        g_scr[8:, :] = gate
        g1 = g_scr[7:tm + 7, :]
        g2 = g_scr[6:tm + 6, :]
        tail = gate[tm - 8:, :]
        g_scr[0:8, :] = tail
        st_ref[...] = tail
    else:
        tpos = lax.broadcasted_iota(jnp.int32, (tm, 1), 0) % t_s
        g1 = jnp.where(tpos >= 1, pltpu.roll(gate, 1, 0), p1_ref[...])
        g2 = jnp.where(tpos >= 2, pltpu.roll(gate, 2, 0), p2_ref[...])
        st_ref[...] = gate
    cw = cw_ref[...]
    conv = cb_ref[...] + g2 * cw[0:1] + g1 * cw[1:2] + gate * cw[2:3]
    act = (conv * (1.0 / (1.0 + jnp.exp(-conv))) * val).astype(BF16)
    o_ref[...] = x + _dot(act, wd_ref[...])


def _ffn(x, seq_len, w, tm, prev=None):
    m, d = x.shape
    d_ff = w['w_down'].shape[0]
    row = lambda width: pl.BlockSpec((tm, width), lambda i: (i, 0))
    single = dict(pipeline_mode=pl.Buffered(1))
    wspecs = [_full(w['g_ffn'].shape),
              pl.BlockSpec(w['w_up'].shape, lambda i: (0, 0), **single),
              _full(w['conv_w'].shape), _full(w['conv_b'].shape),
              pl.BlockSpec(w['w_down'].shape, lambda i: (0, 0), **single)]
    wargs = [w['g_ffn'], w['w_up'], w['conv_w'], w['conv_b'], w['w_down']]
    if prev is None:
        seq_tiles = seq_len // tm
        return pl.pallas_call(
            functools.partial(_ffn_kernel, seq_tiles, 0, d_ff), grid=(m // tm,),
            in_specs=[row(d)] + wspecs,
            out_specs=[row(d), pl.BlockSpec((8, d_ff), lambda i: (i, 0))],
            out_shape=[jax.ShapeDtypeStruct((m, d), F32),
                       jax.ShapeDtypeStruct((m // tm * 8, d_ff), F32)],
            scratch_shapes=[pltpu.VMEM((tm + 8, d_ff), F32)],
            compiler_params=_cparams("arbitrary"), name="ffn_prompt",
        )(x, *wargs)
    p1, p2 = prev
    return pl.pallas_call(
        functools.partial(_ffn_kernel, 0, seq_len, d_ff), grid=(m // tm,),
        in_specs=[row(d)] + wspecs + [row(d_ff), row(d_ff)],
        out_specs=[row(d), row(d_ff)],
        out_shape=[jax.ShapeDtypeStruct((m, d), F32), jax.ShapeDtypeStruct((m, d_ff), F32)],
        compiler_params=_cparams("parallel"), name="ffn_sample",
    )(x, *wargs, p1, p2)


QW = N_GROUPS * DIL_HEADS * LANES
KVW = N_GROUPS * DIL_SLAB


def _proj_dil_kernel(prompt, x_ref, ga_ref, win_ref, sq_ref, sqt_ref, invq_ref, gq_ref,
                     sk_ref, skt_ref, invk_ref, gk_ref, vm_ref,
                     sm_ref, smt_ref, invm_ref, gm_ref, *rest):
    x = x_ref[0] if prompt else x_ref[...]
    h = _rms_rows(x, ga_ref[...]).astype(BF16)
    z = _dot(h, win_ref[...])
    q = z[:, :QW]
    qn = q * _seg_scale(q, sq_ref, sqt_ref, invq_ref) * gq_ref[...]
    kv = z[:, QW:QW + KVW]
    kvn = kv * (_seg_scale(kv, sk_ref, skt_ref, invk_ref) * gk_ref[...] + vm_ref[...])
    mq = z[:, QW + KVW:]
    mqn = (mq * _seg_scale(mq, sm_ref, smt_ref, invm_ref) * gm_ref[...]).astype(BF16)
    if not prompt:
        q_out, kv_out, mq_out = rest
        q_out[...] = qn.astype(BF16)
        kv_out[...] = kvn
        mq_out[...] = mqn
        return
    kv_out, mq_out = rest[0], rest[1]
    q_outs, kvb_outs, q_scr, kv_scr = rest[2:5], rest[5:8], rest[8], rest[9]
    kv_out[0] = kvn
    mq_out[0] = mqn
    for c in range(QW // LANES):
        q_scr[c] = qn[:, c * LANES:(c + 1) * LANES]
    for c in range(KVW // LANES):
        kv_scr[c] = kvn[:, c * LANES:(c + 1) * LANES]
    tm = q.shape[0]
    qc, kc = DIL_HEADS, DIL_SLAB // LANES
    for g, d in enumerate(DIL_RATES):
        for r in range(d):
            rows = pl.ds(r, tm // d, stride=d) if d > 1 else slice(None)
            for c in range(qc):
                q_outs[g][0, r, :, c * LANES:(c + 1) * LANES] = q_scr[g * qc + c, rows, :].astype(BF16)
            for c in range(kc):
                kvb_outs[g][0, r, :, c * LANES:(c + 1) * LANES] = kv_scr[g * kc + c, rows, :].astype(BF16)


def _proj_dil(x3, prompt, tm, w):
    n, t, d = x3.shape
    args = [x3, w['g_attn'], w['w_in'], w['Sq'], w['SqT'], w['invq'], w['gq'],
            w['Sk'], w['SkT'], w['invk'], w['gk'], w['vmask'], w['Sm'], w['SmT'], w['invm'], w['gm']]
    if not prompt:
        x2 = x3.reshape(n * t, d)
        m = n * t
        row = lambda width: pl.BlockSpec((tm, width), lambda i: (i, 0))
        return pl.pallas_call(
            functools.partial(_proj_dil_kernel, False), grid=(m // tm,),
            in_specs=[row(d)] + [_full(a.shape) for a in args[1:]],
            out_specs=[row(QW), row(KVW), row(MEM_W)],
            out_shape=[jax.ShapeDtypeStruct((m, QW), BF16), jax.ShapeDtypeStruct((m, KVW), F32),
                       jax.ShapeDtypeStruct((m, MEM_W), BF16)],
            compiler_params=_cparams("parallel"), name="proj_dil_sample",
        )(x2, *args[1:])
    row3 = lambda width: pl.BlockSpec((1, tm, width), lambda b, i: (b, i, 0))
    outs = [jax.ShapeDtypeStruct((n, t, KVW), F32), jax.ShapeDtypeStruct((n, t, MEM_W), BF16)]
    ospecs = [row3(KVW), row3(MEM_W)]
    for width in (DIL_HEADS * LANES, DIL_SLAB):
        for dil in DIL_RATES:
            outs.append(jax.ShapeDtypeStruct((n, dil, t // dil, width), BF16))
            ospecs.append(pl.BlockSpec((1, dil, tm // dil, width), lambda b, i: (b, 0, i, 0)))
    return pl.pallas_call(
        functools.partial(_proj_dil_kernel, True), grid=(n, t // tm),
        in_specs=[row3(d)] + [_full(a.shape) for a in args[1:]],
        out_specs=ospecs, out_shape=outs,
        scratch_shapes=[pltpu.VMEM((QW // LANES, tm, LANES), F32), pltpu.VMEM((KVW // LANES, tm, LANES), F32)],
        compiler_params=_cparams("parallel", "parallel"), name="proj_dil_prompt",
    )(*args)


def _band_kernel(tq, q_ref, kp_ref, kc_ref, bias_ref, o_ref, ml_ref):
    i = pl.program_id(1)
    kcat = jnp.concatenate([kp_ref[0], kc_ref[0]], axis=0)
    first = jnp.logical_and(i == 0, lax.broadcasted_iota(jnp.int32, (tq, 2 * tq), 1) < tq)
    ms, ls = [], []
    for hd in range(DIL_HEADS):
        s = _dot_nt(q_ref[0, :, hd * LANES:(hd + 1) * LANES], kcat[:, hd * DIL_ROW:hd * DIL_ROW + LANES])
        s = jnp.where(first, NEG, s + bias_ref[hd])
        m = jnp.max(s, axis=-1, keepdims=True)
        p = jnp.exp(s - m)
        l = jnp.sum(p, axis=-1, keepdims=True)
        o_ref[0, :, hd * DIL_ROW:(hd + 1) * DIL_ROW] = _dot(
            (p / l).astype(BF16), kcat[:, hd * DIL_ROW:(hd + 1) * DIL_ROW])
        ms.append(m)
        ls.append(l)
    ml_ref[0] = _cols(ms + ls, LANES)


def _band_attn(q, kv, bias, tq):
    s, l, _ = q.shape
    return pl.pallas_call(
        functools.partial(_band_kernel, tq), grid=(s, l // tq),
        in_specs=[pl.BlockSpec((1, tq, DIL_HEADS * LANES), lambda b, i: (b, i, 0)),
                  pl.BlockSpec((1, tq, DIL_SLAB), lambda b, i: (b, jnp.maximum(i - 1, 0), 0)),
                  pl.BlockSpec((1, tq, DIL_SLAB), lambda b, i: (b, i, 0)),
                  _full(bias.shape)],
        out_specs=[pl.BlockSpec((1, tq, DIL_SLAB), lambda b, i: (b, i, 0)),
                   pl.BlockSpec((1, tq, LANES), lambda b, i: (b, i, 0))],
        out_shape=[jax.ShapeDtypeStruct((s, l, DIL_SLAB), F32),
                   jax.ShapeDtypeStruct((s, l, LANES), F32)],
        compiler_params=_cparams("parallel", "parallel"), name="dil_band_attn",
    )(q, kv, kv, bias)


def _merge_kernel(rates, *refs):
    ng = len(rates)
    o_refs, ml_refs, out_ref = refs[:ng], refs[ng:2 * ng], refs[2 * ng]
    scr = refs[2 * ng + 1:]
    nch = DIL_SLAB // LANES
    chunks, mls = [], []
    for g, d in enumerate(rates):
        if d == 1:
            chunks.append([o_refs[g][0, 0, :, c * LANES:(c + 1) * LANES] for c in range(nch)])
            mls.append(ml_refs[g][0, 0])
        else:
            o_scr, ml_scr = scr[2 * g], scr[2 * g + 1]
            tm = ml_scr.shape[0]
            for r in range(d):
                rows = pl.ds(r, tm // d, stride=d)
                for c in range(nch):
                    o_scr[c, rows, :] = o_refs[g][0, r, :, c * LANES:(c + 1) * LANES]
                ml_scr[rows, :] = ml_refs[g][0, r]
            chunks.append([o_scr[c] for c in range(nch)])
            mls.append(ml_scr[...])
    m_all = functools.reduce(jnp.maximum, [ml[:, :DIL_HEADS] for ml in mls])
    es = [ml[:, DIL_HEADS:2 * DIL_HEADS] * jnp.exp(ml[:, :DIL_HEADS] - m_all) for ml in mls]
    tot = functools.reduce(jnp.add, es)
    ws = [e / tot for e in es]
    for c in range(nch):
        hd = c // (DIL_ROW // LANES)
        acc = None
        for g in range(ng):
            term = chunks[g][c] * ws[g][:, hd:hd + 1]
            acc = term if acc is None else acc + term
        out_ref[0, :, c * LANES:(c + 1) * LANES] = acc.astype(BF16)


def _merge(os_, mls, rates, n, t, tm):
    specs = [pl.BlockSpec((1, d, tm // d, DIL_SLAB), lambda b, i: (b, 0, i, 0)) for d in rates] \
        + [pl.BlockSpec((1, d, tm // d, LANES), lambda b, i: (b, 0, i, 0)) for d in rates]
    scratch = []
    for d in rates:
        scratch += [pltpu.VMEM((DIL_SLAB // LANES, tm, LANES), F32), pltpu.VMEM((tm, LANES), F32)]
    return pl.pallas_call(
        functools.partial(_merge_kernel, tuple(rates)), grid=(n, t // tm),
        in_specs=specs,
        out_specs=pl.BlockSpec((1, tm, DIL_SLAB), lambda b, i: (b, i, 0)),
        out_shape=jax.ShapeDtypeStruct((n, t, DIL_SLAB), BF16),
        scratch_shapes=scratch,
        compiler_params=_cparams("parallel", "parallel"), name="dil_merge",
    )(*os_, *mls)


def _dil_sample_kernel(d, wb, t_s, q_ref, new_ref, buf_ref, bias_ref, o_ref, ml_ref, st_ref, kf_scr):
    nk = DIL_KEYS - 1
    nch = DIL_SLAB // LANES
    new = new_ref[0]
    st_ref[0, 0:wb - t_s, :] = buf_ref[0, t_s:wb, :]
    st_ref[0, wb - t_s:wb, :] = new
    for c in range(nch):
        kf_scr[c, 0:wb, :] = buf_ref[0, :, c * LANES:(c + 1) * LANES]
        kf_scr[c, wb:wb + t_s, :] = new[:, c * LANES:(c + 1) * LANES]
    newf = new.astype(BF16).astype(F32)
    head_of_lane = lax.broadcasted_iota(jnp.int32, (8, DIL_SLAB), 1) // DIL_ROW
    head_of_row = lax.broadcasted_iota(jnp.int32, (8, DIL_SLAB), 0)
    pick = head_of_lane == head_of_row
    bias = bias_ref[...]
    qall = q_ref[0].astype(F32)
    o_rows, m_cols, l_cols = [], [], []
    for i in range(t_s):
        rows = pl.ds(i, nk, stride=d) if d > 1 else pl.ds(i, nk)
        ki = jnp.concatenate([kf_scr[c, rows, :] for c in range(nch)], axis=1).astype(BF16)
        qbd = jnp.where(pick, jnp.broadcast_to(qall[i:i + 1, :], (8, DIL_SLAB)), 0.0)
        s = _dot_nt(qbd.astype(BF16), ki) + bias[:, :nk]
        s_self = jnp.sum(qbd * newf[i:i + 1, :], axis=-1, keepdims=True) + bias[:, nk:nk + 1]
        m = jnp.maximum(jnp.max(s, axis=-1, keepdims=True), s_self)
        p = jnp.exp(s - m)
        p_self = jnp.exp(s_self - m)
        l = jnp.sum(p, axis=-1, keepdims=True) + p_self
        o = _dot((p / l).astype(BF16), ki) + (p_self / l).astype(BF16).astype(F32) * newf[i:i + 1, :]
        o = jnp.sum(jnp.where(pick, o, 0.0), axis=0, keepdims=True)
        o_rows.append(o)
        m_cols.append(m)
        l_cols.append(l)
    o_ref[0] = jnp.concatenate(o_rows, axis=0)
    ml_ref[0] = jnp.concatenate([_cols(m_cols, t_s), _cols(l_cols, t_s)], axis=0)


def _dil_sample(qg, newg, buf, bias, d):
    b, t_s, _ = qg.shape
    wb = buf.shape[1]
    blk = lambda r, c: pl.BlockSpec((1, r, c), lambda s: (s, 0, 0))
    return pl.pallas_call(
        functools.partial(_dil_sample_kernel, d, wb, t_s), grid=(b,),
        in_specs=[blk(t_s, DIL_SLAB), blk(t_s, DIL_SLAB), blk(wb, DIL_SLAB), _full(bias.shape)],
        out_specs=[blk(t_s, DIL_SLAB), blk(16, t_s), blk(wb, DIL_SLAB)],
        out_shape=[jax.ShapeDtypeStruct((b, t_s, DIL_SLAB), F32),
                   jax.ShapeDtypeStruct((b, 16, t_s), F32),
                   jax.ShapeDtypeStruct(buf.shape, F32)],
        scratch_shapes=[pltpu.VMEM((DIL_SLAB // LANES, wb + t_s, LANES), F32)],
        compiler_params=_cparams("parallel"), name="dil_sample_attn",
    )(qg, newg, buf, bias)


def _seg_mats(width, segments):
    s = np.zeros((width, LANES), np.float32)
    inv = np.ones((1, LANES), np.float32)
    for k, (a, n) in enumerate(segments):
        s[a:a + n, k] = 1.0
        inv[0, k] = 1.0 / n
    return jnp.asarray(s, BF16), jnp.asarray(s.T.copy(), BF16), jnp.asarray(inv)


def _pad_heads(wm, heads, dim, slab, offset=0):
    r = wm.shape[0]
    out = jnp.zeros((r, heads, slab), wm.dtype)
    out = out.at[:, :, offset:offset + dim].set(wm.reshape(r, heads, dim))
    return out.reshape(r, heads * slab)


def _mem_q_tables(g_qn_mem):
    seg = [(hd * LANES, MEM_DIM) for hd in range(MEM_HEADS)]
    sm, smt, invm = _seg_mats(MEM_W, seg)
    gm = _pad_heads(jnp.tile(g_qn_mem, MEM_HEADS)[None, :], MEM_HEADS, MEM_DIM, LANES) * MEM_SCALE
    return dict(Sm=sm, SmT=smt, invm=invm, gm=gm)


def _rope_cs(pos):
    inv = ROPE_THETA ** (-jnp.arange(0, ROPE_DIM, 2, dtype=F32) / ROPE_DIM)
    ang = pos.astype(F32)[:, None] * inv[None, :]
    return jnp.cos(ang), jnp.sin(ang)


def _mla_tables(pos, g_qn_nope, g_qn_pe, g_kn_pe):
    cos, sin = _rope_cs(pos)
    t = pos.shape[0]
    hr = ROPE_DIM // 2
    ones = jnp.ones((t, 1), F32)
    cslab = jnp.concatenate([ones * g_qn_nope[None, :], cos * g_qn_pe[None, :hr], cos * g_qn_pe[None, hr:],
                             jnp.zeros((t, LANES - NOPE_DIM - ROPE_DIM), F32)], axis=1) * MLA_SCALE
    sslab = jnp.concatenate([jnp.zeros((t, NOPE_DIM), F32), -sin * g_qn_pe[None, hr:], sin * g_qn_pe[None, :hr],
                             jnp.zeros((t, LANES - NOPE_DIM - ROPE_DIM), F32)], axis=1) * MLA_SCALE
    tk = jnp.concatenate([cos * g_kn_pe[None, :hr], cos * g_kn_pe[None, hr:],
                          -sin * g_kn_pe[None, hr:], sin * g_kn_pe[None, :hr],
                          jnp.zeros((t, LANES - 2 * ROPE_DIM), F32)], axis=1)
    return jnp.tile(cslab, (1, MLA_HEADS)), jnp.tile(sslab, (1, MLA_HEADS)), tk


def _prep_mla(i, j, g_attn, w_o, w_in_a, g_q_a, w_q_b, g_kv_a, w_uk, w_uv, g_qn_mem):
    d = w_in_a.shape[1]
    hr = ROPE_DIM // 2
    wi = w_in_a[j]
    o1, o2, o3 = Q_LORA, Q_LORA + KV_LORA, Q_LORA + KV_LORA + ROPE_DIM
    kpe = wi[:, o2:o3]
    w_in = jnp.concatenate([wi[:, :o2], _pad_heads(wi[:, o3:], MEM_HEADS, MEM_DIM, LANES),
                            kpe, jnp.concatenate([kpe[:, hr:], kpe[:, :hr]], axis=1),
                            jnp.zeros((d, LANES - 2 * ROPE_DIM), F32)], axis=1).astype(BF16)
    hq = NOPE_DIM + ROPE_DIM
    wq3 = w_q_b[j].reshape(Q_LORA, MLA_HEADS, hq)
    wq = _pad_heads(w_q_b[j], MLA_HEADS, hq, LANES).astype(BF16)
    sw = jnp.concatenate([wq3[:, :, NOPE_DIM + hr:], wq3[:, :, NOPE_DIM:NOPE_DIM + hr]], axis=2)
    wq_sw = _pad_heads(sw.reshape(Q_LORA, MLA_HEADS * ROPE_DIM), MLA_HEADS, ROPE_DIM, LANES, NOPE_DIM).astype(BF16)
    seg = []
    for hd in range(MLA_HEADS):
        seg += [(hd * LANES, NOPE_DIM), (hd * LANES + NOPE_DIM, ROPE_DIM)]
    s, st, inv = _seg_mats(MLA_HEADS * LANES, seg)
    wk_c = _pad_heads(w_uk[j].reshape(KV_LORA, MLA_HEADS * NOPE_DIM), MLA_HEADS, NOPE_DIM, LANES).astype(BF16)
    eye = np.zeros((LANES, MLA_HEADS, LANES), np.float32)
    for r in range(ROPE_DIM):
        eye[r, :, NOPE_DIM + r] = 1.0
    wk_p = jnp.asarray(eye.reshape(LANES, MLA_HEADS * LANES), BF16)
    wv3 = w_uv[j]
    wv = jnp.zeros((KV_LORA, MLA_HEADS, LANES), F32)
    wv = wv.at[:, 0::2, :MLA_V_DIM].set(wv3[:, 0::2]).at[:, 1::2, MLA_V_DIM:].set(wv3[:, 1::2])
    wv = wv.reshape(KV_LORA, MLA_HEADS * LANES).astype(BF16)
    wcat = jnp.zeros((MLA_HEADS, LANES, MLA_QCAT), F32)
    wcat = wcat.at[:, :NOPE_DIM, :KV_LORA].set(jnp.transpose(w_uk[j], (1, 2, 0)))
    wcat = wcat.at[:, NOPE_DIM:hq, KV_LORA:MLA_ROW].set(jnp.eye(ROPE_DIM, dtype=F32)[None])
    uvp = jnp.zeros((MLA_HEADS // 2, 2 * KV_LORA, LANES), F32)
    uvp = uvp.at[:, :KV_LORA, :MLA_V_DIM].set(jnp.transpose(wv3[:, 0::2], (1, 0, 2)))
    uvp = uvp.at[:, KV_LORA:, MLA_V_DIM:].set(jnp.transpose(wv3[:, 1::2], (1, 0, 2)))
    ntok = MLA_HEADS * MLA_V_DIM
    w = dict(g_attn=g_attn[i][None], w_in=w_in, g_q_a=g_q_a[j][None], wq=wq, wq_sw=wq_sw,
             g_kv_a=g_kv_a[j][None], S=s, ST=st, inv=inv, wk_c=wk_c, wk_p=wk_p, wv=wv,
             wcat=wcat.astype(BF16), uv_pairs=uvp.astype(BF16),
             wo_tok=w_o[i][:ntok].astype(BF16),
             wo_mem=_pad_rows(w_o[i][ntok:], MEM_HEADS, MEM_DIM, LANES, MEM_DIM).astype(BF16))
    w.update(_mem_q_tables(g_qn_mem[i]))
    return w


def _pad_rows(wm, heads, dim, slab, offset):
    c = wm.shape[1]
    out = jnp.zeros((heads, slab, c), wm.dtype)
    out = out.at[:, offset:offset + dim].set(wm.reshape(heads, dim, c))
    return out.reshape(heads * slab, c)


def _prep_dil(i, j, g_attn, w_o, w_in_b, g_qn_b, g_kn_b, g_qn_mem):
    wi = w_in_b[j]
    gh = N_GROUPS * DIL_HEADS
    nqk = gh * DIL_QK
    wq = _pad_heads(wi[:, :nqk], gh, DIL_QK, LANES)
    wk = wi[:, nqk:2 * nqk].reshape(-1, gh, DIL_QK)
    wv = wi[:, 2 * nqk:2 * nqk + gh * DIL_V].reshape(-1, gh, DIL_V)
    wkv = jnp.concatenate([wk, wv], axis=2).reshape(-1, gh * DIL_ROW)
    wmq = _pad_heads(wi[:, 2 * nqk + gh * DIL_V:], MEM_HEADS, MEM_DIM, LANES)
    w_in = jnp.concatenate([wq, wkv, wmq], axis=1).astype(BF16)
    sq, sqt, invq = _seg_mats(QW, [(k * LANES, DIL_QK) for k in range(gh)])
    sk, skt, invk = _seg_mats(KVW, [(k * DIL_ROW, DIL_QK) for k in range(gh)])
    gq = _pad_heads(jnp.repeat(g_qn_b[j], DIL_HEADS, axis=0).reshape(1, nqk), gh, DIL_QK, LANES) * DIL_SCALE
    gk = _pad_heads(jnp.repeat(g_kn_b[j], DIL_HEADS, axis=0).reshape(1, nqk), gh, DIL_QK, DIL_ROW)
    vmask = _pad_heads(jnp.ones((1, gh * DIL_V), F32), gh, DIL_V, DIL_ROW, DIL_QK)
    ntok = DIL_HEADS * DIL_V
    w = dict(g_attn=g_attn[i][None], w_in=w_in, Sq=sq, SqT=sqt, invq=invq, gq=gq,
             Sk=sk, SkT=skt, invk=invk, gk=gk, vmask=vmask,
             wo_tok=_pad_rows(w_o[i][:ntok], DIL_HEADS, DIL_V, DIL_ROW, DIL_QK).astype(BF16),
             wo_mem=_pad_rows(w_o[i][ntok:], MEM_HEADS, MEM_DIM, LANES, MEM_DIM).astype(BF16))
    w.update(_mem_q_tables(g_qn_mem[i]))
    return w


def _prep_mem(i, g_mem, w_mem_kv, g_kn_mem):
    sk, skt, invk = _seg_mats(MEM_W, [(hd * LANES, MEM_DIM) for hd in range(MEM_HEADS)])
    gk = _pad_heads(jnp.tile(g_kn_mem[i], MEM_HEADS)[None, :], MEM_HEADS, MEM_DIM, LANES)
    vmask = _pad_heads(jnp.ones((1, MEM_HEADS * MEM_DIM), F32), MEM_HEADS, MEM_DIM, LANES, MEM_DIM)
    return dict(g_mem=g_mem[i][None], w_mem=w_mem_kv[i].astype(BF16), Sk=sk, SkT=skt, invk=invk, gk=gk, vmask=vmask)


def _prep_ffn(i, g_ffn, w_up, conv_w, conv_b, w_down):
    return dict(g_ffn=g_ffn[i][None], w_up=w_up[i].astype(BF16), conv_w=conv_w[i],
                conv_b=conv_b[i][None], w_down=w_down[i].astype(BF16))


def _rel_bucket(dist):
    max_exact = N_BUCKETS // 2
    dd = jnp.maximum(dist.astype(F32), 1.0)
    large = max_exact + (jnp.log(dd / max_exact) / math.log(MAX_DISTANCE / max_exact)
                         * (N_BUCKETS - max_exact)).astype(jnp.int32)
    large = jnp.minimum(large, N_BUCKETS - 1)
    return jnp.where(dist < max_exact, dist, large)


def _dil_bias_tables(rel_bias, tq):
    band, samp = [], []
    nk = DIL_KEYS - 1
    for g in range(N_GROUPS):
        bk = _rel_bucket(DIL_RATES[g] * jnp.arange(DIL_KEYS, dtype=jnp.int32))
        b = rel_bias[bk][:, g * DIL_HEADS:(g + 1) * DIL_HEADS].T.astype(F32)
        jj = np.arange(tq)[:, None] + tq - np.arange(2 * tq)[None, :]
        ok = (jj >= 0) & (jj <= nk)
        band.append(jnp.where(jnp.asarray(ok)[None], b[:, np.clip(jj, 0, nk)], NEG))
        rows = jnp.concatenate([b[:, nk - np.arange(nk)], b[:, :1], jnp.zeros((DIL_HEADS, LANES - 1), F32)], axis=1)
        samp.append(jnp.concatenate([rows, jnp.zeros_like(rows)], axis=0))
    return band, samp


def kernel(x_prompt, x_sample, cache_mla, state_win0, state_win1, state_win2, cache_mem, state_conv, page_table, mem_prompt, rel_bias, g_attn, w_o, w_in_a, g_q_a, w_q_b, g_kv_a, w_uk, w_uv, g_qn_nope, g_qn_pe, g_kn_pe, w_in_b, g_qn_b, g_kn_b, g_mem, w_mem_kv, g_qn_mem, g_kn_mem, g_ffn, w_up, conv_w, conv_b, w_down):
    state_wins = (state_win0, state_win1, state_win2)
    n, t_p, d = x_prompt.shape
    b, t_s, _ = x_sample.shape
    n_mem = mem_prompt.shape[1]
    d_ff = w_down.shape[1]
    past = page_table.shape[1] * cache_mla.shape[2]
    mp, ms = n * t_p, b * t_s
    tm_p = min(256, t_p)
    tm_s = min(256, ms)
    tq_band = DIL_KEYS - 1
    assert t_s >= 2 and t_p % tm_p == 0 and ms % tm_s == 0 and tm_s % t_s == 0
    assert all(sw.shape[2] == wdw and wdw == (DIL_KEYS - 1) * r
               for sw, wdw, r in zip(state_wins, DIL_WINDOWS, DIL_RATES))

    xp = x_prompt.reshape(mp, d)
    xs = x_sample.reshape(ms, d)
    mem2 = mem_prompt.reshape(n * n_mem, d)
    pos_p = jnp.tile(jnp.arange(t_p), n)
    pos_s = jnp.tile(past + jnp.arange(t_s), b)
    band_bias, samp_bias = _dil_bias_tables(rel_bias, tq_band)

    mem_out, conv_p_out, conv_s_out = [], [], []
    for i in range(2):
        j = i // 2
        if i % 2 == 0:
            w = _prep_mla(i, j, g_attn, w_o, w_in_a, g_q_a, w_q_b, g_kv_a, w_uk, w_uv, g_qn_mem)
            cq, sq, tk = _mla_tables(pos_p, g_qn_nope[j], g_qn_pe[j], g_kn_pe[j])
            q_p, rows_p, mq_p, k_p, v_p = _proj_mla(xp, True, tm_p, w, cq, sq, tk)
            cq, sq, tk = _mla_tables(pos_s, g_qn_nope[j], g_qn_pe[j], g_kn_pe[j])
            qc_s, rows_s, mq_s = _proj_mla(xs, False, tm_s, w, cq, sq, tk)
            hw = MLA_HEADS * LANES
            tok_p = _mla_flash(q_p.reshape(n, t_p, hw), k_p.reshape(n, t_p, hw), v_p.reshape(n, t_p, hw),
                               min(512, t_p)).reshape(mp, -1)
            q3 = qc_s.reshape(b, t_s, MLA_HEADS, MLA_QCAT).transpose(0, 2, 1, 3).reshape(b, MLA_HEADS * t_s, MLA_QCAT)
            ctx = _mla_sample_attn(q3, rows_s.reshape(b, t_s, MLA_ROW), cache_mla[j], page_table,
                                   min(16, page_table.shape[1]))
            ctx = ctx.reshape(b, MLA_HEADS, t_s, KV_LORA).transpose(0, 2, 1, 3).reshape(ms, MLA_HEADS * KV_LORA)
            tok_s = _mla_uv(ctx, w['uv_pairs'], tm_s)
            mla_rows_p = rows_p.reshape(1, n, t_p, MLA_ROW)
            mla_rows_s = rows_s.reshape(1, b, t_s, MLA_ROW)
        else:
            w = _prep_dil(i, j, g_attn, w_o, w_in_b, g_qn_b, g_kn_b, g_qn_mem)
            res = _proj_dil(xp.reshape(n, t_p, d), True, tm_p, w)
            kv_p, mq_p = res[0], res[1].reshape(mp, MEM_W)
            os_, mls = [], []
            for g, dil in enumerate(DIL_RATES):
                l = t_p // dil
                o, ml = _band_attn(res[2 + g].reshape(n * dil, l, DIL_HEADS * LANES),
                                   res[5 + g].reshape(n * dil, l, DIL_SLAB), band_bias[g], tq_band)
                os_.append(o.reshape(n, dil, l, DIL_SLAB))
                mls.append(ml.reshape(n, dil, l, LANES))
            tok_p = _merge(os_, mls, DIL_RATES, n, t_p, tm_p).reshape(mp, DIL_SLAB)
            win_p = [kv_p[:, t_p - min(wdw, t_p):, g * DIL_SLAB:(g + 1) * DIL_SLAB]
                     .reshape(1, n, min(wdw, t_p), DIL_HEADS, DIL_ROW) for g, wdw in enumerate(DIL_WINDOWS)]
            q_s, kv_s, mq_s = _proj_dil(xs.reshape(b, t_s, d), False, tm_s, w)
            q_s = q_s.reshape(b, t_s, N_GROUPS, DIL_HEADS, LANES)
            q_s = jnp.pad(q_s, ((0, 0),) * 4 + ((0, DIL_ROW - LANES),))
            kv_s = kv_s.reshape(b, t_s, N_GROUPS, DIL_SLAB)
            os_, mls, win_s = [], [], []
            for g, dil in enumerate(DIL_RATES):
                buf = state_wins[g][j]
                wb = buf.shape[1]
                o, ml, st = _dil_sample(q_s[:, :, g].reshape(b, t_s, DIL_SLAB), kv_s[:, :, g],
                                        buf.reshape(b, wb, DIL_SLAB), samp_bias[g], dil)
                os_.append(o.reshape(1, 1, ms, DIL_SLAB))
                ml = jnp.concatenate([ml[:, :DIL_HEADS], ml[:, 8:8 + DIL_HEADS]], axis=1).transpose(0, 2, 1)
                ml = jnp.pad(ml.reshape(ms, 2 * DIL_HEADS), ((0, 0), (0, LANES - 2 * DIL_HEADS)))
                mls.append(ml.reshape(1, 1, ms, LANES))
                win_s.append(st.reshape(1, b, wb, DIL_HEADS, DIL_ROW))
            tok_s = _merge(os_, mls, (1, 1, 1), 1, ms, tm_s).reshape(ms, DIL_SLAB)

        wm = _prep_mem(i, g_mem, w_mem_kv, g_kn_mem)
        mkv = _mem_kv(mem2, wm, min(256, n * n_mem))
        mem_out.append(mkv.reshape(n, n_mem, MEM_HEADS, 2 * MEM_DIM))
        mo_p = _mem_attn(mq_p.reshape(n, t_p, MEM_W), mkv.reshape(n, n_mem, MEM_W), min(512, t_p)).reshape(mp, MEM_W)
        mo_s = _mem_attn(mq_s.reshape(b, t_s, MEM_W), cache_mem[i].reshape(b, n_mem, MEM_W), t_s).reshape(ms, MEM_W)
        xp = _out_proj(xp, tok_p, mo_p, w['wo_tok'], w['wo_mem'], tm_p)
        xs = _out_proj(xs, tok_s, mo_s, w['wo_tok'], w['wo_mem'], tm_s)

        wf = _prep_ffn(i, g_ffn, w_up, conv_w, conv_b, w_down)
        xp, tails = _ffn(xp, t_p, wf, tm_p)
        conv_p_out.append(tails.reshape(n, t_p // tm_p, 8, d_ff)[:, -1, 6:, :])
        prev = state_conv[i]
        zeros = jnp.zeros((b, t_s - 2, d_ff), F32)
        p2 = jnp.concatenate([prev, zeros], axis=1).reshape(ms, d_ff)
        p1 = jnp.concatenate([prev[:, 1:], zeros, jnp.zeros((b, 1, d_ff), F32)], axis=1).reshape(ms, d_ff)
        xs, gates = _ffn(xs, t_s, wf, tm_s, prev=(p1, p2))
        conv_s_out.append(gates.reshape(b, t_s, d_ff)[:, t_s - 2:, :])

    return (xp.reshape(n, t_p, d), xs.reshape(b, t_s, d), mla_rows_p, mla_rows_s,
            win_p[0], win_p[1], win_p[2], win_s[0], win_s[1], win_s[2],
            jnp.stack(mem_out), jnp.stack(conv_p_out), jnp.stack(conv_s_out))
```

```python
import functools
import math

import jax
import jax.numpy as jnp
import numpy as np
from jax import lax
from jax.experimental import pallas as pl
from jax.experimental.pallas import tpu as pltpu

F32 = jnp.float32
BF16 = jnp.bfloat16

EPS = 1e-6
LANES = 128
VMEM_LIMIT = 48 * 1024 * 1024

MLA_HEADS = 12
NOPE_DIM = 64
ROPE_DIM = 32
MLA_V_DIM = 64
Q_LORA = 256
KV_LORA = 256
MLA_ROW = KV_LORA + ROPE_DIM
ROPE_THETA = 10000.0
MLA_SCALE = 1.0 / math.sqrt(NOPE_DIM + ROPE_DIM)
MLA_QCAT = 384

DIL_WINDOWS = (128, 512, 2048)
DIL_RATES = (1, 4, 16)
N_GROUPS = 3
DIL_HEADS = 4
DIL_QK = 64
DIL_V = 192
DIL_KEYS = DIL_WINDOWS[0] // DIL_RATES[0] + 1
DIL_SCALE = 1.0 / math.sqrt(DIL_QK)
DIL_ROW = DIL_QK + DIL_V
DIL_SLAB = DIL_HEADS * DIL_ROW

MEM_HEADS = 4
MEM_DIM = 64
MEM_SCALE = 1.0 / math.sqrt(MEM_DIM)
MEM_W = MEM_HEADS * LANES

N_BUCKETS = 32
MAX_DISTANCE = 2048
NEG = -1e30


def _cparams(*sem):
    return pltpu.CompilerParams(dimension_semantics=sem, vmem_limit_bytes=VMEM_LIMIT)


def _dot(a, b):
    return jnp.dot(a, b, preferred_element_type=F32)


def _dot_nt(a, b):
    return lax.dot_general(a, b, (((1,), (1,)), ((), ())), preferred_element_type=F32)


def _split_dot(x, w):
    hi = x.astype(BF16)
    lo = (x - hi.astype(F32)).astype(BF16)
    return _dot(hi, w) + _dot(lo, w)


def _rms_rows(x, g):
    return x * lax.rsqrt(jnp.mean(x * x, axis=-1, keepdims=True) + EPS) * g


def _seg_scale(x, s_ref, st_ref, inv_ref):
    ss = _split_dot(x * x, s_ref[...])
    r = lax.rsqrt(ss * inv_ref[...] + EPS)
    return _split_dot(r, st_ref[...])


def _cols(vals, width):
    rows = vals[0].shape[0]
    lane = lax.broadcasted_iota(jnp.int32, (rows, width), 1)
    out = jnp.zeros((rows, width), F32)
    for k, v in enumerate(vals):
        out = jnp.where(lane == k, v, out)
    return out


def _full(shape):
    return pl.BlockSpec(shape, lambda *_: (0,) * len(shape))


def _proj_mla_kernel(prompt, x_ref, ga_ref, win_ref, gqa_ref, wq_ref, wqs_ref, gkv_ref,
                     cq_ref, sq_ref, tk_ref, s_ref, st_ref, inv_ref,
                     sm_ref, smt_ref, invm_ref, gm_ref, *rest):
    if prompt:
        wkc_ref, wkp_ref, wv_ref, q_out, rows_out, mq_out, k_out, v_out = rest
    else:
        wcat_ref, q_out, rows_out, mq_out = rest
    h = _rms_rows(x_ref[...], ga_ref[...]).astype(BF16)
    z = _dot(h, win_ref[...])
    qa = _rms_rows(z[:, :Q_LORA], gqa_ref[...]).astype(BF16)
    q = _dot(qa, wq_ref[...])
    qs = _dot(qa, wqs_ref[...])
    scale = _seg_scale(q, s_ref, st_ref, inv_ref)
    cq = jnp.concatenate([cq_ref[...]] * MLA_HEADS, axis=1)
    sq = jnp.concatenate([sq_ref[...]] * MLA_HEADS, axis=1)
    qf = ((q * cq + qs * sq) * scale).astype(BF16)
    c = _rms_rows(z[:, Q_LORA:Q_LORA + KV_LORA], gkv_ref[...])
    zk = z[:, 1024:1152]
    kp = zk[:, :ROPE_DIM]
    rk = lax.rsqrt(jnp.mean(kp * kp, axis=-1, keepdims=True) + EPS)
    t = zk * tk_ref[...] * rk
    kr = t + pltpu.roll(t, LANES - ROPE_DIM, 1)
    rows_out[:, :KV_LORA] = c
    rows_out[:, KV_LORA:] = kr[:, :ROPE_DIM]
    mq = z[:, 512:1024]
    mscale = _seg_scale(mq, sm_ref, smt_ref, invm_ref)
    mq_out[...] = (mq * mscale * gm_ref[...]).astype(BF16)
    if prompt:
        q_out[...] = qf
        cb = c.astype(BF16)
        k_out[...] = (_dot(cb, wkc_ref[...]) + _dot(kr.astype(BF16), wkp_ref[...])).astype(BF16)
        v_out[...] = _dot(cb, wv_ref[...]).astype(BF16)
    else:
        for hd in range(MLA_HEADS):
            q_out[:, hd * MLA_QCAT:(hd + 1) * MLA_QCAT] = _dot(
                qf[:, hd * LANES:(hd + 1) * LANES], wcat_ref[hd]).astype(BF16)


def _proj_mla(x, prompt, tm, w, cq, sq, tk):
    m, d = x.shape
    hw = MLA_HEADS * LANES
    row = lambda width: pl.BlockSpec((tm, width), lambda i: (i, 0))
    args = [x, w['g_attn'], w['w_in'], w['g_q_a'], w['wq'], w['wq_sw'], w['g_kv_a'],
            cq, sq, tk, w['S'], w['ST'], w['inv'], w['Sm'], w['SmT'], w['invm'], w['gm']]
    specs = [row(d)] + [_full(a.shape) for a in args[1:7]] + [row(LANES), row(LANES), row(LANES)] \
        + [_full(a.shape) for a in args[10:]]
    if prompt:
        extra = [w['wk_c'], w['wk_p'], w['wv']]
        outs = [jax.ShapeDtypeStruct((m, hw), BF16), jax.ShapeDtypeStruct((m, MLA_ROW), F32),
                jax.ShapeDtypeStruct((m, MEM_W), BF16), jax.ShapeDtypeStruct((m, hw), BF16),
                jax.ShapeDtypeStruct((m, hw), BF16)]
        ospecs = [row(hw), row(MLA_ROW), row(MEM_W), row(hw), row(hw)]
    else:
        extra = [w['wcat']]
        outs = [jax.ShapeDtypeStruct((m, MLA_HEADS * MLA_QCAT), BF16),
                jax.ShapeDtypeStruct((m, MLA_ROW), F32), jax.ShapeDtypeStruct((m, MEM_W), BF16)]
        ospecs = [row(MLA_HEADS * MLA_QCAT), row(MLA_ROW), row(MEM_W)]
    return pl.pallas_call(
        functools.partial(_proj_mla_kernel, prompt),
        grid=(m // tm,),
        in_specs=specs + [_full(a.shape) for a in extra],
        out_specs=ospecs, out_shape=outs,
        compiler_params=_cparams("parallel"),
        name="proj_mla_prompt" if prompt else "proj_mla_sample",
    )(*args, *extra)


def _flash_kernel(tq, q_ref, k_ref, v_ref, o_ref, m_scr, l_scr, acc_scr):
    i, j = pl.program_id(1), pl.program_id(2)

    @pl.when(j == 0)
    def _():
        m_scr[...] = jnp.full(m_scr.shape, NEG, F32)
        l_scr[...] = jnp.zeros(l_scr.shape, F32)
        acc_scr[...] = jnp.zeros(acc_scr.shape, F32)

    def step(masked):
        if masked:
            keep = lax.broadcasted_iota(jnp.int32, (tq, tq), 1) <= lax.broadcasted_iota(jnp.int32, (tq, tq), 0)
        for hd in range(MLA_HEADS):
            sl = slice(hd * LANES, (hd + 1) * LANES)
            s = _dot_nt(q_ref[0, :, sl], k_ref[0, :, sl])
            if masked:
                s = jnp.where(keep, s, NEG)
            m_old = m_scr[hd]
            m_new = jnp.maximum(m_old, jnp.max(s, axis=-1, keepdims=True))
            alpha = jnp.exp(m_old - m_new)
            p = jnp.exp(s - m_new)
            l_scr[hd] = alpha * l_scr[hd] + jnp.sum(p, axis=-1, keepdims=True)
            acc_scr[hd] = alpha * acc_scr[hd] + _dot(p.astype(BF16), v_ref[0, :, sl])
            m_scr[hd] = m_new

    @pl.when(j < i)
    def _():
        step(False)

    @pl.when(j == i)
    def _():
        step(True)
        for pr in range(MLA_HEADS // 2):
            e, o = 2 * pr, 2 * pr + 1
            o_ref[0, :, pr * LANES:(pr + 1) * LANES] = (
                acc_scr[e] / l_scr[e] + acc_scr[o] / l_scr[o]).astype(BF16)


def _mla_flash(q, k, v, tq):
    n, t, hw = q.shape
    nq = t // tq
    return pl.pallas_call(
        functools.partial(_flash_kernel, tq),
        grid=(n, nq, nq),
        in_specs=[pl.BlockSpec((1, tq, hw), lambda b, i, j: (b, i, 0)),
                  pl.BlockSpec((1, tq, hw), lambda b, i, j: (b, jnp.minimum(i, j), 0)),
                  pl.BlockSpec((1, tq, hw), lambda b, i, j: (b, jnp.minimum(i, j), 0))],
        out_specs=pl.BlockSpec((1, tq, MLA_HEADS * MLA_V_DIM), lambda b, i, j: (b, i, 0)),
        out_shape=jax.ShapeDtypeStruct((n, t, MLA_HEADS * MLA_V_DIM), BF16),
        scratch_shapes=[pltpu.VMEM((MLA_HEADS, tq, 1), F32), pltpu.VMEM((MLA_HEADS, tq, 1), F32),
                        pltpu.VMEM((MLA_HEADS, tq, LANES), F32)],
        compiler_params=_cparams("parallel", "parallel", "arbitrary"),
        name="mla_flash_prompt",
    )(q, k, v)


def _mla_sample_kernel(npg, t_s, pt_ref, q_ref, new_ref, *rest):
    pages, (o_ref, m_scr, l_scr, acc_scr) = rest[:npg], rest[npg:]
    c = pl.program_id(1)

    @pl.when(c == 0)
    def _():
        m_scr[...] = jnp.full(m_scr.shape, NEG, F32)
        l_scr[...] = jnp.zeros(l_scr.shape, F32)
        acc_scr[...] = jnp.zeros(acc_scr.shape, F32)

    q = q_ref[0][:, :MLA_ROW]

    def update(s, pv):
        m_old = m_scr[...]
        m_new = jnp.maximum(m_old, jnp.max(s, axis=-1, keepdims=True))
        alpha = jnp.exp(m_old - m_new)
        p = jnp.exp(s - m_new)
        l_scr[...] = alpha * l_scr[...] + jnp.sum(p, axis=-1, keepdims=True)
        acc_scr[...] = alpha * acc_scr[...] + pv(p.astype(BF16))
        m_scr[...] = m_new

    kt = jnp.concatenate([pg[0].astype(BF16) for pg in pages], axis=1)
    update(_dot(q, kt), lambda p: _dot_nt(p, kt[:KV_LORA]))

    @pl.when(c == pl.num_programs(1) - 1)
    def _():
        nb = new_ref[0].astype(BF16)
        s = _dot_nt(q, nb)
        rows = q.shape[0]
        tq = lax.broadcasted_iota(jnp.int32, (rows, t_s), 0) % t_s
        s = jnp.where(lax.broadcasted_iota(jnp.int32, (rows, t_s), 1) <= tq, s, NEG)
        update(s, lambda p: _dot(p, nb[:, :KV_LORA]))
        o_ref[0] = (acc_scr[...] / l_scr[...]).astype(BF16)


def _mla_sample_attn(q3, rows3, pool_t, page_table, npg):
    b, rows, _ = q3.shape
    t_s = rows3.shape[1]
    n_pages = page_table.shape[1]
    psz = pool_t.shape[2]

    def page_spec(k):
        return pl.BlockSpec((1, MLA_ROW, psz), lambda s, c, pt: (pt[s * n_pages + c * npg + k], 0, 0))

    return pl.pallas_call(
        functools.partial(_mla_sample_kernel, npg, t_s),
        grid_spec=pltpu.PrefetchScalarGridSpec(
            num_scalar_prefetch=1,
            grid=(b, n_pages // npg),
            in_specs=[pl.BlockSpec((1, rows, MLA_QCAT), lambda s, c, pt: (s, 0, 0)),
                      pl.BlockSpec((1, t_s, MLA_ROW), lambda s, c, pt: (s, 0, 0))]
            + [page_spec(k) for k in range(npg)],
            out_specs=pl.BlockSpec((1, rows, KV_LORA), lambda s, c, pt: (s, 0, 0)),
            scratch_shapes=[pltpu.VMEM((rows, 1), F32), pltpu.VMEM((rows, 1), F32),
                            pltpu.VMEM((rows, KV_LORA), F32)]),
        out_shape=jax.ShapeDtypeStruct((b, rows, KV_LORA), BF16),
        compiler_params=_cparams("parallel", "arbitrary"),
        name="mla_sample_attn",
    )(page_table.reshape(-1), q3, rows3, *([pool_t] * npg))


def _uv_kernel(ctx_ref, w_ref, o_ref):
    for pr in range(MLA_HEADS // 2):
        o_ref[:, pr * LANES:(pr + 1) * LANES] = _dot(
            ctx_ref[:, pr * 2 * KV_LORA:(pr + 1) * 2 * KV_LORA], w_ref[pr]).astype(BF16)


def _mla_uv(ctx, w_pairs, tm):
    m = ctx.shape[0]
    return pl.pallas_call(
        _uv_kernel, grid=(m // tm,),
        in_specs=[pl.BlockSpec((tm, ctx.shape[1]), lambda i: (i, 0)), _full(w_pairs.shape)],
        out_specs=pl.BlockSpec((tm, MLA_HEADS * MLA_V_DIM), lambda i: (i, 0)),
        out_shape=jax.ShapeDtypeStruct((m, MLA_HEADS * MLA_V_DIM), BF16),
        compiler_params=_cparams("parallel"), name="mla_sample_uv",
    )(ctx, w_pairs)


def _mem_kv_kernel(x_ref, g_ref, w_ref, s_ref, st_ref, inv_ref, gk_ref, vm_ref, o_ref):
    h = _rms_rows(x_ref[...], g_ref[...]).astype(BF16)
    kv = _dot(h, w_ref[...])
    scale = _seg_scale(kv, s_ref, st_ref, inv_ref) * gk_ref[...] + vm_ref[...]
    o_ref[...] = kv * scale


def _mem_kv(mem2, w, tm):
    m, d = mem2.shape
    args = [mem2, w['g_mem'], w['w_mem'], w['Sk'], w['SkT'], w['invk'], w['gk'], w['vmask']]
    return pl.pallas_call(
        _mem_kv_kernel, grid=(m // tm,),
        in_specs=[pl.BlockSpec((tm, d), lambda i: (i, 0))] + [_full(a.shape) for a in args[1:]],
        out_specs=pl.BlockSpec((tm, MEM_W), lambda i: (i, 0)),
        out_shape=jax.ShapeDtypeStruct((m, MEM_W), F32),
        compiler_params=_cparams("parallel"), name="mem_kv",
    )(*args)


def _mem_attn_body(q, kv):
    outs = []
    for hd in range(MEM_HEADS):
        sl = slice(hd * LANES, (hd + 1) * LANES)
        s = _dot_nt(q[:, sl], kv[:, sl])
        p = jnp.exp(s - jnp.max(s, axis=-1, keepdims=True))
        p = p / jnp.sum(p, axis=-1, keepdims=True)
        outs.append(_dot(p.astype(BF16), kv[:, sl]))
    return jnp.concatenate(outs, axis=-1)


def _mem_attn_kernel(q_ref, kv_ref, o_ref):
    o_ref[0] = _mem_attn_body(q_ref[0], kv_ref[0].astype(BF16)).astype(BF16)


def _mem_attn(q3, kv3, tq):
    n, t, _ = q3.shape
    return pl.pallas_call(
        _mem_attn_kernel, grid=(n, t // tq),
        in_specs=[pl.BlockSpec((1, tq, MEM_W), lambda b, i: (b, i, 0)),
                  pl.BlockSpec((1, kv3.shape[1], MEM_W), lambda b, i: (b, 0, 0))],
        out_specs=pl.BlockSpec((1, tq, MEM_W), lambda b, i: (b, i, 0)),
        out_shape=jax.ShapeDtypeStruct((n, t, MEM_W), BF16),
        compiler_params=_cparams("parallel", "parallel"), name="mem_attn",
    )(q3, kv3)


def _mem_rows_kernel(sb, q_ref, kv_ref, o_ref):
    nq, nkv = q_ref.shape[1], kv_ref.shape[2]
    same = (lax.broadcasted_iota(jnp.int32, (nq, nkv), 0) % MEM_HEADS
            == lax.broadcasted_iota(jnp.int32, (nq, nkv), 1) % MEM_HEADS)
    for s in range(sb):
        kv = kv_ref[0, s].astype(BF16)
        sc = jnp.where(same, _dot_nt(q_ref[s], kv), NEG)
        p = jnp.exp(sc - jnp.max(sc, axis=-1, keepdims=True))
        p = p / jnp.sum(p, axis=-1, keepdims=True)
        o_ref[s] = _dot(p.astype(BF16), kv).astype(BF16)


def _mem_attn_rows(q3, cache4, layer, sb):
    b, nq, _ = q3.shape
    return pl.pallas_call(
        functools.partial(_mem_rows_kernel, sb), grid=(b // sb,),
        in_specs=[pl.BlockSpec((sb, nq, LANES), lambda i: (i, 0, 0)),
                  pl.BlockSpec((1, sb, cache4.shape[2], LANES), lambda i: (layer, i, 0, 0))],
        out_specs=pl.BlockSpec((sb, nq, LANES), lambda i: (i, 0, 0)),
        out_shape=jax.ShapeDtypeStruct((b, nq, LANES), BF16),
        compiler_params=_cparams("parallel"), name="mem_attn_sample",
    )(q3, cache4)


def _out_proj_kernel(x_ref, a_ref, b_ref, wa_ref, wb_ref, o_ref):
    o_ref[...] = x_ref[...] + _dot(a_ref[...], wa_ref[...]) + _dot(b_ref[...], wb_ref[...])


def _out_proj(x, a, b, wa, wb, tm):
    m, d = x.shape
    row = lambda width: pl.BlockSpec((tm, width), lambda i: (i, 0))
    return pl.pallas_call(
        _out_proj_kernel, grid=(m // tm,),
        in_specs=[row(d), row(a.shape[1]), row(b.shape[1]), _full(wa.shape), _full(wb.shape)],
        out_specs=row(d), out_shape=jax.ShapeDtypeStruct((m, d), F32),
        compiler_params=_cparams("parallel"), name="out_proj",
    )(x, a, b, wa, wb)


def _ffn_kernel(seq_tiles, t_s, d_ff, x_ref, g_ref, wu_ref, cw_ref, cb_ref, wd_ref, *rest):
    if seq_tiles:
        o_ref, st_ref, g_scr = rest
    else:
        p1_ref, p2_ref, o_ref, st_ref = rest
    x = x_ref[...]
    tm = x.shape[0]
    h = _rms_rows(x, g_ref[...]).astype(BF16)
    u = _dot(h, wu_ref[...])
    gate, val = u[:, :d_ff], u[:, d_ff:]
    if seq_tiles:
        @pl.when(pl.program_id(0) % seq_tiles == 0)
        def _():
            g_scr[0:8, :] = jnp.zeros((8, d_ff), F32)
        g_scr[8:, :] = gate
        g1 = g_scr[7:tm + 7, :]
        g2 = g_scr[6:tm + 6, :]
        tail = gate[tm - 8:, :]
        g_scr[0:8, :] = tail
        st_ref[...] = tail
    else:
        tpos = lax.broadcasted_iota(jnp.int32, (tm, 1), 0) % t_s
        g1 = jnp.where(tpos >= 1, pltpu.roll(gate, 1, 0), p1_ref[...])
        g2 = jnp.where(tpos >= 2, pltpu.roll(gate, 2, 0), p2_ref[...])
        st_ref[...] = gate
    cw = cw_ref[...]
    conv = cb_ref[...] + g2 * cw[0:1] + g1 * cw[1:2] + gate * cw[2:3]
    act = (conv * (1.0 / (1.0 + jnp.exp(-conv))) * val).astype(BF16)
    o_ref[...] = x + _dot(act, wd_ref[...])


def _ffn(x, seq_len, w, tm, prev=None):
    m, d = x.shape
    d_ff = w['w_down'].shape[0]
    row = lambda width: pl.BlockSpec((tm, width), lambda i: (i, 0))
    single = dict(pipeline_mode=pl.Buffered(1))
    wspecs = [_full(w['g_ffn'].shape),
              pl.BlockSpec(w['w_up'].shape, lambda i: (0, 0), **single),
              _full(w['conv_w'].shape), _full(w['conv_b'].shape),
              pl.BlockSpec(w['w_down'].shape, lambda i: (0, 0), **single)]
    wargs = [w['g_ffn'], w['w_up'], w['conv_w'], w['conv_b'], w['w_down']]
    if prev is None:
        seq_tiles = seq_len // tm
        return pl.pallas_call(
            functools.partial(_ffn_kernel, seq_tiles, 0, d_ff), grid=(m // tm,),
            in_specs=[row(d)] + wspecs,
            out_specs=[row(d), pl.BlockSpec((8, d_ff), lambda i: (i, 0))],
            out_shape=[jax.ShapeDtypeStruct((m, d), F32),
                       jax.ShapeDtypeStruct((m // tm * 8, d_ff), F32)],
            scratch_shapes=[pltpu.VMEM((tm + 8, d_ff), F32)],
            compiler_params=_cparams("arbitrary"), name="ffn_prompt",
        )(x, *wargs)
    p1, p2 = prev
    return pl.pallas_call(
        functools.partial(_ffn_kernel, 0, seq_len, d_ff), grid=(m // tm,),
        in_specs=[row(d)] + wspecs + [row(d_ff), row(d_ff)],
        out_specs=[row(d), row(d_ff)],
        out_shape=[jax.ShapeDtypeStruct((m, d), F32), jax.ShapeDtypeStruct((m, d_ff), F32)],
        compiler_params=_cparams("parallel"), name="ffn_sample",
    )(x, *wargs, p1, p2)


QW = N_GROUPS * DIL_HEADS * LANES
KVW = N_GROUPS * DIL_SLAB


def _proj_dil_kernel(prompt, x_ref, ga_ref, win_ref, sq_ref, sqt_ref, invq_ref, gq_ref,
                     sk_ref, skt_ref, invk_ref, gk_ref, vm_ref,
                     sm_ref, smt_ref, invm_ref, gm_ref, *rest):
    x = x_ref[0] if prompt else x_ref[...]
    h = _rms_rows(x, ga_ref[...]).astype(BF16)
    z = _dot(h, win_ref[...])
    q = z[:, :QW]
    qn = q * _seg_scale(q, sq_ref, sqt_ref, invq_ref) * gq_ref[...]
    kv = z[:, QW:QW + KVW]
    kvn = kv * (_seg_scale(kv, sk_ref, skt_ref, invk_ref) * gk_ref[...] + vm_ref[...])
    mq = z[:, QW + KVW:]
    mqn = (mq * _seg_scale(mq, sm_ref, smt_ref, invm_ref) * gm_ref[...]).astype(BF16)
    if not prompt:
        q_out, kv_out, mq_out = rest
        q_out[...] = qn.astype(BF16)
        kv_out[...] = kvn
        mq_out[...] = mqn
        return
    kv_out, mq_out = rest[0], rest[1]
    q_outs, kvb_outs, q_scr, kv_scr = rest[2:5], rest[5:8], rest[8], rest[9]
    kv_out[0] = kvn
    mq_out[0] = mqn
    for c in range(QW // LANES):
        q_scr[c] = qn[:, c * LANES:(c + 1) * LANES]
    for c in range(KVW // LANES):
        kv_scr[c] = kvn[:, c * LANES:(c + 1) * LANES]
    tm = q.shape[0]
    qc, kc = DIL_HEADS, DIL_SLAB // LANES
    for g, d in enumerate(DIL_RATES):
        for r in range(d):
            rows = pl.ds(r, tm // d, stride=d) if d > 1 else slice(None)
            for c in range(qc):
                q_outs[g][0, r, :, c * LANES:(c + 1) * LANES] = q_scr[g * qc + c, rows, :].astype(BF16)
            for c in range(kc):
                kvb_outs[g][0, r, :, c * LANES:(c + 1) * LANES] = kv_scr[g * kc + c, rows, :].astype(BF16)


def _proj_dil(x3, prompt, tm, w):
    n, t, d = x3.shape
    args = [x3, w['g_attn'], w['w_in'], w['Sq'], w['SqT'], w['invq'], w['gq'],
            w['Sk'], w['SkT'], w['invk'], w['gk'], w['vmask'], w['Sm'], w['SmT'], w['invm'], w['gm']]
    if not prompt:
        x2 = x3.reshape(n * t, d)
        m = n * t
        row = lambda width: pl.BlockSpec((tm, width), lambda i: (i, 0))
        return pl.pallas_call(
            functools.partial(_proj_dil_kernel, False), grid=(m // tm,),
            in_specs=[row(d)] + [_full(a.shape) for a in args[1:]],
            out_specs=[row(QW), row(KVW), row(MEM_W)],
            out_shape=[jax.ShapeDtypeStruct((m, QW), BF16), jax.ShapeDtypeStruct((m, KVW), F32),
                       jax.ShapeDtypeStruct((m, MEM_W), BF16)],
            compiler_params=_cparams("parallel"), name="proj_dil_sample",
        )(x2, *args[1:])
    row3 = lambda width: pl.BlockSpec((1, tm, width), lambda b, i: (b, i, 0))
    outs = [jax.ShapeDtypeStruct((n, t, KVW), F32), jax.ShapeDtypeStruct((n, t, MEM_W), BF16)]
    ospecs = [row3(KVW), row3(MEM_W)]
    for width in (DIL_HEADS * LANES, DIL_SLAB):
        for dil in DIL_RATES:
            outs.append(jax.ShapeDtypeStruct((n, dil, t // dil, width), BF16))
            ospecs.append(pl.BlockSpec((1, dil, tm // dil, width), lambda b, i: (b, 0, i, 0)))
    return pl.pallas_call(
        functools.partial(_proj_dil_kernel, True), grid=(n, t // tm),
        in_specs=[row3(d)] + [_full(a.shape) for a in args[1:]],
        out_specs=ospecs, out_shape=outs,
        scratch_shapes=[pltpu.VMEM((QW // LANES, tm, LANES), F32), pltpu.VMEM((KVW // LANES, tm, LANES), F32)],
        compiler_params=_cparams("parallel", "parallel"), name="proj_dil_prompt",
    )(*args)


def _band_kernel(tq, q_ref, kp_ref, kc_ref, bias_ref, o_ref, ml_ref):
    i = pl.program_id(1)
    kcat = jnp.concatenate([kp_ref[0], kc_ref[0]], axis=0)
    first = jnp.logical_and(i == 0, lax.broadcasted_iota(jnp.int32, (tq, 2 * tq), 1) < tq)
    ms, ls = [], []
    for hd in range(DIL_HEADS):
        s = _dot_nt(q_ref[0, :, hd * LANES:(hd + 1) * LANES], kcat[:, hd * DIL_ROW:hd * DIL_ROW + LANES])
        s = jnp.where(first, NEG, s + bias_ref[hd])
        m = jnp.max(s, axis=-1, keepdims=True)
        p = jnp.exp(s - m)
        l = jnp.sum(p, axis=-1, keepdims=True)
        o_ref[0, :, hd * DIL_ROW:(hd + 1) * DIL_ROW] = _dot(
            (p / l).astype(BF16), kcat[:, hd * DIL_ROW:(hd + 1) * DIL_ROW])
        ms.append(m)
        ls.append(l)
    ml_ref[0] = _cols(ms + ls, LANES)


def _band_attn(q, kv, bias, tq):
    s, l, _ = q.shape
    return pl.pallas_call(
        functools.partial(_band_kernel, tq), grid=(s, l // tq),
        in_specs=[pl.BlockSpec((1, tq, DIL_HEADS * LANES), lambda b, i: (b, i, 0)),
                  pl.BlockSpec((1, tq, DIL_SLAB), lambda b, i: (b, jnp.maximum(i - 1, 0), 0)),
                  pl.BlockSpec((1, tq, DIL_SLAB), lambda b, i: (b, i, 0)),
                  _full(bias.shape)],
        out_specs=[pl.BlockSpec((1, tq, DIL_SLAB), lambda b, i: (b, i, 0)),
                   pl.BlockSpec((1, tq, LANES), lambda b, i: (b, i, 0))],
        out_shape=[jax.ShapeDtypeStruct((s, l, DIL_SLAB), F32),
                   jax.ShapeDtypeStruct((s, l, LANES), F32)],
        compiler_params=_cparams("parallel", "parallel"), name="dil_band_attn",
    )(q, kv, kv, bias)


def _merge_kernel(rates, *refs):
    ng = len(rates)
    o_refs, ml_refs, out_ref = refs[:ng], refs[ng:2 * ng], refs[2 * ng]
    scr = refs[2 * ng + 1:]
    nch = DIL_SLAB // LANES
    chunks, mls = [], []
    for g, d in enumerate(rates):
        if d == 1:
            chunks.append([o_refs[g][0, 0, :, c * LANES:(c + 1) * LANES] for c in range(nch)])
            mls.append(ml_refs[g][0, 0])
        else:
            o_scr, ml_scr = scr[2 * g], scr[2 * g + 1]
            tm = ml_scr.shape[0]
            for r in range(d):
                rows = pl.ds(r, tm // d, stride=d)
                for c in range(nch):
                    o_scr[c, rows, :] = o_refs[g][0, r, :, c * LANES:(c + 1) * LANES]
                ml_scr[rows, :] = ml_refs[g][0, r]
            chunks.append([o_scr[c] for c in range(nch)])
            mls.append(ml_scr[...])
    m_all = functools.reduce(jnp.maximum, [ml[:, :DIL_HEADS] for ml in mls])
    es = [ml[:, DIL_HEADS:2 * DIL_HEADS] * jnp.exp(ml[:, :DIL_HEADS] - m_all) for ml in mls]
    tot = functools.reduce(jnp.add, es)
    ws = [e / tot for e in es]
    for c in range(nch):
        hd = c // (DIL_ROW // LANES)
        acc = None
        for g in range(ng):
            term = chunks[g][c] * ws[g][:, hd:hd + 1]
            acc = term if acc is None else acc + term
        out_ref[0, :, c * LANES:(c + 1) * LANES] = acc.astype(BF16)


def _merge(os_, mls, rates, n, t, tm):
    specs = [pl.BlockSpec((1, d, tm // d, DIL_SLAB), lambda b, i: (b, 0, i, 0)) for d in rates] \
        + [pl.BlockSpec((1, d, tm // d, LANES), lambda b, i: (b, 0, i, 0)) for d in rates]
    scratch = []
    for d in rates:
        scratch += [pltpu.VMEM((DIL_SLAB // LANES, tm, LANES), F32), pltpu.VMEM((tm, LANES), F32)]
    return pl.pallas_call(
        functools.partial(_merge_kernel, tuple(rates)), grid=(n, t // tm),
        in_specs=specs,
        out_specs=pl.BlockSpec((1, tm, DIL_SLAB), lambda b, i: (b, i, 0)),
        out_shape=jax.ShapeDtypeStruct((n, t, DIL_SLAB), BF16),
        scratch_shapes=scratch,
        compiler_params=_cparams("parallel", "parallel"), name="dil_merge",
    )(*os_, *mls)


def _dil_sample_kernel(d, wb, t_s, q_ref, new_ref, buf_ref, bias_ref, sbias_ref, o_ref, ml_ref, st_ref, *scr):
    nk = DIL_KEYS - 1
    new = new_ref[0]
    st_ref[0, 0:wb - t_s] = buf_ref[0, t_s:wb]
    st_ref[0, wb - t_s:wb] = new
    if d < t_s:
        src = scr[0]
        src[0:wb] = buf_ref[0]
        src[wb:wb + t_s] = new
    else:
        src = buf_ref.at[0]
    top = lax.broadcasted_iota(jnp.int32, (8, LANES), 0) < DIL_HEADS
    lane = lax.broadcasted_iota(jnp.int32, (8, LANES), 1)
    bias = bias_ref[...]
    sbias = sbias_ref[...][:, 0:1]

    def both_halves(col):
        x = jnp.where(top, jnp.broadcast_to(col, (8, LANES)), 0.0)
        return x + pltpu.roll(x, DIL_HEADS, 0)

    for i in range(t_s):
        a3 = src[pl.ds(i, nk, stride=d)] if d > 1 else src[pl.ds(i, nk)]
        a = a3.reshape(nk * 8, LANES).astype(BF16)
        qi = q_ref[0, i].astype(BF16)
        s = _dot_nt(qi, a) + bias
        nf = new[i].astype(BF16).astype(F32)
        s_self = jnp.sum(qi.astype(F32) * nf, axis=-1, keepdims=True) + sbias
        m = jnp.maximum(jnp.max(s, axis=-1, keepdims=True), s_self)
        p = jnp.exp(s - m)
        p_self = jnp.exp(s_self - m)
        l = jnp.sum(p, axis=-1, keepdims=True) + p_self
        pn = p / l
        p2 = jnp.concatenate([pn, pltpu.roll(pn, DIL_HEADS, 1)], axis=0).astype(BF16)
        o2 = _dot(p2, a)
        ps = both_halves((p_self / l).astype(BF16).astype(F32))
        o_ref[0, i] = jnp.where(top, o2[0:8], pltpu.roll(o2[8:16], DIL_HEADS, 0)) + ps * nf
        ml_ref[0, i] = jnp.where(lane == 0, both_halves(m), jnp.where(lane == 1, both_halves(l), 0.0))


def _dil_sample(qg, newg, buf, bias, sbias, d):
    b, t_s = qg.shape[:2]
    wb = buf.shape[1]
    blk = lambda r: pl.BlockSpec((1, r, 8, LANES), lambda s: (s, 0, 0, 0))
    tiles = jax.ShapeDtypeStruct((b, t_s, 8, LANES), F32)
    return pl.pallas_call(
        functools.partial(_dil_sample_kernel, d, wb, t_s), grid=(b,),
        in_specs=[blk(t_s), blk(t_s), blk(wb), _full(bias.shape), _full(sbias.shape)],
        out_specs=[blk(t_s), blk(t_s), blk(wb)],
        out_shape=[tiles, tiles, jax.ShapeDtypeStruct(buf.shape, F32)],
        scratch_shapes=[pltpu.VMEM((wb + t_s, 8, LANES), F32)] if d < t_s else [],
        compiler_params=_cparams("parallel"), name="dil_sample_attn",
    )(qg, newg, buf, bias, sbias)


def _merge_tiles_kernel(ng, *refs):
    o_refs, ml_refs, out_ref = refs[:ng], refs[ng:2 * ng], refs[2 * ng]
    ms = [r[...][:, :, 0:1] for r in ml_refs]
    ls = [r[...][:, :, 1:2] for r in ml_refs]
    m_all = functools.reduce(jnp.maximum, ms)
    es = [l * jnp.exp(m - m_all) for m, l in zip(ms, ls)]
    tot = functools.reduce(jnp.add, es)
    out_ref[...] = functools.reduce(jnp.add, [o[...] * (e / tot) for o, e in zip(o_refs, es)])


def _merge_tiles(os_, mls, tr):
    r = os_[0].shape[0]
    spec = pl.BlockSpec((tr, 8, LANES), lambda i: (i, 0, 0))
    return pl.pallas_call(
        functools.partial(_merge_tiles_kernel, len(os_)), grid=(r // tr,),
        in_specs=[spec] * (2 * len(os_)), out_specs=spec,
        out_shape=jax.ShapeDtypeStruct((r, 8, LANES), F32),
        compiler_params=_cparams("parallel"), name="dil_merge_sample",
    )(*os_, *mls)


def _seg_mats(width, segments):
    s = np.zeros((width, LANES), np.float32)
    inv = np.ones((1, LANES), np.float32)
    for k, (a, n) in enumerate(segments):
        s[a:a + n, k] = 1.0
        inv[0, k] = 1.0 / n
    return jnp.asarray(s, BF16), jnp.asarray(s.T.copy(), BF16), jnp.asarray(inv)


def _pad_heads(wm, heads, dim, slab, offset=0):
    r = wm.shape[0]
    out = jnp.pad(wm.reshape(r, heads, dim), ((0, 0), (0, 0), (offset, slab - dim - offset)))
    return out.reshape(r, heads * slab)


def _mem_q_tables(g_qn_mem):
    seg = [(hd * LANES, MEM_DIM) for hd in range(MEM_HEADS)]
    sm, smt, invm = _seg_mats(MEM_W, seg)
    gm = _pad_heads(jnp.tile(g_qn_mem, MEM_HEADS)[None, :], MEM_HEADS, MEM_DIM, LANES) * MEM_SCALE
    return dict(Sm=sm, SmT=smt, invm=invm, gm=gm)


def _rope_cs(pos):
    inv = ROPE_THETA ** (-jnp.arange(0, ROPE_DIM, 2, dtype=F32) / ROPE_DIM)
    ang = pos.astype(F32)[:, None] * inv[None, :]
    return jnp.cos(ang), jnp.sin(ang)


def _mla_tables(pos, g_qn_nope, g_qn_pe, g_kn_pe):
    cos, sin = _rope_cs(pos)
    t = pos.shape[0]
    hr = ROPE_DIM // 2
    ones = jnp.ones((t, 1), F32)
    cslab = jnp.concatenate([ones * g_qn_nope[None, :], cos * g_qn_pe[None, :hr], cos * g_qn_pe[None, hr:],
                             jnp.zeros((t, LANES - NOPE_DIM - ROPE_DIM), F32)], axis=1) * MLA_SCALE
    sslab = jnp.concatenate([jnp.zeros((t, NOPE_DIM), F32), -sin * g_qn_pe[None, hr:], sin * g_qn_pe[None, :hr],
                             jnp.zeros((t, LANES - NOPE_DIM - ROPE_DIM), F32)], axis=1) * MLA_SCALE
    tk = jnp.concatenate([cos * g_kn_pe[None, :hr], cos * g_kn_pe[None, hr:],
                          -sin * g_kn_pe[None, hr:], sin * g_kn_pe[None, :hr],
                          jnp.zeros((t, LANES - 2 * ROPE_DIM), F32)], axis=1)
    return cslab, sslab, tk


def _prep_mla(i, j, g_attn, w_o, w_in_a, g_q_a, w_q_b, g_kv_a, w_uk, w_uv, g_qn_mem):
    d = w_in_a.shape[1]
    hr = ROPE_DIM // 2
    wi = w_in_a[j]
    o1, o2, o3 = Q_LORA, Q_LORA + KV_LORA, Q_LORA + KV_LORA + ROPE_DIM
    kpe = wi[:, o2:o3]
    w_in = jnp.concatenate([wi[:, :o2], _pad_heads(wi[:, o3:], MEM_HEADS, MEM_DIM, LANES),
                            kpe, jnp.concatenate([kpe[:, hr:], kpe[:, :hr]], axis=1),
                            jnp.zeros((d, LANES - 2 * ROPE_DIM), F32)], axis=1).astype(BF16)
    hq = NOPE_DIM + ROPE_DIM
    wq3 = w_q_b[j].reshape(Q_LORA, MLA_HEADS, hq)
    wq = _pad_heads(w_q_b[j], MLA_HEADS, hq, LANES).astype(BF16)
    sw = jnp.concatenate([wq3[:, :, NOPE_DIM + hr:], wq3[:, :, NOPE_DIM:NOPE_DIM + hr]], axis=2)
    wq_sw = _pad_heads(sw.reshape(Q_LORA, MLA_HEADS * ROPE_DIM), MLA_HEADS, ROPE_DIM, LANES, NOPE_DIM).astype(BF16)
    seg = []
    for hd in range(MLA_HEADS):
        seg += [(hd * LANES, NOPE_DIM), (hd * LANES + NOPE_DIM, ROPE_DIM)]
    s, st, inv = _seg_mats(MLA_HEADS * LANES, seg)
    wk_c = _pad_heads(w_uk[j].reshape(KV_LORA, MLA_HEADS * NOPE_DIM), MLA_HEADS, NOPE_DIM, LANES).astype(BF16)
    eye = np.zeros((LANES, MLA_HEADS, LANES), np.float32)
    for r in range(ROPE_DIM):
        eye[r, :, NOPE_DIM + r] = 1.0
    wk_p = jnp.asarray(eye.reshape(LANES, MLA_HEADS * LANES), BF16)
    wv4 = w_uv[j].reshape(KV_LORA, MLA_HEADS // 2, 2, MLA_V_DIM)
    lo = ((0, 0), (0, 0), (0, LANES - MLA_V_DIM))
    hi = ((0, 0), (0, 0), (LANES - MLA_V_DIM, 0))
    wv = jnp.stack([jnp.pad(wv4[:, :, 0], lo), jnp.pad(wv4[:, :, 1], hi)], axis=2)
    wv = wv.reshape(KV_LORA, MLA_HEADS * LANES).astype(BF16)
    pe_rows = np.zeros((ROPE_DIM, MLA_QCAT), np.float32)
    pe_rows[np.arange(ROPE_DIM), KV_LORA + np.arange(ROPE_DIM)] = 1.0
    wcat = jnp.concatenate([
        jnp.pad(jnp.transpose(w_uk[j], (1, 2, 0)), ((0, 0), (0, 0), (0, MLA_QCAT - KV_LORA))),
        jnp.broadcast_to(jnp.asarray(pe_rows)[None], (MLA_HEADS, ROPE_DIM, MLA_QCAT)),
        jnp.zeros((MLA_HEADS, LANES - hq, MLA_QCAT), F32)], axis=1)
    wvp = jnp.transpose(wv4, (1, 2, 0, 3))
    uvp = jnp.concatenate([jnp.pad(wvp[:, 0], lo), jnp.pad(wvp[:, 1], hi)], axis=1)
    ntok = MLA_HEADS * MLA_V_DIM
    w = dict(g_attn=g_attn[i][None], w_in=w_in, g_q_a=g_q_a[j][None], wq=wq, wq_sw=wq_sw,
             g_kv_a=g_kv_a[j][None], S=s, ST=st, inv=inv, wk_c=wk_c, wk_p=wk_p, wv=wv,
             wcat=wcat.astype(BF16), uv_pairs=uvp.astype(BF16),
             wo_tok=w_o[i][:ntok].astype(BF16),
             wo_mem=_pad_rows(w_o[i][ntok:], MEM_HEADS, MEM_DIM, LANES, MEM_DIM).astype(BF16))
    w.update(_mem_q_tables(g_qn_mem[i]))
    return w


def _pad_rows(wm, heads, dim, slab, offset):
    c = wm.shape[1]
    out = jnp.pad(wm.reshape(heads, dim, c), ((0, 0), (offset, slab - dim - offset), (0, 0)))
    return out.reshape(heads * slab, c)


def _prep_dil(i, j, g_attn, w_o, w_in_b, g_qn_b, g_kn_b, g_qn_mem):
    wi = w_in_b[j]
    gh = N_GROUPS * DIL_HEADS
    nqk = gh * DIL_QK
    wq = _pad_heads(wi[:, :nqk], gh, DIL_QK, LANES)
    wk = wi[:, nqk:2 * nqk].reshape(-1, gh, DIL_QK)
    wv = wi[:, 2 * nqk:2 * nqk + gh * DIL_V].reshape(-1, gh, DIL_V)
    wkv = jnp.concatenate([wk, wv], axis=2).reshape(-1, gh * DIL_ROW)
    wmq = _pad_heads(wi[:, 2 * nqk + gh * DIL_V:], MEM_HEADS, MEM_DIM, LANES)
    w_in = jnp.concatenate([wq, wkv, wmq], axis=1).astype(BF16)
    sq, sqt, invq = _seg_mats(QW, [(k * LANES, DIL_QK) for k in range(gh)])
    sk, skt, invk = _seg_mats(KVW, [(k * DIL_ROW, DIL_QK) for k in range(gh)])
    gq = _pad_heads(jnp.repeat(g_qn_b[j], DIL_HEADS, axis=0).reshape(1, nqk), gh, DIL_QK, LANES) * DIL_SCALE
    gk = _pad_heads(jnp.repeat(g_kn_b[j], DIL_HEADS, axis=0).reshape(1, nqk), gh, DIL_QK, DIL_ROW)
    vmask = _pad_heads(jnp.ones((1, gh * DIL_V), F32), gh, DIL_V, DIL_ROW, DIL_QK)
    ntok = DIL_HEADS * DIL_V
    w = dict(g_attn=g_attn[i][None], w_in=w_in, Sq=sq, SqT=sqt, invq=invq, gq=gq,
             Sk=sk, SkT=skt, invk=invk, gk=gk, vmask=vmask,
             wo_tok=_pad_rows(w_o[i][:ntok], DIL_HEADS, DIL_V, DIL_ROW, DIL_QK).astype(BF16),
             wo_mem=_pad_rows(w_o[i][ntok:], MEM_HEADS, MEM_DIM, LANES, MEM_DIM).astype(BF16))
    w.update(_mem_q_tables(g_qn_mem[i]))
    return w


def _prep_mem(i, g_mem, w_mem_kv, g_kn_mem):
    sk, skt, invk = _seg_mats(MEM_W, [(hd * LANES, MEM_DIM) for hd in range(MEM_HEADS)])
    gk = _pad_heads(jnp.tile(g_kn_mem[i], MEM_HEADS)[None, :], MEM_HEADS, MEM_DIM, LANES)
    vmask = _pad_heads(jnp.ones((1, MEM_HEADS * MEM_DIM), F32), MEM_HEADS, MEM_DIM, LANES, MEM_DIM)
    return dict(g_mem=g_mem[i][None], w_mem=w_mem_kv[i].astype(BF16), Sk=sk, SkT=skt, invk=invk, gk=gk, vmask=vmask)


def _prep_ffn(i, g_ffn, w_up, conv_w, conv_b, w_down):
    return dict(g_ffn=g_ffn[i][None], w_up=w_up[i].astype(BF16), conv_w=conv_w[i],
                conv_b=conv_b[i][None], w_down=w_down[i].astype(BF16))


def _rel_bucket(dist):
    max_exact = N_BUCKETS // 2
    dd = jnp.maximum(dist.astype(F32), 1.0)
    large = max_exact + (jnp.log(dd / max_exact) / math.log(MAX_DISTANCE / max_exact)
                         * (N_BUCKETS - max_exact)).astype(jnp.int32)
    large = jnp.minimum(large, N_BUCKETS - 1)
    return jnp.where(dist < max_exact, dist, large)


def _dil_bias_tables(rel_bias, tq):
    band, samp, self_b = [], [], []
    nk = DIL_KEYS - 1
    period = 3 * tq
    srow = jnp.arange(8)[None, None, :] == jnp.arange(DIL_HEADS)[:, None, None]
    for g in range(N_GROUPS):
        bk = _rel_bucket(DIL_RATES[g] * jnp.arange(DIL_KEYS, dtype=jnp.int32))
        b = rel_bias[bk][:, g * DIL_HEADS:(g + 1) * DIL_HEADS].T.astype(F32)
        rev = b[:, ::-1]
        f = jnp.concatenate([rev, jnp.full((DIL_HEADS, period - DIL_KEYS), NEG, F32)], axis=1)
        toe = jnp.tile(f, (1, tq))[:, :tq * (period - 1)].reshape(DIL_HEADS, tq, period - 1)
        band.append(toe[:, :, :2 * tq])
        tab = jnp.where(srow, rev[:, :nk, None], NEG).reshape(DIL_HEADS, 8 * nk)
        samp.append(jnp.pad(tab, ((0, 8 - DIL_HEADS), (0, 0)), constant_values=NEG))
        self_b.append(jnp.pad(b[:, :1], ((0, 8 - DIL_HEADS), (0, LANES - 1))))
    return band, samp, self_b


def kernel(x_prompt, x_sample, cache_mla, state_win0, state_win1, state_win2, cache_mem, state_conv, page_table, mem_prompt, rel_bias, g_attn, w_o, w_in_a, g_q_a, w_q_b, g_kv_a, w_uk, w_uv, g_qn_nope, g_qn_pe, g_kn_pe, w_in_b, g_qn_b, g_kn_b, g_mem, w_mem_kv, g_qn_mem, g_kn_mem, g_ffn, w_up, conv_w, conv_b, w_down):
    state_wins = (state_win0, state_win1, state_win2)
    n, t_p, d = x_prompt.shape
    b, t_s, _ = x_sample.shape
    n_mem = mem_prompt.shape[1]
    d_ff = w_down.shape[1]
    past = page_table.shape[1] * cache_mla.shape[2]
    mp, ms = n * t_p, b * t_s
    tm_p = min(256, t_p)
    tm_s = min(256, ms)
    tq_band = DIL_KEYS - 1
    assert t_s >= 2 and t_p % tm_p == 0 and ms % tm_s == 0 and tm_s % t_s == 0
    assert all(sw.shape[2] == wdw and wdw == (DIL_KEYS - 1) * r
               for sw, wdw, r in zip(state_wins, DIL_WINDOWS, DIL_RATES))

    xp = x_prompt.reshape(mp, d)
    xs = x_sample.reshape(ms, d)
    mem2 = mem_prompt.reshape(n * n_mem, d)
    pos_p = jnp.tile(jnp.arange(t_p), n)
    pos_s = jnp.tile(past + jnp.arange(t_s), b)
    band_bias, samp_bias, self_bias = _dil_bias_tables(rel_bias, tq_band)
    cache_rows = cache_mem.reshape(cache_mem.shape[0], b, n_mem * MEM_HEADS, 2 * MEM_DIM)

    mem_out, conv_p_out, conv_s_out = [], [], []
    for i in range(2):
        j = i // 2
        if i % 2 == 0:
            w = _prep_mla(i, j, g_attn, w_o, w_in_a, g_q_a, w_q_b, g_kv_a, w_uk, w_uv, g_qn_mem)
            cq, sq, tk = _mla_tables(pos_p, g_qn_nope[j], g_qn_pe[j], g_kn_pe[j])
            q_p, rows_p, mq_p, k_p, v_p = _proj_mla(xp, True, tm_p, w, cq, sq, tk)
            cq, sq, tk = _mla_tables(pos_s, g_qn_nope[j], g_qn_pe[j], g_kn_pe[j])
            qc_s, rows_s, mq_s = _proj_mla(xs, False, tm_s, w, cq, sq, tk)
            hw = MLA_HEADS * LANES
            tok_p = _mla_flash(q_p.reshape(n, t_p, hw), k_p.reshape(n, t_p, hw), v_p.reshape(n, t_p, hw),
                               min(512, t_p)).reshape(mp, -1)
            q3 = qc_s.reshape(b, t_s, MLA_HEADS, MLA_QCAT).transpose(0, 2, 1, 3).reshape(b, MLA_HEADS * t_s, MLA_QCAT)
            ctx = _mla_sample_attn(q3, rows_s.reshape(b, t_s, MLA_ROW), cache_mla[j].transpose(0, 2, 1),
                                   page_table, min(16, page_table.shape[1]))
            ctx = ctx.reshape(b, MLA_HEADS, t_s, KV_LORA).transpose(0, 2, 1, 3).reshape(ms, MLA_HEADS * KV_LORA)
            tok_s = _mla_uv(ctx, w['uv_pairs'], tm_s)
            mla_rows_p = rows_p.reshape(1, n, t_p, MLA_ROW)
            mla_rows_s = rows_s.reshape(1, b, t_s, MLA_ROW)
        else:
            w = _prep_dil(i, j, g_attn, w_o, w_in_b, g_qn_b, g_kn_b, g_qn_mem)
            res = _proj_dil(xp.reshape(n, t_p, d), True, tm_p, w)
            kv_p, mq_p = res[0], res[1].reshape(mp, MEM_W)
            os_, mls = [], []
            for g, dil in enumerate(DIL_RATES):
                l = t_p // dil
                o, ml = _band_attn(res[2 + g].reshape(n * dil, l, DIL_HEADS * LANES),
                                   res[5 + g].reshape(n * dil, l, DIL_SLAB), band_bias[g], tq_band)
                os_.append(o.reshape(n, dil, l, DIL_SLAB))
                mls.append(ml.reshape(n, dil, l, LANES))
            tok_p = _merge(os_, mls, DIL_RATES, n, t_p, tm_p).reshape(mp, DIL_SLAB)
            win_p = [kv_p[:, t_p - min(wdw, t_p):, g * DIL_SLAB:(g + 1) * DIL_SLAB]
                     .reshape(1, n, min(wdw, t_p), DIL_HEADS, DIL_ROW) for g, wdw in enumerate(DIL_WINDOWS)]
            q_s, kv_s, mq_s = _proj_dil(xs.reshape(b, t_s, d), False, tm_s, w)
            nc = DIL_ROW // LANES
            q_t = jnp.pad(q_s.astype(F32).reshape(b, t_s, N_GROUPS, DIL_HEADS, LANES),
                          ((0, 0), (0, 0), (0, 0), (0, 8 - DIL_HEADS), (0, 0))).transpose(2, 0, 1, 3, 4)
            kv_t = kv_s.reshape(b, t_s, N_GROUPS, DIL_HEADS, nc, LANES).transpose(2, 0, 1, 4, 3, 5)
            kv_t = kv_t.reshape(N_GROUPS, b, t_s, 8, LANES)
            os_, mls, win_s = [], [], []
            for g, dil in enumerate(DIL_RATES):
                buf = state_wins[g][j]
                wb = buf.shape[1]
                buf_t = buf.reshape(b, wb, DIL_HEADS, nc, LANES).transpose(0, 1, 3, 2, 4).reshape(b, wb, 8, LANES)
                o, ml, st = _dil_sample(q_t[g], kv_t[g], buf_t, samp_bias[g], self_bias[g], dil)
                os_.append(o.reshape(ms, 8, LANES))
                mls.append(ml.reshape(ms, 8, LANES))
                st = st.reshape(b, wb, nc, DIL_HEADS, LANES).transpose(0, 1, 3, 2, 4)
                win_s.append(st.reshape(1, b, wb, DIL_HEADS, DIL_ROW))
            tok_s = _merge_tiles(os_, mls, tm_s).reshape(ms, nc, DIL_HEADS, LANES).transpose(0, 2, 1, 3)
            tok_s = tok_s.reshape(ms, DIL_SLAB).astype(BF16)

        wm = _prep_mem(i, g_mem, w_mem_kv, g_kn_mem)
        mkv = _mem_kv(mem2, wm, min(256, n * n_mem))
        mem_out.append(mkv.reshape(n, n_mem, MEM_HEADS, 2 * MEM_DIM))
        mo_p = _mem_attn(mq_p.reshape(n, t_p, MEM_W), mkv.reshape(n, n_mem, MEM_W), min(512, t_p)).reshape(mp, MEM_W)
        mo_s = _mem_attn_rows(mq_s.reshape(b, t_s * MEM_HEADS, LANES), cache_rows, i, min(8, b)).reshape(ms, MEM_W)
        xp = _out_proj(xp, tok_p, mo_p, w['wo_tok'], w['wo_mem'], tm_p)
        xs = _out_proj(xs, tok_s, mo_s, w['wo_tok'], w['wo_mem'], tm_s)

        wf = _prep_ffn(i, g_ffn, w_up, conv_w, conv_b, w_down)
        xp, tails = _ffn(xp, t_p, wf, tm_p)
        conv_p_out.append(tails.reshape(n, t_p // tm_p, 8, d_ff)[:, -1, 6:, :])
        prev = state_conv[i]
        zeros = jnp.zeros((b, t_s - 2, d_ff), F32)
        p2 = jnp.concatenate([prev, zeros], axis=1).reshape(ms, d_ff)
        p1 = jnp.concatenate([prev[:, 1:], zeros, jnp.zeros((b, 1, d_ff), F32)], axis=1).reshape(ms, d_ff)
        xs, gates = _ffn(xs, t_s, wf, tm_s, prev=(p1, p2))
        conv_s_out.append(gates.reshape(b, t_s, d_ff)[:, t_s - 2:, :])

    return (xp.reshape(n, t_p, d), xs.reshape(b, t_s, d), mla_rows_p, mla_rows_s,
            win_p[0], win_p[1], win_p[2], win_s[0], win_s[1], win_s[2],
            jnp.stack(mem_out), jnp.stack(conv_p_out), jnp.stack(conv_s_out))
```

```python
import functools
import math

import jax
import jax.numpy as jnp
import numpy as np
from jax import lax
from jax.experimental import pallas as pl
from jax.experimental.pallas import tpu as pltpu

F32 = jnp.float32
BF16 = jnp.bfloat16

EPS = 1e-6
LANES = 128
VMEM_LIMIT = 48 * 1024 * 1024

MLA_HEADS = 12
NOPE_DIM = 64
ROPE_DIM = 32
MLA_V_DIM = 64
Q_LORA = 256
KV_LORA = 256
MLA_ROW = KV_LORA + ROPE_DIM
ROPE_THETA = 10000.0
MLA_SCALE = 1.0 / math.sqrt(NOPE_DIM + ROPE_DIM)
MLA_QCAT = 384
MLA_PAGES_PER_STEP = 32
MLA_PAGE_CHUNK = 32
FLASH_HEAD_GROUP = 4

DIL_WINDOWS = (128, 512, 2048)
DIL_RATES = (1, 4, 16)
N_GROUPS = 3
DIL_HEADS = 4
DIL_QK = 64
DIL_V = 192
DIL_KEYS = DIL_WINDOWS[0] // DIL_RATES[0] + 1
DIL_SCALE = 1.0 / math.sqrt(DIL_QK)
DIL_ROW = DIL_QK + DIL_V
DIL_SLAB = DIL_HEADS * DIL_ROW

MEM_HEADS = 4
MEM_DIM = 64
MEM_SCALE = 1.0 / math.sqrt(MEM_DIM)
MEM_W = MEM_HEADS * LANES

N_BUCKETS = 32
MAX_DISTANCE = 2048
NEG = -1e30


def _cparams(*sem):
    return pltpu.CompilerParams(dimension_semantics=sem, vmem_limit_bytes=VMEM_LIMIT)


def _dot(a, b):
    return jnp.dot(a, b, preferred_element_type=F32)


def _dot_nt(a, b):
    return lax.dot_general(a, b, (((1,), (1,)), ((), ())), preferred_element_type=F32)


def _split_dot(x, w):
    hi = x.astype(BF16)
    lo = (x - hi.astype(F32)).astype(BF16)
    return _dot(hi, w) + _dot(lo, w)


def _rms_rows(x, g):
    return x * lax.rsqrt(jnp.mean(x * x, axis=-1, keepdims=True) + EPS) * g


def _seg_scale(x, s_ref, st_ref, inv_ref):
    ss = _split_dot(x * x, s_ref[...])
    r = lax.rsqrt(ss * inv_ref[...] + EPS)
    return _split_dot(r, st_ref[...])


def _cols(vals, width):
    rows = vals[0].shape[0]
    lane = lax.broadcasted_iota(jnp.int32, (rows, width), 1)
    out = jnp.zeros((rows, width), F32)
    for k, v in enumerate(vals):
        out = jnp.where(lane == k, v, out)
    return out


def _full(shape):
    return pl.BlockSpec(shape, lambda *_: (0,) * len(shape))


def _proj_mla_kernel(prompt, x_ref, ga_ref, win_ref, gqa_ref, wq_ref, wqs_ref, gkv_ref,
                     cq_ref, sq_ref, tk_ref, s_ref, st_ref, inv_ref,
                     sm_ref, smt_ref, invm_ref, gm_ref, *rest):
    if prompt:
        wkc_ref, wkp_ref, wv_ref, vone_ref, q_out, rows_out, mq_out, k_out, v_out = rest
    else:
        wcat_ref, q_out, rows_out, mq_out = rest
    h = _rms_rows(x_ref[...], ga_ref[...]).astype(BF16)
    z = _dot(h, win_ref[...])
    qa = _rms_rows(z[:, :Q_LORA], gqa_ref[...]).astype(BF16)
    q = _dot(qa, wq_ref[...])
    qs = _dot(qa, wqs_ref[...])
    scale = _seg_scale(q, s_ref, st_ref, inv_ref)
    cq = jnp.concatenate([cq_ref[...]] * MLA_HEADS, axis=1)
    sq = jnp.concatenate([sq_ref[...]] * MLA_HEADS, axis=1)
    qf = ((q * cq + qs * sq) * scale).astype(BF16)
    c = _rms_rows(z[:, Q_LORA:Q_LORA + KV_LORA], gkv_ref[...])
    zk = z[:, 1024:1152]
    kp = zk[:, :ROPE_DIM]
    rk = lax.rsqrt(jnp.mean(kp * kp, axis=-1, keepdims=True) + EPS)
    t = zk * tk_ref[...] * rk
    kr = t + pltpu.roll(t, LANES - ROPE_DIM, 1)
    rows_out[:, :KV_LORA] = c
    rows_out[:, KV_LORA:] = kr[:, :ROPE_DIM]
    mq = z[:, 512:1024]
    mscale = _seg_scale(mq, sm_ref, smt_ref, invm_ref)
    mq_out[...] = (mq * mscale * gm_ref[...]).astype(BF16)
    if prompt:
        q_out[...] = qf
        cb = c.astype(BF16)
        k_out[...] = (_dot(cb, wkc_ref[...]) + _dot(kr.astype(BF16), wkp_ref[...])).astype(BF16)
        v_out[...] = (_dot(cb, wv_ref[...]) + vone_ref[...]).astype(BF16)
    else:
        for hd in range(MLA_HEADS):
            q_out[:, hd * MLA_QCAT:(hd + 1) * MLA_QCAT] = _dot(
                qf[:, hd * LANES:(hd + 1) * LANES], wcat_ref[hd]).astype(BF16)


def _proj_mla(x, prompt, tm, w, cq, sq, tk):
    m, d = x.shape
    hw = MLA_HEADS * LANES
    row = lambda width: pl.BlockSpec((tm, width), lambda i: (i, 0))
    args = [x, w['g_attn'], w['w_in'], w['g_q_a'], w['wq'], w['wq_sw'], w['g_kv_a'],
            cq, sq, tk, w['S'], w['ST'], w['inv'], w['Sm'], w['SmT'], w['invm'], w['gm']]
    specs = [row(d)] + [_full(a.shape) for a in args[1:7]] + [row(LANES), row(LANES), row(LANES)] \
        + [_full(a.shape) for a in args[10:]]
    if prompt:
        extra = [w['wk_c'], w['wk_p'], w['wv'], w['vone']]
        outs = [jax.ShapeDtypeStruct((m, hw), BF16), jax.ShapeDtypeStruct((m, MLA_ROW), F32),
                jax.ShapeDtypeStruct((m, MEM_W), BF16), jax.ShapeDtypeStruct((m, hw), BF16),
                jax.ShapeDtypeStruct((m, hw), BF16)]
        ospecs = [row(hw), row(MLA_ROW), row(MEM_W), row(hw), row(hw)]
    else:
        extra = [w['wcat']]
        outs = [jax.ShapeDtypeStruct((m, MLA_HEADS * MLA_QCAT), BF16),
                jax.ShapeDtypeStruct((m, MLA_ROW), F32), jax.ShapeDtypeStruct((m, MEM_W), BF16)]
        ospecs = [row(MLA_HEADS * MLA_QCAT), row(MLA_ROW), row(MEM_W)]
    return pl.pallas_call(
        functools.partial(_proj_mla_kernel, prompt),
        grid=(m // tm,),
        in_specs=specs + [_full(a.shape) for a in extra],
        out_specs=ospecs, out_shape=outs,
        compiler_params=_cparams("parallel"),
        name="proj_mla_prompt" if prompt else "proj_mla_sample",
    )(*args, *extra)


def _flash_kernel(tq, q_ref, k_ref, v_ref, o_ref, m_scr, acc_scr):
    i, j = pl.program_id(1), pl.program_id(2)

    @pl.when(j == 0)
    def _():
        m_scr[...] = jnp.full(m_scr.shape, NEG, F32)
        acc_scr[...] = jnp.zeros(acc_scr.shape, F32)

    def step(masked):
        if masked:
            keep = lax.broadcasted_iota(jnp.int32, (tq, tq), 1) <= lax.broadcasted_iota(jnp.int32, (tq, tq), 0)
        for h0 in range(0, MLA_HEADS, FLASH_HEAD_GROUP):
            heads = range(h0, h0 + FLASH_HEAD_GROUP)
            scores = [_dot_nt(q_ref[0, :, hd * LANES:(hd + 1) * LANES], k_ref[0, :, hd * LANES:(hd + 1) * LANES])
                      for hd in heads]
            for hd, s in zip(heads, scores):
                if masked:
                    s = jnp.where(keep, s, NEG)
                m_old = m_scr[hd]
                m_new = jnp.maximum(m_old, jnp.max(s, axis=-1, keepdims=True))
                alpha = jnp.exp(m_old - m_new)
                p = jnp.exp(s - jnp.concatenate([m_new] * (tq // LANES), axis=1))
                acc_scr[hd] = alpha * acc_scr[hd] + _dot(p.astype(BF16), v_ref[0, :, hd * LANES:(hd + 1) * LANES])
                m_scr[hd] = m_new

    @pl.when(j < i)
    def _():
        step(False)

    @pl.when(j == i)
    def _():
        step(True)
        low = lax.broadcasted_iota(jnp.int32, (tq, LANES), 1) < MLA_V_DIM
        for pr in range(MLA_HEADS // 2):
            ae, ao = acc_scr[2 * pr], acc_scr[2 * pr + 1]
            o_ref[0, :, pr * LANES:(pr + 1) * LANES] = jnp.where(
                low, ae / ae[:, MLA_V_DIM:MLA_V_DIM + 1], ao / ao[:, 0:1]).astype(BF16)


def _mla_flash(q, k, v, tq):
    n, t, hw = q.shape
    nq = t // tq
    return pl.pallas_call(
        functools.partial(_flash_kernel, tq),
        grid=(n, nq, nq),
        in_specs=[pl.BlockSpec((1, tq, hw), lambda b, i, j: (b, i, 0)),
                  pl.BlockSpec((1, tq, hw), lambda b, i, j: (b, jnp.minimum(i, j), 0)),
                  pl.BlockSpec((1, tq, hw), lambda b, i, j: (b, jnp.minimum(i, j), 0))],
        out_specs=pl.BlockSpec((1, tq, MLA_HEADS * MLA_V_DIM), lambda b, i, j: (b, i, 0)),
        out_shape=jax.ShapeDtypeStruct((n, t, MLA_HEADS * MLA_V_DIM), BF16),
        scratch_shapes=[pltpu.VMEM((MLA_HEADS, tq, LANES), F32), pltpu.VMEM((MLA_HEADS, tq, LANES), F32)],
        compiler_params=_cparams("parallel", "parallel", "arbitrary"),
        name="mla_flash_prompt",
    )(q, k, v)


def _mla_sample_kernel(npg, t_s, pt_ref, q_ref, new_ref, *rest):
    pages, (o_ref, m_scr, l_scr, acc_scr) = rest[:npg], rest[npg:]
    c = pl.program_id(1)

    @pl.when(c == 0)
    def _():
        m_scr[...] = jnp.full(m_scr.shape, NEG, F32)
        l_scr[...] = jnp.zeros(l_scr.shape, F32)
        acc_scr[...] = jnp.zeros(acc_scr.shape, F32)

    q = q_ref[0][:, :MLA_ROW]

    def update(state, s, pv):
        m_old, l_old, acc = state
        m_new = jnp.maximum(m_old, jnp.max(s, axis=-1, keepdims=True))
        alpha = jnp.exp(m_old - m_new)
        p = jnp.exp(s - m_new[:, 0:1])
        l_new = alpha * l_old + jnp.sum(p, axis=-1, keepdims=True)
        acc = jnp.concatenate([alpha] * (KV_LORA // LANES), axis=1) * acc + pv(p.astype(BF16))
        return m_new, l_new, acc

    def save(state):
        m_scr[...], l_scr[...], acc_scr[...] = state

    state = (m_scr[...], l_scr[...], acc_scr[...])
    for c0 in range(0, npg, MLA_PAGE_CHUNK):
        kt = jnp.concatenate([pg[0].astype(BF16) for pg in pages[c0:c0 + MLA_PAGE_CHUNK]], axis=1)
        state = update(state, _dot(q, kt), lambda p, kt=kt: _dot_nt(p, kt[:KV_LORA]))
    save(state)

    @pl.when(c == pl.num_programs(1) - 1)
    def _():
        nb = new_ref[0].astype(BF16)
        s = _dot_nt(q, nb)
        rows = q.shape[0]
        tq = lax.broadcasted_iota(jnp.int32, (rows, t_s), 0) % t_s
        s = jnp.where(lax.broadcasted_iota(jnp.int32, (rows, t_s), 1) <= tq, s, NEG)
        _, l_fin, acc = update((m_scr[...], l_scr[...], acc_scr[...]), s, lambda p: _dot(p, nb[:, :KV_LORA]))
        o_ref[0] = (acc / jnp.concatenate([l_fin] * (KV_LORA // LANES), axis=1)).astype(BF16)


def _mla_sample_attn(q3, rows3, pool_t, page_table, npg):
    b, rows, _ = q3.shape
    t_s = rows3.shape[1]
    n_pages = page_table.shape[1]
    psz = pool_t.shape[2]

    def page_spec(k):
        return pl.BlockSpec((1, MLA_ROW, psz), lambda s, c, pt: (pt[s * n_pages + c * npg + k], 0, 0))

    return pl.pallas_call(
        functools.partial(_mla_sample_kernel, npg, t_s),
        grid_spec=pltpu.PrefetchScalarGridSpec(
            num_scalar_prefetch=1,
            grid=(b, n_pages // npg),
            in_specs=[pl.BlockSpec((1, rows, MLA_QCAT), lambda s, c, pt: (s, 0, 0)),
                      pl.BlockSpec((1, t_s, MLA_ROW), lambda s, c, pt: (s, 0, 0))]
            + [page_spec(k) for k in range(npg)],
            out_specs=pl.BlockSpec((1, rows, KV_LORA), lambda s, c, pt: (s, 0, 0)),
            scratch_shapes=[pltpu.VMEM((rows, LANES), F32), pltpu.VMEM((rows, LANES), F32),
                            pltpu.VMEM((rows, KV_LORA), F32)]),
        out_shape=jax.ShapeDtypeStruct((b, rows, KV_LORA), BF16),
        compiler_params=_cparams("parallel", "arbitrary"),
        name="mla_sample_attn",
    )(page_table.reshape(-1), q3, rows3, *([pool_t] * npg))


def _uv_kernel(ctx_ref, w_ref, o_ref):
    for pr in range(MLA_HEADS // 2):
        o_ref[:, pr * LANES:(pr + 1) * LANES] = _dot(
            ctx_ref[:, pr * 2 * KV_LORA:(pr + 1) * 2 * KV_LORA], w_ref[pr]).astype(BF16)


def _mla_uv(ctx, w_pairs, tm):
    m = ctx.shape[0]
    return pl.pallas_call(
        _uv_kernel, grid=(m // tm,),
        in_specs=[pl.BlockSpec((tm, ctx.shape[1]), lambda i: (i, 0)), _full(w_pairs.shape)],
        out_specs=pl.BlockSpec((tm, MLA_HEADS * MLA_V_DIM), lambda i: (i, 0)),
        out_shape=jax.ShapeDtypeStruct((m, MLA_HEADS * MLA_V_DIM), BF16),
        compiler_params=_cparams("parallel"), name="mla_sample_uv",
    )(ctx, w_pairs)


def _mem_kv_kernel(x_ref, g_ref, w_ref, s_ref, st_ref, inv_ref, gk_ref, vm_ref, o_ref):
    h = _rms_rows(x_ref[...], g_ref[...]).astype(BF16)
    kv = _dot(h, w_ref[...])
    scale = _seg_scale(kv, s_ref, st_ref, inv_ref) * gk_ref[...] + vm_ref[...]
    o_ref[...] = kv * scale


def _mem_kv(mem2, w, tm):
    m, d = mem2.shape
    args = [mem2, w['g_mem'], w['w_mem'], w['Sk'], w['SkT'], w['invk'], w['gk'], w['vmask']]
    return pl.pallas_call(
        _mem_kv_kernel, grid=(m // tm,),
        in_specs=[pl.BlockSpec((tm, d), lambda i: (i, 0))] + [_full(a.shape) for a in args[1:]],
        out_specs=pl.BlockSpec((tm, MEM_W), lambda i: (i, 0)),
        out_shape=jax.ShapeDtypeStruct((m, MEM_W), F32),
        compiler_params=_cparams("parallel"), name="mem_kv",
    )(*args)


def _mem_attn_body(q, kv):
    outs = []
    scores = [_dot_nt(q[:, hd * LANES:(hd + 1) * LANES], kv[:, hd * LANES:(hd + 1) * LANES])
              for hd in range(MEM_HEADS)]
    for hd, s in enumerate(scores):
        sl = slice(hd * LANES, (hd + 1) * LANES)
        p = jnp.exp(s - jnp.max(s, axis=-1, keepdims=True))
        p = p / jnp.sum(p, axis=-1, keepdims=True)
        outs.append(_dot(p.astype(BF16), kv[:, sl]))
    return jnp.concatenate(outs, axis=-1)


def _mem_attn_kernel(q_ref, kv_ref, o_ref):
    o_ref[0] = _mem_attn_body(q_ref[0], kv_ref[0].astype(BF16)).astype(BF16)


def _mem_attn(q3, kv3, tq):
    n, t, _ = q3.shape
    return pl.pallas_call(
        _mem_attn_kernel, grid=(n, t // tq),
        in_specs=[pl.BlockSpec((1, tq, MEM_W), lambda b, i: (b, i, 0)),
                  pl.BlockSpec((1, kv3.shape[1], MEM_W), lambda b, i: (b, 0, 0))],
        out_specs=pl.BlockSpec((1, tq, MEM_W), lambda b, i: (b, i, 0)),
        out_shape=jax.ShapeDtypeStruct((n, t, MEM_W), BF16),
        compiler_params=_cparams("parallel", "parallel"), name="mem_attn",
    )(q3, kv3)


def _mem_rows_kernel(sb, q_ref, kv_ref, o_ref):
    nq, nkv = q_ref.shape[1], kv_ref.shape[2]
    same = (lax.broadcasted_iota(jnp.int32, (nq, nkv), 0) % MEM_HEADS
            == lax.broadcasted_iota(jnp.int32, (nq, nkv), 1) % MEM_HEADS)
    for s in range(sb):
        kv = kv_ref[0, s].astype(BF16)
        sc = jnp.where(same, _dot_nt(q_ref[s], kv), NEG)
        p = jnp.exp(sc - jnp.max(sc, axis=-1, keepdims=True))
        p = p / jnp.sum(p, axis=-1, keepdims=True)
        o_ref[s] = _dot(p.astype(BF16), kv).astype(BF16)


def _mem_attn_rows(q3, cache4, layer, sb):
    b, nq, _ = q3.shape
    return pl.pallas_call(
        functools.partial(_mem_rows_kernel, sb), grid=(b // sb,),
        in_specs=[pl.BlockSpec((sb, nq, LANES), lambda i: (i, 0, 0)),
                  pl.BlockSpec((1, sb, cache4.shape[2], LANES), lambda i: (layer, i, 0, 0))],
        out_specs=pl.BlockSpec((sb, nq, LANES), lambda i: (i, 0, 0)),
        out_shape=jax.ShapeDtypeStruct((b, nq, LANES), BF16),
        compiler_params=_cparams("parallel"), name="mem_attn_sample",
    )(q3, cache4)


def _out_proj_kernel(x_ref, a_ref, b_ref, wa_ref, wb_ref, o_ref):
    o_ref[...] = x_ref[...] + _dot(a_ref[...], wa_ref[...]) + _dot(b_ref[...], wb_ref[...])


def _out_proj(x, a, b, wa, wb, tm):
    m, d = x.shape
    row = lambda width: pl.BlockSpec((tm, width), lambda i: (i, 0))
    return pl.pallas_call(
        _out_proj_kernel, grid=(m // tm,),
        in_specs=[row(d), row(a.shape[1]), row(b.shape[1]), _full(wa.shape), _full(wb.shape)],
        out_specs=row(d), out_shape=jax.ShapeDtypeStruct((m, d), F32),
        compiler_params=_cparams("parallel"), name="out_proj",
    )(x, a, b, wa, wb)


def _ffn_kernel(seq_tiles, t_s, d_ff, x_ref, g_ref, wu_ref, cw_ref, cb_ref, wd_ref, *rest):
    if seq_tiles:
        o_ref, st_ref, g_scr = rest
    else:
        p1_ref, p2_ref, o_ref, st_ref = rest
    x = x_ref[...]
    tm = x.shape[0]
    h = _rms_rows(x, g_ref[...]).astype(BF16)
    u = _dot(h, wu_ref[...])
    gate, val = u[:, :d_ff], u[:, d_ff:]
    if seq_tiles:
        @pl.when(pl.program_id(0) % seq_tiles == 0)
        def _():
            g_scr[0:8, :] = jnp.zeros((8, d_ff), F32)
        g_scr[8:, :] = gate
        g1 = g_scr[7:tm + 7, :]
        g2 = g_scr[6:tm + 6, :]
        tail = gate[tm - 8:, :]
        g_scr[0:8, :] = tail
        st_ref[...] = tail
    else:
        tpos = lax.broadcasted_iota(jnp.int32, (tm, 1), 0) % t_s
        g1 = jnp.where(tpos >= 1, pltpu.roll(gate, 1, 0), p1_ref[...])
        g2 = jnp.where(tpos >= 2, pltpu.roll(gate, 2, 0), p2_ref[...])
        st_ref[...] = gate
    cw = cw_ref[...]
    conv = cb_ref[...] + g2 * cw[0:1] + g1 * cw[1:2] + gate * cw[2:3]
    act = (conv * (1.0 / (1.0 + jnp.exp(-conv))) * val).astype(BF16)
    o_ref[...] = x + _dot(act, wd_ref[...])


def _ffn(x, seq_len, w, tm, prev=None):
    m, d = x.shape
    d_ff = w['w_down'].shape[0]
    row = lambda width: pl.BlockSpec((tm, width), lambda i: (i, 0))
    single = dict(pipeline_mode=pl.Buffered(1))
    wspecs = [_full(w['g_ffn'].shape),
              pl.BlockSpec(w['w_up'].shape, lambda i: (0, 0), **single),
              _full(w['conv_w'].shape), _full(w['conv_b'].shape),
              pl.BlockSpec(w['w_down'].shape, lambda i: (0, 0), **single)]
    wargs = [w['g_ffn'], w['w_up'], w['conv_w'], w['conv_b'], w['w_down']]
    if prev is None:
        seq_tiles = seq_len // tm
        return pl.pallas_call(
            functools.partial(_ffn_kernel, seq_tiles, 0, d_ff), grid=(m // tm,),
            in_specs=[row(d)] + wspecs,
            out_specs=[row(d), pl.BlockSpec((8, d_ff), lambda i: (i, 0))],
            out_shape=[jax.ShapeDtypeStruct((m, d), F32),
                       jax.ShapeDtypeStruct((m // tm * 8, d_ff), F32)],
            scratch_shapes=[pltpu.VMEM((tm + 8, d_ff), F32)],
            compiler_params=_cparams("arbitrary"), name="ffn_prompt",
        )(x, *wargs)
    p1, p2 = prev
    return pl.pallas_call(
        functools.partial(_ffn_kernel, 0, seq_len, d_ff), grid=(m // tm,),
        in_specs=[row(d)] + wspecs + [row(d_ff), row(d_ff)],
        out_specs=[row(d), row(d_ff)],
        out_shape=[jax.ShapeDtypeStruct((m, d), F32), jax.ShapeDtypeStruct((m, d_ff), F32)],
        compiler_params=_cparams("parallel"), name="ffn_sample",
    )(x, *wargs, p1, p2)


QW = N_GROUPS * DIL_HEADS * LANES
KVW = N_GROUPS * DIL_SLAB


def _proj_dil_kernel(prompt, x_ref, ga_ref, win_ref, sq_ref, sqt_ref, invq_ref, gq_ref,
                     sk_ref, skt_ref, invk_ref, gk_ref, vm_ref,
                     sm_ref, smt_ref, invm_ref, gm_ref, *rest):
    x = x_ref[0] if prompt else x_ref[...]
    h = _rms_rows(x, ga_ref[...]).astype(BF16)
    z = _dot(h, win_ref[...])
    q = z[:, :QW]
    qn = q * _seg_scale(q, sq_ref, sqt_ref, invq_ref) * gq_ref[...]
    kv = z[:, QW:QW + KVW]
    kvn = kv * (_seg_scale(kv, sk_ref, skt_ref, invk_ref) * gk_ref[...] + vm_ref[...])
    mq = z[:, QW + KVW:]
    mqn = (mq * _seg_scale(mq, sm_ref, smt_ref, invm_ref) * gm_ref[...]).astype(BF16)
    if not prompt:
        q_out, kv_out, mq_out = rest
        q_out[...] = qn.astype(BF16)
        kv_out[...] = kvn
        mq_out[...] = mqn
        return
    kv_out, mq_out = rest[0], rest[1]
    q_outs, kvb_outs, q_scr, kv_scr = rest[2:5], rest[5:8], rest[8], rest[9]
    kv_out[0] = kvn
    mq_out[0] = mqn
    for c in range(QW // LANES):
        q_scr[c] = qn[:, c * LANES:(c + 1) * LANES]
    for c in range(KVW // LANES):
        kv_scr[c] = kvn[:, c * LANES:(c + 1) * LANES]
    tm = q.shape[0]
    qc, kc = DIL_HEADS, DIL_SLAB // LANES
    for g, d in enumerate(DIL_RATES):
        for r in range(d):
            rows = pl.ds(r, tm // d, stride=d) if d > 1 else slice(None)
            for c in range(qc):
                q_outs[g][0, r, :, c * LANES:(c + 1) * LANES] = q_scr[g * qc + c, rows, :].astype(BF16)
            for c in range(kc):
                kvb_outs[g][0, r, :, c * LANES:(c + 1) * LANES] = kv_scr[g * kc + c, rows, :].astype(BF16)


def _proj_dil(x3, prompt, tm, w):
    n, t, d = x3.shape
    args = [x3, w['g_attn'], w['w_in'], w['Sq'], w['SqT'], w['invq'], w['gq'],
            w['Sk'], w['SkT'], w['invk'], w['gk'], w['vmask'], w['Sm'], w['SmT'], w['invm'], w['gm']]
    if not prompt:
        x2 = x3.reshape(n * t, d)
        m = n * t
        row = lambda width: pl.BlockSpec((tm, width), lambda i: (i, 0))
        return pl.pallas_call(
            functools.partial(_proj_dil_kernel, False), grid=(m // tm,),
            in_specs=[row(d)] + [_full(a.shape) for a in args[1:]],
            out_specs=[row(QW), row(KVW), row(MEM_W)],
            out_shape=[jax.ShapeDtypeStruct((m, QW), BF16), jax.ShapeDtypeStruct((m, KVW), F32),
                       jax.ShapeDtypeStruct((m, MEM_W), BF16)],
            compiler_params=_cparams("parallel"), name="proj_dil_sample",
        )(x2, *args[1:])
    row3 = lambda width: pl.BlockSpec((1, tm, width), lambda b, i: (b, i, 0))
    outs = [jax.ShapeDtypeStruct((n, t, KVW), F32), jax.ShapeDtypeStruct((n, t, MEM_W), BF16)]
    ospecs = [row3(KVW), row3(MEM_W)]
    for width in (DIL_HEADS * LANES, DIL_SLAB):
        for dil in DIL_RATES:
            outs.append(jax.ShapeDtypeStruct((n, dil, t // dil, width), BF16))
            ospecs.append(pl.BlockSpec((1, dil, tm // dil, width), lambda b, i: (b, 0, i, 0)))
    return pl.pallas_call(
        functools.partial(_proj_dil_kernel, True), grid=(n, t // tm),
        in_specs=[row3(d)] + [_full(a.shape) for a in args[1:]],
        out_specs=ospecs, out_shape=outs,
        scratch_shapes=[pltpu.VMEM((QW // LANES, tm, LANES), F32), pltpu.VMEM((KVW // LANES, tm, LANES), F32)],
        compiler_params=_cparams("parallel", "parallel"), name="proj_dil_prompt",
    )(*args)


def _band_kernel(tq, q_ref, kp_ref, kc_ref, bias_ref, o_ref, ml_ref):
    i = pl.program_id(1)
    kcat = jnp.concatenate([kp_ref[0], kc_ref[0]], axis=0)
    first = jnp.logical_and(i == 0, lax.broadcasted_iota(jnp.int32, (tq, 2 * tq), 1) < tq)
    ms, ls = [], []
    scores = [_dot_nt(q_ref[0, :, hd * LANES:(hd + 1) * LANES], kcat[:, hd * DIL_ROW:hd * DIL_ROW + LANES])
              for hd in range(DIL_HEADS)]
    for hd, s in enumerate(scores):
        s = jnp.where(first, NEG, s + bias_ref[hd])
        m = jnp.max(s, axis=-1, keepdims=True)
        p = jnp.exp(s - m)
        l = jnp.sum(p, axis=-1, keepdims=True)
        o_ref[0, :, hd * DIL_ROW:(hd + 1) * DIL_ROW] = _dot(
            (p / l).astype(BF16), kcat[:, hd * DIL_ROW:(hd + 1) * DIL_ROW])
        ms.append(m)
        ls.append(l)
    ml_ref[0] = _cols(ms + ls, LANES)


def _band_attn(q, kv, bias, tq):
    s, l, _ = q.shape
    return pl.pallas_call(
        functools.partial(_band_kernel, tq), grid=(s, l // tq),
        in_specs=[pl.BlockSpec((1, tq, DIL_HEADS * LANES), lambda b, i: (b, i, 0)),
                  pl.BlockSpec((1, tq, DIL_SLAB), lambda b, i: (b, jnp.maximum(i - 1, 0), 0)),
                  pl.BlockSpec((1, tq, DIL_SLAB), lambda b, i: (b, i, 0)),
                  _full(bias.shape)],
        out_specs=[pl.BlockSpec((1, tq, DIL_SLAB), lambda b, i: (b, i, 0)),
                   pl.BlockSpec((1, tq, LANES), lambda b, i: (b, i, 0))],
        out_shape=[jax.ShapeDtypeStruct((s, l, DIL_SLAB), F32),
                   jax.ShapeDtypeStruct((s, l, LANES), F32)],
        compiler_params=_cparams("parallel", "parallel"), name="dil_band_attn",
    )(q, kv, kv, bias)


def _merge_kernel(rates, *refs):
    ng = len(rates)
    o_refs, ml_refs, out_ref = refs[:ng], refs[ng:2 * ng], refs[2 * ng]
    scr = refs[2 * ng + 1:]
    nch = DIL_SLAB // LANES
    chunks, mls = [], []
    for g, d in enumerate(rates):
        if d == 1:
            chunks.append([o_refs[g][0, 0, :, c * LANES:(c + 1) * LANES] for c in range(nch)])
            mls.append(ml_refs[g][0, 0])
        else:
            o_scr, ml_scr = scr[2 * g], scr[2 * g + 1]
            tm = ml_scr.shape[0]
            for r in range(d):
                rows = pl.ds(r, tm // d, stride=d)
                for c in range(nch):
                    o_scr[c, rows, :] = o_refs[g][0, r, :, c * LANES:(c + 1) * LANES]
                ml_scr[rows, :] = ml_refs[g][0, r]
            chunks.append([o_scr[c] for c in range(nch)])
            mls.append(ml_scr[...])
    m_all = functools.reduce(jnp.maximum, [ml[:, :DIL_HEADS] for ml in mls])
    es = [ml[:, DIL_HEADS:2 * DIL_HEADS] * jnp.exp(ml[:, :DIL_HEADS] - m_all) for ml in mls]
    tot = functools.reduce(jnp.add, es)
    ws = [e / tot for e in es]
    for c in range(nch):
        hd = c // (DIL_ROW // LANES)
        acc = None
        for g in range(ng):
            term = chunks[g][c] * ws[g][:, hd:hd + 1]
            acc = term if acc is None else acc + term
        out_ref[0, :, c * LANES:(c + 1) * LANES] = acc.astype(BF16)


def _merge(os_, mls, rates, n, t, tm):
    specs = [pl.BlockSpec((1, d, tm // d, DIL_SLAB), lambda b, i: (b, 0, i, 0)) for d in rates] \
        + [pl.BlockSpec((1, d, tm // d, LANES), lambda b, i: (b, 0, i, 0)) for d in rates]
    scratch = []
    for d in rates:
        scratch += [pltpu.VMEM((DIL_SLAB // LANES, tm, LANES), F32), pltpu.VMEM((tm, LANES), F32)]
    return pl.pallas_call(
        functools.partial(_merge_kernel, tuple(rates)), grid=(n, t // tm),
        in_specs=specs,
        out_specs=pl.BlockSpec((1, tm, DIL_SLAB), lambda b, i: (b, i, 0)),
        out_shape=jax.ShapeDtypeStruct((n, t, DIL_SLAB), BF16),
        scratch_shapes=scratch,
        compiler_params=_cparams("parallel", "parallel"), name="dil_merge",
    )(*os_, *mls)


def _dil_sample_kernel(d, wb, t_s, q_ref, new_ref, buf_ref, bias_ref, o_ref, ml_ref, st_ref, *scr):
    nk = DIL_KEYS - 1
    new = new_ref[0]
    st_ref[0, 0:wb - t_s] = buf_ref[0, t_s:wb]
    st_ref[0, wb - t_s:wb] = new
    if d < t_s:
        src = scr[0]
        src[0:wb] = buf_ref[0]
        src[wb:wb + t_s] = new
    else:
        src = buf_ref.at[0]
    top = lax.broadcasted_iota(jnp.int32, (8, LANES), 0) < DIL_HEADS
    lane = lax.broadcasted_iota(jnp.int32, (8, LANES), 1)
    bias = bias_ref[...]
    nq, ntok = _dil_classes(d, t_s)

    def both_halves(col):
        x = jnp.where(top, jnp.broadcast_to(col, (8, LANES)), 0.0)
        return x + pltpu.roll(x, DIL_HEADS, 0)

    classes = range(t_s // nq)
    toks, scores, stats, outs = [], [], [], []
    for r in classes:
        if d < t_s:
            a3 = src[pl.ds(r, ntok, stride=d)] if d > 1 else src[pl.ds(r, ntok)]
        else:
            pad = jnp.zeros((ntok - nk - 1, 8, LANES), F32)
            a3 = jnp.concatenate([src[pl.ds(r, nk, stride=d)], new[r:r + 1], pad], axis=0)
        a = a3.reshape(ntok * 8, LANES).astype(BF16)
        qc = q_ref[0, pl.ds(r, nq, stride=d)] if nq > 1 else q_ref[0, r:r + 1]
        qc = qc.reshape(nq * 8, LANES).astype(BF16)
        toks.append(a)
        scores.append(_dot_nt(qc, a) + bias)
    for r in classes:
        s = scores[r]
        m = jnp.max(s, axis=-1, keepdims=True)
        p = jnp.exp(s - m)
        l = jnp.sum(p, axis=-1, keepdims=True)
        pn = p / l
        stats.append((m, l))
        p2 = jnp.concatenate([pn, pltpu.roll(pn, DIL_HEADS, 1)], axis=0).astype(BF16)
        outs.append(_dot(p2, toks[r]))
    for r in classes:
        o2, (m, l) = outs[r], stats[r]
        for k in range(nq):
            i = r + k * d
            lo, hi = o2[8 * k:8 * k + 8], o2[8 * (nq + k):8 * (nq + k) + 8]
            o_ref[0, i] = jnp.where(top, lo, pltpu.roll(hi, DIL_HEADS, 0))
            ml_ref[0, i] = jnp.where(lane == 0, both_halves(m[8 * k:8 * k + 8]),
                                     jnp.where(lane == 1, both_halves(l[8 * k:8 * k + 8]), 0.0))


def _dil_classes(d, t_s):
    nq = max(t_s // d, 1)
    ntok = DIL_KEYS - 1 + nq
    return nq, ntok + ntok % 2


def _dil_sample(qg, newg, buf, bias, d):
    b, t_s = qg.shape[:2]
    wb = buf.shape[1]
    blk = lambda r: pl.BlockSpec((1, r, 8, LANES), lambda s: (s, 0, 0, 0))
    tiles = jax.ShapeDtypeStruct((b, t_s, 8, LANES), F32)
    return pl.pallas_call(
        functools.partial(_dil_sample_kernel, d, wb, t_s), grid=(b,),
        in_specs=[blk(t_s), blk(t_s), blk(wb), _full(bias.shape)],
        out_specs=[blk(t_s), blk(t_s), blk(wb)],
        out_shape=[tiles, tiles, jax.ShapeDtypeStruct(buf.shape, F32)],
        scratch_shapes=[pltpu.VMEM((wb + t_s, 8, LANES), F32)] if d < t_s else [],
        compiler_params=_cparams("parallel"), name="dil_sample_attn",
    )(qg, newg, buf, bias)


def _merge_tiles_kernel(ng, *refs):
    o_refs, ml_refs, out_ref = refs[:ng], refs[ng:2 * ng], refs[2 * ng]
    ms = [r[...][:, :, 0:1] for r in ml_refs]
    ls = [r[...][:, :, 1:2] for r in ml_refs]
    m_all = functools.reduce(jnp.maximum, ms)
    es = [l * jnp.exp(m - m_all) for m, l in zip(ms, ls)]
    tot = functools.reduce(jnp.add, es)
    out_ref[...] = functools.reduce(jnp.add, [o[...] * (e / tot) for o, e in zip(o_refs, es)])


def _merge_tiles(os_, mls, tr):
    r = os_[0].shape[0]
    spec = pl.BlockSpec((tr, 8, LANES), lambda i: (i, 0, 0))
    return pl.pallas_call(
        functools.partial(_merge_tiles_kernel, len(os_)), grid=(r // tr,),
        in_specs=[spec] * (2 * len(os_)), out_specs=spec,
        out_shape=jax.ShapeDtypeStruct((r, 8, LANES), F32),
        compiler_params=_cparams("parallel"), name="dil_merge_sample",
    )(*os_, *mls)


def _seg_mats(width, segments):
    s = np.zeros((width, LANES), np.float32)
    inv = np.ones((1, LANES), np.float32)
    for k, (a, n) in enumerate(segments):
        s[a:a + n, k] = 1.0
        inv[0, k] = 1.0 / n
    return jnp.asarray(s, BF16), jnp.asarray(s.T.copy(), BF16), jnp.asarray(inv)


def _pad_heads(wm, heads, dim, slab, offset=0):
    r = wm.shape[0]
    out = jnp.pad(wm.reshape(r, heads, dim), ((0, 0), (0, 0), (offset, slab - dim - offset)))
    return out.reshape(r, heads * slab)


def _mem_q_tables(g_qn_mem):
    seg = [(hd * LANES, MEM_DIM) for hd in range(MEM_HEADS)]
    sm, smt, invm = _seg_mats(MEM_W, seg)
    gm = _pad_heads(jnp.tile(g_qn_mem, MEM_HEADS)[None, :], MEM_HEADS, MEM_DIM, LANES) * MEM_SCALE
    return dict(Sm=sm, SmT=smt, invm=invm, gm=gm)


def _rope_cs(pos):
    inv = ROPE_THETA ** (-jnp.arange(0, ROPE_DIM, 2, dtype=F32) / ROPE_DIM)
    ang = pos.astype(F32)[:, None] * inv[None, :]
    return jnp.cos(ang), jnp.sin(ang)


def _mla_tables(pos, g_qn_nope, g_qn_pe, g_kn_pe):
    cos, sin = _rope_cs(pos)
    t = pos.shape[0]
    hr = ROPE_DIM // 2
    ones = jnp.ones((t, 1), F32)
    cslab = jnp.concatenate([ones * g_qn_nope[None, :], cos * g_qn_pe[None, :hr], cos * g_qn_pe[None, hr:],
                             jnp.zeros((t, LANES - NOPE_DIM - ROPE_DIM), F32)], axis=1) * MLA_SCALE
    sslab = jnp.concatenate([jnp.zeros((t, NOPE_DIM), F32), -sin * g_qn_pe[None, hr:], sin * g_qn_pe[None, :hr],
                             jnp.zeros((t, LANES - NOPE_DIM - ROPE_DIM), F32)], axis=1) * MLA_SCALE
    tk = jnp.concatenate([cos * g_kn_pe[None, :hr], cos * g_kn_pe[None, hr:],
                          -sin * g_kn_pe[None, hr:], sin * g_kn_pe[None, :hr],
                          jnp.zeros((t, LANES - 2 * ROPE_DIM), F32)], axis=1)
    return cslab, sslab, tk


def _prep_mla(i, j, g_attn, w_o, w_in_a, g_q_a, w_q_b, g_kv_a, w_uk, w_uv, g_qn_mem):
    d = w_in_a.shape[1]
    hr = ROPE_DIM // 2
    wi = w_in_a[j]
    o1, o2, o3 = Q_LORA, Q_LORA + KV_LORA, Q_LORA + KV_LORA + ROPE_DIM
    kpe = wi[:, o2:o3]
    w_in = jnp.concatenate([wi[:, :o2], _pad_heads(wi[:, o3:], MEM_HEADS, MEM_DIM, LANES),
                            kpe, jnp.concatenate([kpe[:, hr:], kpe[:, :hr]], axis=1),
                            jnp.zeros((d, LANES - 2 * ROPE_DIM), F32)], axis=1).astype(BF16)
    hq = NOPE_DIM + ROPE_DIM
    wq3 = w_q_b[j].reshape(Q_LORA, MLA_HEADS, hq)
    wq = _pad_heads(w_q_b[j], MLA_HEADS, hq, LANES).astype(BF16)
    sw = jnp.concatenate([wq3[:, :, NOPE_DIM + hr:], wq3[:, :, NOPE_DIM:NOPE_DIM + hr]], axis=2)
    wq_sw = _pad_heads(sw.reshape(Q_LORA, MLA_HEADS * ROPE_DIM), MLA_HEADS, ROPE_DIM, LANES, NOPE_DIM).astype(BF16)
    seg = []
    for hd in range(MLA_HEADS):
        seg += [(hd * LANES, NOPE_DIM), (hd * LANES + NOPE_DIM, ROPE_DIM)]
    s, st, inv = _seg_mats(MLA_HEADS * LANES, seg)
    wk_c = _pad_heads(w_uk[j].reshape(KV_LORA, MLA_HEADS * NOPE_DIM), MLA_HEADS, NOPE_DIM, LANES).astype(BF16)
    eye = np.zeros((LANES, MLA_HEADS, LANES), np.float32)
    for r in range(ROPE_DIM):
        eye[r, :, NOPE_DIM + r] = 1.0
    wk_p = jnp.asarray(eye.reshape(LANES, MLA_HEADS * LANES), BF16)
    wv4 = w_uv[j].reshape(KV_LORA, MLA_HEADS // 2, 2, MLA_V_DIM)
    lo = ((0, 0), (0, 0), (0, LANES - MLA_V_DIM))
    hi = ((0, 0), (0, 0), (LANES - MLA_V_DIM, 0))
    wv = jnp.stack([jnp.pad(wv4[:, :, 0], lo), jnp.pad(wv4[:, :, 1], hi)], axis=2)
    wv = wv.reshape(KV_LORA, MLA_HEADS * LANES).astype(BF16)
    vone = np.zeros((1, MLA_HEADS, LANES), np.float32)
    vone[0, 0::2, MLA_V_DIM] = 1.0
    vone[0, 1::2, 0] = 1.0
    vone = jnp.asarray(vone.reshape(1, MLA_HEADS * LANES))
    pe_rows = np.zeros((ROPE_DIM, MLA_QCAT), np.float32)
    pe_rows[np.arange(ROPE_DIM), KV_LORA + np.arange(ROPE_DIM)] = 1.0
    wcat = jnp.concatenate([
        jnp.pad(jnp.transpose(w_uk[j], (1, 2, 0)), ((0, 0), (0, 0), (0, MLA_QCAT - KV_LORA))),
        jnp.broadcast_to(jnp.asarray(pe_rows)[None], (MLA_HEADS, ROPE_DIM, MLA_QCAT)),
        jnp.zeros((MLA_HEADS, LANES - hq, MLA_QCAT), F32)], axis=1)
    wvp = jnp.transpose(wv4, (1, 2, 0, 3))
    uvp = jnp.concatenate([jnp.pad(wvp[:, 0], lo), jnp.pad(wvp[:, 1], hi)], axis=1)
    ntok = MLA_HEADS * MLA_V_DIM
    w = dict(g_attn=g_attn[i][None], w_in=w_in, g_q_a=g_q_a[j][None], wq=wq, wq_sw=wq_sw,
             g_kv_a=g_kv_a[j][None], S=s, ST=st, inv=inv, wk_c=wk_c, wk_p=wk_p, wv=wv, vone=vone,
             wcat=wcat.astype(BF16), uv_pairs=uvp.astype(BF16),
             wo_tok=w_o[i][:ntok].astype(BF16),
             wo_mem=_pad_rows(w_o[i][ntok:], MEM_HEADS, MEM_DIM, LANES, MEM_DIM).astype(BF16))
    w.update(_mem_q_tables(g_qn_mem[i]))
    return w


def _pad_rows(wm, heads, dim, slab, offset):
    c = wm.shape[1]
    out = jnp.pad(wm.reshape(heads, dim, c), ((0, 0), (offset, slab - dim - offset), (0, 0)))
    return out.reshape(heads * slab, c)


def _prep_dil(i, j, g_attn, w_o, w_in_b, g_qn_b, g_kn_b, g_qn_mem):
    wi = w_in_b[j]
    gh = N_GROUPS * DIL_HEADS
    nqk = gh * DIL_QK
    wq = _pad_heads(wi[:, :nqk], gh, DIL_QK, LANES)
    wk = wi[:, nqk:2 * nqk].reshape(-1, gh, DIL_QK)
    wv = wi[:, 2 * nqk:2 * nqk + gh * DIL_V].reshape(-1, gh, DIL_V)
    wkv = jnp.concatenate([wk, wv], axis=2).reshape(-1, gh * DIL_ROW)
    wmq = _pad_heads(wi[:, 2 * nqk + gh * DIL_V:], MEM_HEADS, MEM_DIM, LANES)
    w_in = jnp.concatenate([wq, wkv, wmq], axis=1).astype(BF16)
    sq, sqt, invq = _seg_mats(QW, [(k * LANES, DIL_QK) for k in range(gh)])
    sk, skt, invk = _seg_mats(KVW, [(k * DIL_ROW, DIL_QK) for k in range(gh)])
    gq = _pad_heads(jnp.repeat(g_qn_b[j], DIL_HEADS, axis=0).reshape(1, nqk), gh, DIL_QK, LANES) * DIL_SCALE
    gk = _pad_heads(jnp.repeat(g_kn_b[j], DIL_HEADS, axis=0).reshape(1, nqk), gh, DIL_QK, DIL_ROW)
    vmask = _pad_heads(jnp.ones((1, gh * DIL_V), F32), gh, DIL_V, DIL_ROW, DIL_QK)
    ntok = DIL_HEADS * DIL_V
    w = dict(g_attn=g_attn[i][None], w_in=w_in, Sq=sq, SqT=sqt, invq=invq, gq=gq,
             Sk=sk, SkT=skt, invk=invk, gk=gk, vmask=vmask,
             wo_tok=_pad_rows(w_o[i][:ntok], DIL_HEADS, DIL_V, DIL_ROW, DIL_QK).astype(BF16),
             wo_mem=_pad_rows(w_o[i][ntok:], MEM_HEADS, MEM_DIM, LANES, MEM_DIM).astype(BF16))
    w.update(_mem_q_tables(g_qn_mem[i]))
    return w


def _prep_mem(i, g_mem, w_mem_kv, g_kn_mem):
    sk, skt, invk = _seg_mats(MEM_W, [(hd * LANES, MEM_DIM) for hd in range(MEM_HEADS)])
    gk = _pad_heads(jnp.tile(g_kn_mem[i], MEM_HEADS)[None, :], MEM_HEADS, MEM_DIM, LANES)
    vmask = _pad_heads(jnp.ones((1, MEM_HEADS * MEM_DIM), F32), MEM_HEADS, MEM_DIM, LANES, MEM_DIM)
    return dict(g_mem=g_mem[i][None], w_mem=w_mem_kv[i].astype(BF16), Sk=sk, SkT=skt, invk=invk, gk=gk, vmask=vmask)


def _prep_ffn(i, g_ffn, w_up, conv_w, conv_b, w_down):
    return dict(g_ffn=g_ffn[i][None], w_up=w_up[i].astype(BF16), conv_w=conv_w[i],
                conv_b=conv_b[i][None], w_down=w_down[i].astype(BF16))


def _rel_bucket(dist):
    max_exact = N_BUCKETS // 2
    dd = jnp.maximum(dist.astype(F32), 1.0)
    large = max_exact + (jnp.log(dd / max_exact) / math.log(MAX_DISTANCE / max_exact)
                         * (N_BUCKETS - max_exact)).astype(jnp.int32)
    large = jnp.minimum(large, N_BUCKETS - 1)
    return jnp.where(dist < max_exact, dist, large)


def _dil_bias_tables(rel_bias, tq, t_s):
    band, samp = [], []

    def toeplitz(rev, rows, cols):
        period = cols + rows + 1
        f = jnp.concatenate([rev, jnp.full((DIL_HEADS, period - DIL_KEYS), NEG, F32)], axis=1)
        return jnp.tile(f, (1, rows))[:, :rows * (period - 1)].reshape(DIL_HEADS, rows, period - 1)[:, :, :cols]

    srow = jnp.arange(8)[None, None, None, :] == jnp.arange(DIL_HEADS)[:, None, None, None]
    for g in range(N_GROUPS):
        bk = _rel_bucket(DIL_RATES[g] * jnp.arange(DIL_KEYS, dtype=jnp.int32))
        b = rel_bias[bk][:, g * DIL_HEADS:(g + 1) * DIL_HEADS].T.astype(F32)
        rev = b[:, ::-1]
        band.append(toeplitz(rev, tq, 2 * tq))
        nq, ntok = _dil_classes(DIL_RATES[g], t_s)
        tab = jnp.where(srow, toeplitz(rev, nq, ntok)[..., None], NEG).reshape(DIL_HEADS, nq, ntok * 8)
        tab = jnp.pad(tab.transpose(1, 0, 2), ((0, 0), (0, 8 - DIL_HEADS), (0, 0)), constant_values=NEG)
        samp.append(tab.reshape(nq * 8, ntok * 8))
    return band, samp


def kernel(x_prompt, x_sample, cache_mla, state_win0, state_win1, state_win2, cache_mem, state_conv, page_table, mem_prompt, rel_bias, g_attn, w_o, w_in_a, g_q_a, w_q_b, g_kv_a, w_uk, w_uv, g_qn_nope, g_qn_pe, g_kn_pe, w_in_b, g_qn_b, g_kn_b, g_mem, w_mem_kv, g_qn_mem, g_kn_mem, g_ffn, w_up, conv_w, conv_b, w_down):
    state_wins = (state_win0, state_win1, state_win2)
    n, t_p, d = x_prompt.shape
    b, t_s, _ = x_sample.shape
    n_mem = mem_prompt.shape[1]
    d_ff = w_down.shape[1]
    past = page_table.shape[1] * cache_mla.shape[2]
    mp, ms = n * t_p, b * t_s
    tm_p = min(256, t_p)
    tm_s = min(256, ms)
    tq_band = DIL_KEYS - 1
    assert t_s >= 2 and t_p % tm_p == 0 and ms % tm_s == 0 and tm_s % t_s == 0
    assert all(sw.shape[2] == wdw and wdw == (DIL_KEYS - 1) * r
               for sw, wdw, r in zip(state_wins, DIL_WINDOWS, DIL_RATES))

    xp = x_prompt.reshape(mp, d)
    xs = x_sample.reshape(ms, d)
    mem2 = mem_prompt.reshape(n * n_mem, d)
    pos_p = jnp.tile(jnp.arange(t_p), n)
    pos_s = jnp.tile(past + jnp.arange(t_s), b)
    band_bias, samp_bias = _dil_bias_tables(rel_bias, tq_band, t_s)
    cache_rows = cache_mem.reshape(cache_mem.shape[0], b, n_mem * MEM_HEADS, 2 * MEM_DIM)

    mem_out, conv_p_out, conv_s_out = [], [], []
    for i in range(2):
        j = i // 2
        if i % 2 == 0:
            w = _prep_mla(i, j, g_attn, w_o, w_in_a, g_q_a, w_q_b, g_kv_a, w_uk, w_uv, g_qn_mem)
            cq, sq, tk = _mla_tables(pos_p, g_qn_nope[j], g_qn_pe[j], g_kn_pe[j])
            q_p, rows_p, mq_p, k_p, v_p = _proj_mla(xp, True, tm_p, w, cq, sq, tk)
            cq, sq, tk = _mla_tables(pos_s, g_qn_nope[j], g_qn_pe[j], g_kn_pe[j])
            qc_s, rows_s, mq_s = _proj_mla(xs, False, tm_s, w, cq, sq, tk)
            hw = MLA_HEADS * LANES
            tok_p = _mla_flash(q_p.reshape(n, t_p, hw), k_p.reshape(n, t_p, hw), v_p.reshape(n, t_p, hw),
                               min(512, t_p)).reshape(mp, -1)
            q3 = qc_s.reshape(b, t_s, MLA_HEADS, MLA_QCAT).transpose(0, 2, 1, 3).reshape(b, MLA_HEADS * t_s, MLA_QCAT)
            ctx = _mla_sample_attn(q3, rows_s.reshape(b, t_s, MLA_ROW), cache_mla[j].transpose(0, 2, 1),
                                   page_table, min(MLA_PAGES_PER_STEP, page_table.shape[1]))
            ctx = ctx.reshape(b, MLA_HEADS, t_s, KV_LORA).transpose(0, 2, 1, 3).reshape(ms, MLA_HEADS * KV_LORA)
            tok_s = _mla_uv(ctx, w['uv_pairs'], tm_s)
            mla_rows_p = rows_p.reshape(1, n, t_p, MLA_ROW)
            mla_rows_s = rows_s.reshape(1, b, t_s, MLA_ROW)
        else:
            w = _prep_dil(i, j, g_attn, w_o, w_in_b, g_qn_b, g_kn_b, g_qn_mem)
            res = _proj_dil(xp.reshape(n, t_p, d), True, tm_p, w)
            kv_p, mq_p = res[0], res[1].reshape(mp, MEM_W)
            os_, mls = [], []
            for g, dil in enumerate(DIL_RATES):
                l = t_p // dil
                o, ml = _band_attn(res[2 + g].reshape(n * dil, l, DIL_HEADS * LANES),
                                   res[5 + g].reshape(n * dil, l, DIL_SLAB), band_bias[g], tq_band)
                os_.append(o.reshape(n, dil, l, DIL_SLAB))
                mls.append(ml.reshape(n, dil, l, LANES))
            tok_p = _merge(os_, mls, DIL_RATES, n, t_p, tm_p).reshape(mp, DIL_SLAB)
            win_p = [kv_p[:, t_p - min(wdw, t_p):, g * DIL_SLAB:(g + 1) * DIL_SLAB]
                     .reshape(1, n, min(wdw, t_p), DIL_HEADS, DIL_ROW) for g, wdw in enumerate(DIL_WINDOWS)]
            q_s, kv_s, mq_s = _proj_dil(xs.reshape(b, t_s, d), False, tm_s, w)
            nc = DIL_ROW // LANES
            q_t = jnp.pad(q_s.astype(F32).reshape(b, t_s, N_GROUPS, DIL_HEADS, LANES),
                          ((0, 0), (0, 0), (0, 0), (0, 8 - DIL_HEADS), (0, 0))).transpose(2, 0, 1, 3, 4)
            kv_t = kv_s.reshape(b, t_s, N_GROUPS, DIL_HEADS, nc, LANES).transpose(2, 0, 1, 4, 3, 5)
            kv_t = kv_t.reshape(N_GROUPS, b, t_s, 8, LANES)
            os_, mls, win_s = [], [], []
            for g, dil in enumerate(DIL_RATES):
                buf = state_wins[g][j]
                wb = buf.shape[1]
                buf_t = buf.reshape(b, wb, DIL_HEADS, nc, LANES).transpose(0, 1, 3, 2, 4).reshape(b, wb, 8, LANES)
                o, ml, st = _dil_sample(q_t[g], kv_t[g], buf_t, samp_bias[g], dil)
                os_.append(o.reshape(ms, 8, LANES))
                mls.append(ml.reshape(ms, 8, LANES))
                st = st.reshape(b, wb, nc, DIL_HEADS, LANES).transpose(0, 1, 3, 2, 4)
                win_s.append(st.reshape(1, b, wb, DIL_HEADS, DIL_ROW))
            tok_s = _merge_tiles(os_, mls, tm_s).reshape(ms, nc, DIL_HEADS, LANES).transpose(0, 2, 1, 3)
            tok_s = tok_s.reshape(ms, DIL_SLAB).astype(BF16)

        wm = _prep_mem(i, g_mem, w_mem_kv, g_kn_mem)
        mkv = _mem_kv(mem2, wm, min(256, n * n_mem))
        mem_out.append(mkv.reshape(n, n_mem, MEM_HEADS, 2 * MEM_DIM))
        mo_p = _mem_attn(mq_p.reshape(n, t_p, MEM_W), mkv.reshape(n, n_mem, MEM_W), min(512, t_p)).reshape(mp, MEM_W)
        mo_s = _mem_attn_rows(mq_s.reshape(b, t_s * MEM_HEADS, LANES), cache_rows, i, min(8, b)).reshape(ms, MEM_W)
        xp = _out_proj(xp, tok_p, mo_p, w['wo_tok'], w['wo_mem'], tm_p)
        xs = _out_proj(xs, tok_s, mo_s, w['wo_tok'], w['wo_mem'], tm_s)

        wf = _prep_ffn(i, g_ffn, w_up, conv_w, conv_b, w_down)
        xp, tails = _ffn(xp, t_p, wf, tm_p)
        conv_p_out.append(tails.reshape(n, t_p // tm_p, 8, d_ff)[:, -1, 6:, :])
        prev = state_conv[i]
        zeros = jnp.zeros((b, t_s - 2, d_ff), F32)
        p2 = jnp.concatenate([prev, zeros], axis=1).reshape(ms, d_ff)
        p1 = jnp.concatenate([prev[:, 1:], zeros, jnp.zeros((b, 1, d_ff), F32)], axis=1).reshape(ms, d_ff)
        xs, gates = _ffn(xs, t_s, wf, tm_s, prev=(p1, p2))
        conv_s_out.append(gates.reshape(b, t_s, d_ff)[:, t_s - 2:, :])

    return (xp.reshape(n, t_p, d), xs.reshape(b, t_s, d), mla_rows_p, mla_rows_s,
            win_p[0], win_p[1], win_p[2], win_s[0], win_s[1], win_s[2],
            jnp.stack(mem_out), jnp.stack(conv_p_out), jnp.stack(conv_s_out))
```

```python
import functools
import math

import jax
import jax.numpy as jnp
import numpy as np
from jax import lax
from jax.experimental import pallas as pl
from jax.experimental.pallas import tpu as pltpu

F32 = jnp.float32
BF16 = jnp.bfloat16

EPS = 1e-6
LANES = 128
VMEM_LIMIT = 48 * 1024 * 1024

MLA_HEADS = 12
NOPE_DIM = 64
ROPE_DIM = 32
MLA_V_DIM = 64
Q_LORA = 256
KV_LORA = 256
MLA_ROW = KV_LORA + ROPE_DIM
ROPE_THETA = 10000.0
MLA_SCALE = 1.0 / math.sqrt(NOPE_DIM + ROPE_DIM)
MLA_QCAT = 384
MLA_PAGES_PER_STEP = 32
MLA_PAGE_CHUNK = 16
FLASH_HEAD_GROUP = 4

DIL_WINDOWS = (128, 512, 2048)
DIL_RATES = (1, 4, 16)
N_GROUPS = 3
DIL_HEADS = 4
DIL_QK = 64
DIL_V = 192
DIL_KEYS = DIL_WINDOWS[0] // DIL_RATES[0] + 1
DIL_SCALE = 1.0 / math.sqrt(DIL_QK)
DIL_ROW = DIL_QK + DIL_V
DIL_SLAB = DIL_HEADS * DIL_ROW

MEM_HEADS = 4
MEM_DIM = 64
MEM_SCALE = 1.0 / math.sqrt(MEM_DIM)
MEM_W = MEM_HEADS * LANES

N_BUCKETS = 32
MAX_DISTANCE = 2048
NEG = -1e30


def _cparams(*sem):
    return pltpu.CompilerParams(dimension_semantics=sem, vmem_limit_bytes=VMEM_LIMIT)


def _dot(a, b):
    return jnp.dot(a, b, preferred_element_type=F32)


def _dot_nt(a, b):
    return lax.dot_general(a, b, (((1,), (1,)), ((), ())), preferred_element_type=F32)


def _split_dot(x, w):
    hi = x.astype(BF16)
    lo = (x - hi.astype(F32)).astype(BF16)
    return _dot(hi, w) + _dot(lo, w)


def _rms_rows(x, g):
    return x * lax.rsqrt(jnp.mean(x * x, axis=-1, keepdims=True) + EPS) * g


def _seg_scale(x, s_ref, st2_ref, inv_ref):
    ss = _dot((x * x).astype(BF16), s_ref[...])
    r = lax.rsqrt(ss * inv_ref[...] + EPS)
    hi = r.astype(BF16)
    lo = (r - hi.astype(F32)).astype(BF16)
    return _dot(jnp.concatenate([hi, lo], axis=1), st2_ref[...])


def _cols(vals, width):
    rows = vals[0].shape[0]
    lane = lax.broadcasted_iota(jnp.int32, (rows, width), 1)
    out = jnp.zeros((rows, width), F32)
    for k, v in enumerate(vals):
        out = jnp.where(lane == k, v, out)
    return out


def _full(shape):
    return pl.BlockSpec(shape, lambda *_: (0,) * len(shape))


def _proj_mla_kernel(prompt, x_ref, ga_ref, win_ref, gqa_ref, wq_ref, wqs_ref, gkv_ref,
                     cq_ref, sq_ref, tk_ref, s_ref, st_ref, inv_ref,
                     sm_ref, smt_ref, invm_ref, gm_ref, *rest):
    if prompt:
        wkc_ref, wkp_ref, wv_ref, vone_ref, q_out, rows_out, mq_out, k_out, v_out = rest
    else:
        wcat_ref, q_out, rows_out, mq_out = rest
    h = _rms_rows(x_ref[...], ga_ref[...]).astype(BF16)
    z = _dot(h, win_ref[...])
    qa = _rms_rows(z[:, :Q_LORA], gqa_ref[...]).astype(BF16)
    q = _dot(qa, wq_ref[...])
    qs = _dot(qa, wqs_ref[...])
    scale = _seg_scale(q, s_ref, st_ref, inv_ref)
    cq = jnp.concatenate([cq_ref[...]] * MLA_HEADS, axis=1)
    sq = jnp.concatenate([sq_ref[...]] * MLA_HEADS, axis=1)
    qf = ((q * cq + qs * sq) * scale).astype(BF16)
    c = _rms_rows(z[:, Q_LORA:Q_LORA + KV_LORA], gkv_ref[...])
    zk = z[:, 1024:1152]
    kp = zk[:, :ROPE_DIM]
    rk = lax.rsqrt(jnp.mean(kp * kp, axis=-1, keepdims=True) + EPS)
    t = zk * tk_ref[...] * rk
    kr = t + pltpu.roll(t, LANES - ROPE_DIM, 1)
    rows_out[:, :KV_LORA] = c
    rows_out[:, KV_LORA:] = kr[:, :ROPE_DIM]
    mq = z[:, 512:1024]
    mscale = _seg_scale(mq, sm_ref, smt_ref, invm_ref)
    mq_out[...] = (mq * mscale * gm_ref[...]).astype(BF16)
    if prompt:
        q_out[...] = qf
        cb = c.astype(BF16)
        k_out[...] = (_dot(cb, wkc_ref[...]) + _dot(kr.astype(BF16), wkp_ref[...])).astype(BF16)
        v_out[...] = (_dot(cb, wv_ref[...]) + vone_ref[...]).astype(BF16)
    else:
        for hd in range(MLA_HEADS):
            q_out[:, hd * MLA_QCAT:(hd + 1) * MLA_QCAT] = _dot(
                qf[:, hd * LANES:(hd + 1) * LANES], wcat_ref[hd]).astype(BF16)


def _proj_mla(x, prompt, tm, w, cq, sq, tk):
    m, d = x.shape
    hw = MLA_HEADS * LANES
    row = lambda width: pl.BlockSpec((tm, width), lambda i: (i, 0))
    args = [x, w['g_attn'], w['w_in'], w['g_q_a'], w['wq'], w['wq_sw'], w['g_kv_a'],
            cq, sq, tk, w['S'], w['ST'], w['inv'], w['Sm'], w['SmT'], w['invm'], w['gm']]
    specs = [row(d)] + [_full(a.shape) for a in args[1:7]] + [row(LANES), row(LANES), row(LANES)] \
        + [_full(a.shape) for a in args[10:]]
    if prompt:
        extra = [w['wk_c'], w['wk_p'], w['wv'], w['vone']]
        outs = [jax.ShapeDtypeStruct((m, hw), BF16), jax.ShapeDtypeStruct((m, MLA_ROW), F32),
                jax.ShapeDtypeStruct((m, MEM_W), BF16), jax.ShapeDtypeStruct((m, hw), BF16),
                jax.ShapeDtypeStruct((m, hw), BF16)]
        ospecs = [row(hw), row(MLA_ROW), row(MEM_W), row(hw), row(hw)]
    else:
        extra = [w['wcat']]
        outs = [jax.ShapeDtypeStruct((m, MLA_HEADS * MLA_QCAT), BF16),
                jax.ShapeDtypeStruct((m, MLA_ROW), F32), jax.ShapeDtypeStruct((m, MEM_W), BF16)]
        ospecs = [row(MLA_HEADS * MLA_QCAT), row(MLA_ROW), row(MEM_W)]
    return pl.pallas_call(
        functools.partial(_proj_mla_kernel, prompt),
        grid=(m // tm,),
        in_specs=specs + [_full(a.shape) for a in extra],
        out_specs=ospecs, out_shape=outs,
        compiler_params=_cparams("parallel"),
        name="proj_mla_prompt" if prompt else "proj_mla_sample",
    )(*args, *extra)


def _flash_kernel(tq, q_ref, k_ref, v_ref, o_ref, m_scr, acc_scr):
    i, j = pl.program_id(1), pl.program_id(2)

    @pl.when(j == 0)
    def _():
        m_scr[...] = jnp.full(m_scr.shape, NEG, F32)
        acc_scr[...] = jnp.zeros(acc_scr.shape, F32)

    def step(masked):
        if masked:
            keep = lax.broadcasted_iota(jnp.int32, (tq, tq), 1) <= lax.broadcasted_iota(jnp.int32, (tq, tq), 0)
        for h0 in range(0, MLA_HEADS, FLASH_HEAD_GROUP):
            heads = range(h0, h0 + FLASH_HEAD_GROUP)
            scores = [_dot_nt(q_ref[0, :, hd * LANES:(hd + 1) * LANES], k_ref[0, :, hd * LANES:(hd + 1) * LANES])
                      for hd in heads]
            for hd, s in zip(heads, scores):
                if masked:
                    s = jnp.where(keep, s, NEG)
                m_old = m_scr[hd]
                m_new = jnp.maximum(m_old, jnp.max(s, axis=-1, keepdims=True))
                alpha = jnp.exp(m_old - m_new)
                p = jnp.exp(s - jnp.concatenate([m_new] * (tq // LANES), axis=1))
                acc_scr[hd] = alpha * acc_scr[hd] + _dot(p.astype(BF16), v_ref[0, :, hd * LANES:(hd + 1) * LANES])
                m_scr[hd] = m_new

    @pl.when(j < i)
    def _():
        step(False)

    @pl.when(j == i)
    def _():
        step(True)
        low = lax.broadcasted_iota(jnp.int32, (tq, LANES), 1) < MLA_V_DIM
        for pr in range(MLA_HEADS // 2):
            ae, ao = acc_scr[2 * pr], acc_scr[2 * pr + 1]
            o_ref[0, :, pr * LANES:(pr + 1) * LANES] = jnp.where(
                low, ae / ae[:, MLA_V_DIM:MLA_V_DIM + 1], ao / ao[:, 0:1]).astype(BF16)


def _mla_flash(q, k, v, tq):
    n, t, hw = q.shape
    nq = t // tq
    return pl.pallas_call(
        functools.partial(_flash_kernel, tq),
        grid=(n, nq, nq),
        in_specs=[pl.BlockSpec((1, tq, hw), lambda b, i, j: (b, i, 0)),
                  pl.BlockSpec((1, tq, hw), lambda b, i, j: (b, jnp.minimum(i, j), 0)),
                  pl.BlockSpec((1, tq, hw), lambda b, i, j: (b, jnp.minimum(i, j), 0))],
        out_specs=pl.BlockSpec((1, tq, MLA_HEADS * MLA_V_DIM), lambda b, i, j: (b, i, 0)),
        out_shape=jax.ShapeDtypeStruct((n, t, MLA_HEADS * MLA_V_DIM), BF16),
        scratch_shapes=[pltpu.VMEM((MLA_HEADS, tq, LANES), F32), pltpu.VMEM((MLA_HEADS, tq, LANES), F32)],
        compiler_params=_cparams("parallel", "parallel", "arbitrary"),
        name="mla_flash_prompt",
    )(q, k, v)


def _mla_sample_kernel(npg, t_s, pt_ref, q_ref, new_ref, *rest):
    pages, (o_ref, m_scr, l_scr, acc_scr) = rest[:npg], rest[npg:]
    c = pl.program_id(1)

    @pl.when(c == 0)
    def _():
        m_scr[...] = jnp.full(m_scr.shape, NEG, F32)
        l_scr[...] = jnp.zeros(l_scr.shape, F32)
        acc_scr[...] = jnp.zeros(acc_scr.shape, F32)

    q = q_ref[0][:, :MLA_ROW]

    def update(state, s, pv):
        m_old, l_old, acc = state
        m_new = jnp.maximum(m_old, jnp.max(s, axis=-1, keepdims=True))
        alpha = jnp.exp(m_old - m_new)
        p = jnp.exp(s - m_new[:, 0:1])
        l_new = alpha * l_old + jnp.sum(p, axis=-1, keepdims=True)
        acc = jnp.concatenate([alpha] * (KV_LORA // LANES), axis=1) * acc + pv(p.astype(BF16))
        return m_new, l_new, acc

    def load(c0):
        return jnp.concatenate([pg[0].astype(BF16) for pg in pages[c0:c0 + MLA_PAGE_CHUNK]], axis=1)

    starts = list(range(0, npg, MLA_PAGE_CHUNK))
    state = (m_scr[...], l_scr[...], acc_scr[...])
    kts = {0: load(starts[0])}
    scores = {0: _dot(q, kts[0])}
    if len(starts) > 1:
        kts[1] = load(starts[1])
    for ch in range(len(starts)):
        if ch + 1 < len(starts):
            scores[ch + 1] = _dot(q, kts[ch + 1])
        if ch + 2 < len(starts):
            kts[ch + 2] = load(starts[ch + 2])
        state = update(state, scores.pop(ch), lambda p, kt=kts.pop(ch): _dot_nt(p, kt[:KV_LORA]))
    m_scr[...], l_scr[...], acc_scr[...] = state

    @pl.when(c == pl.num_programs(1) - 1)
    def _():
        nb = new_ref[0].astype(BF16)
        s = _dot_nt(q, nb)
        rows = q.shape[0]
        tq = lax.broadcasted_iota(jnp.int32, (rows, t_s), 0) % t_s
        s = jnp.where(lax.broadcasted_iota(jnp.int32, (rows, t_s), 1) <= tq, s, NEG)
        _, l_fin, acc = update((m_scr[...], l_scr[...], acc_scr[...]), s, lambda p: _dot(p, nb[:, :KV_LORA]))
        o_ref[0] = (acc / jnp.concatenate([l_fin] * (KV_LORA // LANES), axis=1)).astype(BF16)


def _mla_sample_attn(q3, rows3, pool_t, page_table, npg):
    b, rows, _ = q3.shape
    t_s = rows3.shape[1]
    n_pages = page_table.shape[1]
    psz = pool_t.shape[2]

    def page_spec(k):
        return pl.BlockSpec((1, MLA_ROW, psz), lambda s, c, pt: (pt[s * n_pages + c * npg + k], 0, 0))

    return pl.pallas_call(
        functools.partial(_mla_sample_kernel, npg, t_s),
        grid_spec=pltpu.PrefetchScalarGridSpec(
            num_scalar_prefetch=1,
            grid=(b, n_pages // npg),
            in_specs=[pl.BlockSpec((1, rows, MLA_QCAT), lambda s, c, pt: (s, 0, 0)),
                      pl.BlockSpec((1, t_s, MLA_ROW), lambda s, c, pt: (s, 0, 0))]
            + [page_spec(k) for k in range(npg)],
            out_specs=pl.BlockSpec((1, rows, KV_LORA), lambda s, c, pt: (s, 0, 0)),
            scratch_shapes=[pltpu.VMEM((rows, LANES), F32), pltpu.VMEM((rows, LANES), F32),
                            pltpu.VMEM((rows, KV_LORA), F32)]),
        out_shape=jax.ShapeDtypeStruct((b, rows, KV_LORA), BF16),
        compiler_params=_cparams("parallel", "arbitrary"),
        name="mla_sample_attn",
    )(page_table.reshape(-1), q3, rows3, *([pool_t] * npg))


def _uv_kernel(ctx_ref, w_ref, o_ref):
    for pr in range(MLA_HEADS // 2):
        o_ref[:, pr * LANES:(pr + 1) * LANES] = _dot(
            ctx_ref[:, pr * 2 * KV_LORA:(pr + 1) * 2 * KV_LORA], w_ref[pr]).astype(BF16)


def _mla_uv(ctx, w_pairs, tm):
    m = ctx.shape[0]
    return pl.pallas_call(
        _uv_kernel, grid=(m // tm,),
        in_specs=[pl.BlockSpec((tm, ctx.shape[1]), lambda i: (i, 0)), _full(w_pairs.shape)],
        out_specs=pl.BlockSpec((tm, MLA_HEADS * MLA_V_DIM), lambda i: (i, 0)),
        out_shape=jax.ShapeDtypeStruct((m, MLA_HEADS * MLA_V_DIM), BF16),
        compiler_params=_cparams("parallel"), name="mla_sample_uv",
    )(ctx, w_pairs)


def _mem_kv_kernel(x_ref, g_ref, w_ref, s_ref, st_ref, inv_ref, gk_ref, vm_ref, o_ref):
    h = _rms_rows(x_ref[...], g_ref[...]).astype(BF16)
    kv = _dot(h, w_ref[...])
    scale = _seg_scale(kv, s_ref, st_ref, inv_ref) * gk_ref[...] + vm_ref[...]
    o_ref[...] = kv * scale


def _mem_kv(mem2, w, tm):
    m, d = mem2.shape
    args = [mem2, w['g_mem'], w['w_mem'], w['Sk'], w['SkT'], w['invk'], w['gk'], w['vmask']]
    return pl.pallas_call(
        _mem_kv_kernel, grid=(m // tm,),
        in_specs=[pl.BlockSpec((tm, d), lambda i: (i, 0))] + [_full(a.shape) for a in args[1:]],
        out_specs=pl.BlockSpec((tm, MEM_W), lambda i: (i, 0)),
        out_shape=jax.ShapeDtypeStruct((m, MEM_W), F32),
        compiler_params=_cparams("parallel"), name="mem_kv",
    )(*args)


def _mem_attn_body(q, kv):
    outs = []
    scores = [_dot_nt(q[:, hd * LANES:(hd + 1) * LANES], kv[:, hd * LANES:(hd + 1) * LANES])
              for hd in range(MEM_HEADS)]
    for hd, s in enumerate(scores):
        sl = slice(hd * LANES, (hd + 1) * LANES)
        p = jnp.exp(s - jnp.max(s, axis=-1, keepdims=True))
        p = p / jnp.sum(p, axis=-1, keepdims=True)
        outs.append(_dot(p.astype(BF16), kv[:, sl]))
    return jnp.concatenate(outs, axis=-1)


def _mem_attn_kernel(q_ref, kv_ref, o_ref):
    o_ref[0] = _mem_attn_body(q_ref[0], kv_ref[0].astype(BF16)).astype(BF16)


def _mem_attn(q3, kv3, tq):
    n, t, _ = q3.shape
    return pl.pallas_call(
        _mem_attn_kernel, grid=(n, t // tq),
        in_specs=[pl.BlockSpec((1, tq, MEM_W), lambda b, i: (b, i, 0)),
                  pl.BlockSpec((1, kv3.shape[1], MEM_W), lambda b, i: (b, 0, 0))],
        out_specs=pl.BlockSpec((1, tq, MEM_W), lambda b, i: (b, i, 0)),
        out_shape=jax.ShapeDtypeStruct((n, t, MEM_W), BF16),
        compiler_params=_cparams("parallel", "parallel"), name="mem_attn",
    )(q3, kv3)


def _mem_rows_kernel(sb, q_ref, kv_ref, o_ref):
    nq, nkv = q_ref.shape[1], kv_ref.shape[2]
    same = (lax.broadcasted_iota(jnp.int32, (nq, nkv), 0) % MEM_HEADS
            == lax.broadcasted_iota(jnp.int32, (nq, nkv), 1) % MEM_HEADS)
    for s in range(sb):
        kv = kv_ref[0, s].astype(BF16)
        sc = jnp.where(same, _dot_nt(q_ref[s], kv), NEG)
        p = jnp.exp(sc - jnp.max(sc, axis=-1, keepdims=True))
        p = p / jnp.sum(p, axis=-1, keepdims=True)
        o_ref[s] = _dot(p.astype(BF16), kv).astype(BF16)


def _mem_attn_rows(q3, cache4, layer, sb):
    b, nq, _ = q3.shape
    return pl.pallas_call(
        functools.partial(_mem_rows_kernel, sb), grid=(b // sb,),
        in_specs=[pl.BlockSpec((sb, nq, LANES), lambda i: (i, 0, 0)),
                  pl.BlockSpec((1, sb, cache4.shape[2], LANES), lambda i: (layer, i, 0, 0))],
        out_specs=pl.BlockSpec((sb, nq, LANES), lambda i: (i, 0, 0)),
        out_shape=jax.ShapeDtypeStruct((b, nq, LANES), BF16),
        compiler_params=_cparams("parallel"), name="mem_attn_sample",
    )(q3, cache4)


def _out_proj_kernel(x_ref, a_ref, b_ref, wa_ref, wb_ref, o_ref):
    o_ref[...] = x_ref[...] + _dot(a_ref[...], wa_ref[...]) + _dot(b_ref[...], wb_ref[...])


def _out_proj(x, a, b, wa, wb, tm):
    m, d = x.shape
    row = lambda width: pl.BlockSpec((tm, width), lambda i: (i, 0))
    return pl.pallas_call(
        _out_proj_kernel, grid=(m // tm,),
        in_specs=[row(d), row(a.shape[1]), row(b.shape[1]), _full(wa.shape), _full(wb.shape)],
        out_specs=row(d), out_shape=jax.ShapeDtypeStruct((m, d), F32),
        compiler_params=_cparams("parallel"), name="out_proj",
    )(x, a, b, wa, wb)


def _ffn_kernel(seq_tiles, t_s, d_ff, x_ref, g_ref, wu_ref, cw_ref, cb_ref, wd_ref, *rest):
    if seq_tiles:
        o_ref, st_ref, g_scr = rest
    else:
        p1_ref, p2_ref, o_ref, st_ref = rest
    x = x_ref[...]
    tm = x.shape[0]
    h = _rms_rows(x, g_ref[...]).astype(BF16)
    u = _dot(h, wu_ref[...])
    gate, val = u[:, :d_ff], u[:, d_ff:]
    if seq_tiles:
        @pl.when(pl.program_id(0) % seq_tiles == 0)
        def _():
            g_scr[0:8, :] = jnp.zeros((8, d_ff), F32)
        g_scr[8:, :] = gate
        g1 = g_scr[7:tm + 7, :]
        g2 = g_scr[6:tm + 6, :]
        tail = gate[tm - 8:, :]
        g_scr[0:8, :] = tail
        st_ref[...] = tail
    else:
        tpos = lax.broadcasted_iota(jnp.int32, (tm, 1), 0) % t_s
        g1 = jnp.where(tpos >= 1, pltpu.roll(gate, 1, 0), p1_ref[...])
        g2 = jnp.where(tpos >= 2, pltpu.roll(gate, 2, 0), p2_ref[...])
        st_ref[...] = gate
    cw = cw_ref[...]
    conv = cb_ref[...] + g2 * cw[0:1] + g1 * cw[1:2] + gate * cw[2:3]
    act = (conv * (1.0 / (1.0 + jnp.exp(-conv))) * val).astype(BF16)
    o_ref[...] = x + _dot(act, wd_ref[...])


def _ffn(x, seq_len, w, tm, prev=None):
    m, d = x.shape
    d_ff = w['w_down'].shape[0]
    row = lambda width: pl.BlockSpec((tm, width), lambda i: (i, 0))
    single = dict(pipeline_mode=pl.Buffered(1))
    wspecs = [_full(w['g_ffn'].shape),
              pl.BlockSpec(w['w_up'].shape, lambda i: (0, 0), **single),
              _full(w['conv_w'].shape), _full(w['conv_b'].shape),
              pl.BlockSpec(w['w_down'].shape, lambda i: (0, 0), **single)]
    wargs = [w['g_ffn'], w['w_up'], w['conv_w'], w['conv_b'], w['w_down']]
    if prev is None:
        seq_tiles = seq_len // tm
        return pl.pallas_call(
            functools.partial(_ffn_kernel, seq_tiles, 0, d_ff), grid=(m // tm,),
            in_specs=[row(d)] + wspecs,
            out_specs=[row(d), pl.BlockSpec((8, d_ff), lambda i: (i, 0))],
            out_shape=[jax.ShapeDtypeStruct((m, d), F32),
                       jax.ShapeDtypeStruct((m // tm * 8, d_ff), F32)],
            scratch_shapes=[pltpu.VMEM((tm + 8, d_ff), F32)],
            compiler_params=_cparams("arbitrary"), name="ffn_prompt",
        )(x, *wargs)
    p1, p2 = prev
    return pl.pallas_call(
        functools.partial(_ffn_kernel, 0, seq_len, d_ff), grid=(m // tm,),
        in_specs=[row(d)] + wspecs + [row(d_ff), row(d_ff)],
        out_specs=[row(d), row(d_ff)],
        out_shape=[jax.ShapeDtypeStruct((m, d), F32), jax.ShapeDtypeStruct((m, d_ff), F32)],
        compiler_params=_cparams("parallel"), name="ffn_sample",
    )(x, *wargs, p1, p2)


QW = N_GROUPS * DIL_HEADS * LANES
KVW = N_GROUPS * DIL_SLAB


def _proj_dil_kernel(prompt, x_ref, ga_ref, win_ref, sq_ref, sqt_ref, invq_ref, gq_ref,
                     sk_ref, skt_ref, invk_ref, gk_ref, vm_ref,
                     sm_ref, smt_ref, invm_ref, gm_ref, *rest):
    x = x_ref[0] if prompt else x_ref[...]
    h = _rms_rows(x, ga_ref[...]).astype(BF16)
    z = _dot(h, win_ref[...])
    q = z[:, :QW]
    qn = q * _seg_scale(q, sq_ref, sqt_ref, invq_ref) * gq_ref[...]
    kv = z[:, QW:QW + KVW]
    kvn = kv * (_seg_scale(kv, sk_ref, skt_ref, invk_ref) * gk_ref[...] + vm_ref[...])
    mq = z[:, QW + KVW:]
    mqn = (mq * _seg_scale(mq, sm_ref, smt_ref, invm_ref) * gm_ref[...]).astype(BF16)
    if not prompt:
        q_out, kv_out, mq_out = rest
        q_out[...] = qn.astype(BF16)
        kv_out[...] = kvn
        mq_out[...] = mqn
        return
    kv_out, mq_out = rest[0], rest[1]
    q_outs, kvb_outs, q_scr, kv_scr = rest[2:5], rest[5:8], rest[8], rest[9]
    kv_out[0] = kvn
    mq_out[0] = mqn
    for c in range(QW // LANES):
        q_scr[c] = qn[:, c * LANES:(c + 1) * LANES]
    for c in range(KVW // LANES):
        kv_scr[c] = kvn[:, c * LANES:(c + 1) * LANES]
    tm = q.shape[0]
    qc, kc = DIL_HEADS, DIL_SLAB // LANES
    for g, d in enumerate(DIL_RATES):
        for r in range(d):
            rows = pl.ds(r, tm // d, stride=d) if d > 1 else slice(None)
            for c in range(qc):
                q_outs[g][0, r, :, c * LANES:(c + 1) * LANES] = q_scr[g * qc + c, rows, :].astype(BF16)
            for c in range(kc):
                kvb_outs[g][0, r, :, c * LANES:(c + 1) * LANES] = kv_scr[g * kc + c, rows, :].astype(BF16)


def _proj_dil(x3, prompt, tm, w):
    n, t, d = x3.shape
    args = [x3, w['g_attn'], w['w_in'], w['Sq'], w['SqT'], w['invq'], w['gq'],
            w['Sk'], w['SkT'], w['invk'], w['gk'], w['vmask'], w['Sm'], w['SmT'], w['invm'], w['gm']]
    if not prompt:
        x2 = x3.reshape(n * t, d)
        m = n * t
        row = lambda width: pl.BlockSpec((tm, width), lambda i: (i, 0))
        return pl.pallas_call(
            functools.partial(_proj_dil_kernel, False), grid=(m // tm,),
            in_specs=[row(d)] + [_full(a.shape) for a in args[1:]],
            out_specs=[row(QW), row(KVW), row(MEM_W)],
            out_shape=[jax.ShapeDtypeStruct((m, QW), BF16), jax.ShapeDtypeStruct((m, KVW), F32),
                       jax.ShapeDtypeStruct((m, MEM_W), BF16)],
            compiler_params=_cparams("parallel"), name="proj_dil_sample",
        )(x2, *args[1:])
    row3 = lambda width: pl.BlockSpec((1, tm, width), lambda b, i: (b, i, 0))
    tail = min(max(DIL_WINDOWS), t)
    skip = (t - tail) // tm
    outs = [jax.ShapeDtypeStruct((n, tail, KVW), F32), jax.ShapeDtypeStruct((n, t, MEM_W), BF16)]
    ospecs = [pl.BlockSpec((1, tm, KVW), lambda b, i: (b, jnp.maximum(i - skip, 0), 0)), row3(MEM_W)]
    for width in (DIL_HEADS * LANES, DIL_SLAB):
        for dil in DIL_RATES:
            outs.append(jax.ShapeDtypeStruct((n, dil, t // dil, width), BF16))
            ospecs.append(pl.BlockSpec((1, dil, tm // dil, width), lambda b, i: (b, 0, i, 0)))
    return pl.pallas_call(
        functools.partial(_proj_dil_kernel, True), grid=(n, t // tm),
        in_specs=[row3(d)] + [_full(a.shape) for a in args[1:]],
        out_specs=ospecs, out_shape=outs,
        scratch_shapes=[pltpu.VMEM((QW // LANES, tm, LANES), F32), pltpu.VMEM((KVW // LANES, tm, LANES), F32)],
        compiler_params=_cparams("parallel", "arbitrary"), name="proj_dil_prompt",
    )(*args)


def _band_kernel(tq, nqb, q_ref, kp_ref, kc_ref, bias_ref, o_ref, ml_ref):
    i = pl.program_id(1)
    kcat = jnp.concatenate([kp_ref[0], kc_ref[0]], axis=0)
    first = jnp.logical_and(i == 0, lax.broadcasted_iota(jnp.int32, (tq, 2 * tq), 1) < tq)
    work = [(j, hd) for j in range(nqb) for hd in range(DIL_HEADS)]
    scores = [_dot_nt(q_ref[0, j * tq:(j + 1) * tq, hd * LANES:(hd + 1) * LANES],
                      kcat[j * tq:(j + 2) * tq, hd * DIL_ROW:hd * DIL_ROW + LANES]) for j, hd in work]
    stats = [[] for _ in range(nqb)]
    for (j, hd), s in zip(work, scores):
        s = s + bias_ref[hd]
        if j == 0:
            s = jnp.where(first, NEG, s)
        m = jnp.max(s, axis=-1, keepdims=True)
        p = jnp.exp(s - m)
        l = jnp.sum(p, axis=-1, keepdims=True)
        o_ref[0, j * tq:(j + 1) * tq, hd * DIL_ROW:(hd + 1) * DIL_ROW] = _dot(
            (p / l).astype(BF16), kcat[j * tq:(j + 2) * tq, hd * DIL_ROW:(hd + 1) * DIL_ROW]).astype(BF16)
        stats[j].append((m, l))
    for j in range(nqb):
        ml_ref[0, j * tq:(j + 1) * tq] = _cols([m for m, _ in stats[j]] + [l for _, l in stats[j]], LANES)


def _band_attn(q, kv, bias, tq, nqb):
    s, l, _ = q.shape
    tb = tq * nqb
    return pl.pallas_call(
        functools.partial(_band_kernel, tq, nqb), grid=(s, l // tb),
        in_specs=[pl.BlockSpec((1, tb, DIL_HEADS * LANES), lambda b, i: (b, i, 0)),
                  pl.BlockSpec((1, tq, DIL_SLAB), lambda b, i: (b, jnp.maximum(i * nqb - 1, 0), 0)),
                  pl.BlockSpec((1, tb, DIL_SLAB), lambda b, i: (b, i, 0)),
                  _full(bias.shape)],
        out_specs=[pl.BlockSpec((1, tb, DIL_SLAB), lambda b, i: (b, i, 0)),
                   pl.BlockSpec((1, tb, LANES), lambda b, i: (b, i, 0))],
        out_shape=[jax.ShapeDtypeStruct((s, l, DIL_SLAB), BF16),
                   jax.ShapeDtypeStruct((s, l, LANES), F32)],
        compiler_params=_cparams("parallel", "parallel"), name="dil_band_attn",
    )(q, kv, kv, bias)


def _merge_kernel(rates, *refs):
    ng = len(rates)
    o_refs, ml_refs, out_ref = refs[:ng], refs[ng:2 * ng], refs[2 * ng]
    scr = refs[2 * ng + 1:]
    nch = DIL_SLAB // LANES
    chunks, mls = [], []
    for g, d in enumerate(rates):
        if d == 1:
            chunks.append([o_refs[g][0, 0, :, c * LANES:(c + 1) * LANES].astype(F32) for c in range(nch)])
            mls.append(ml_refs[g][0, 0])
        else:
            o_scr, ml_scr = scr[2 * g], scr[2 * g + 1]
            tm = ml_scr.shape[0]
            for r in range(d):
                rows = pl.ds(r, tm // d, stride=d)
                for c in range(nch):
                    o_scr[c, rows, :] = o_refs[g][0, r, :, c * LANES:(c + 1) * LANES].astype(F32)
                ml_scr[rows, :] = ml_refs[g][0, r]
            chunks.append([o_scr[c] for c in range(nch)])
            mls.append(ml_scr[...])
    m_all = functools.reduce(jnp.maximum, [ml[:, :DIL_HEADS] for ml in mls])
    es = [ml[:, DIL_HEADS:2 * DIL_HEADS] * jnp.exp(ml[:, :DIL_HEADS] - m_all) for ml in mls]
    tot = functools.reduce(jnp.add, es)
    ws = [e / tot for e in es]
    for c in range(nch):
        hd = c // (DIL_ROW // LANES)
        acc = None
        for g in range(ng):
            term = chunks[g][c] * ws[g][:, hd:hd + 1]
            acc = term if acc is None else acc + term
        out_ref[0, :, c * LANES:(c + 1) * LANES] = acc.astype(BF16)


def _merge(os_, mls, rates, n, t, tm):
    specs = [pl.BlockSpec((1, d, tm // d, DIL_SLAB), lambda b, i: (b, 0, i, 0)) for d in rates] \
        + [pl.BlockSpec((1, d, tm // d, LANES), lambda b, i: (b, 0, i, 0)) for d in rates]
    scratch = []
    for d in rates:
        scratch += [pltpu.VMEM((DIL_SLAB // LANES, tm, LANES), F32), pltpu.VMEM((tm, LANES), F32)]
    return pl.pallas_call(
        functools.partial(_merge_kernel, tuple(rates)), grid=(n, t // tm),
        in_specs=specs,
        out_specs=pl.BlockSpec((1, tm, DIL_SLAB), lambda b, i: (b, i, 0)),
        out_shape=jax.ShapeDtypeStruct((n, t, DIL_SLAB), BF16),
        scratch_shapes=scratch,
        compiler_params=_cparams("parallel", "parallel"), name="dil_merge",
    )(*os_, *mls)


def _dil_sample_kernel(d, wb, t_s, q_ref, new_ref, buf_ref, bias_ref, o_ref, ml_ref, st_ref, *scr):
    nk = DIL_KEYS - 1
    new = new_ref[0]
    st_ref[0, 0:wb - t_s] = buf_ref[0, t_s:wb]
    st_ref[0, wb - t_s:wb] = new
    if d < t_s:
        src = scr[0]
        src[0:wb] = buf_ref[0]
        src[wb:wb + t_s] = new
    else:
        src = buf_ref.at[0]
    top = lax.broadcasted_iota(jnp.int32, (8, LANES), 0) < DIL_HEADS
    lane = lax.broadcasted_iota(jnp.int32, (8, LANES), 1)
    bias = bias_ref[...]
    nq, ntok = _dil_classes(d, t_s)

    def both_halves(col):
        x = jnp.where(top, jnp.broadcast_to(col, (8, LANES)), 0.0)
        return x + pltpu.roll(x, DIL_HEADS, 0)

    classes = range(t_s // nq)
    toks, scores, stats, outs = [], [], [], []
    for r in classes:
        if d < t_s:
            a3 = src[pl.ds(r, ntok, stride=d)] if d > 1 else src[pl.ds(r, ntok)]
        else:
            pad = jnp.zeros((ntok - nk - 1, 8, LANES), F32)
            a3 = jnp.concatenate([src[pl.ds(r, nk, stride=d)], new[r:r + 1], pad], axis=0)
        a = a3.reshape(ntok * 8, LANES).astype(BF16)
        qc = q_ref[0, pl.ds(r, nq, stride=d)] if nq > 1 else q_ref[0, r:r + 1]
        qc = qc.reshape(nq * 8, LANES).astype(BF16)
        toks.append(a)
        scores.append(_dot_nt(qc, a) + bias)
    for r in classes:
        s = scores[r]
        m = jnp.max(s, axis=-1, keepdims=True)
        p = jnp.exp(s - m)
        l = jnp.sum(p, axis=-1, keepdims=True)
        pn = p / l
        stats.append((m, l))
        p2 = jnp.concatenate([pn, pltpu.roll(pn, DIL_HEADS, 1)], axis=0).astype(BF16)
        outs.append(_dot(p2, toks[r]))
    for r in classes:
        o2, (m, l) = outs[r], stats[r]
        for k in range(nq):
            i = r + k * d
            lo, hi = o2[8 * k:8 * k + 8], o2[8 * (nq + k):8 * (nq + k) + 8]
            o_ref[0, i] = jnp.where(top, lo, pltpu.roll(hi, DIL_HEADS, 0))
            ml_ref[0, i] = jnp.where(lane == 0, both_halves(m[8 * k:8 * k + 8]),
                                     jnp.where(lane == 1, both_halves(l[8 * k:8 * k + 8]), 0.0))


def _dil_classes(d, t_s):
    nq = max(t_s // d, 1)
    ntok = DIL_KEYS - 1 + nq
    return nq, ntok + ntok % 2


def _dil_sample(qg, newg, buf, bias, d):
    b, t_s = qg.shape[:2]
    wb = buf.shape[1]
    blk = lambda r: pl.BlockSpec((1, r, 8, LANES), lambda s: (s, 0, 0, 0))
    tiles = jax.ShapeDtypeStruct((b, t_s, 8, LANES), F32)
    return pl.pallas_call(
        functools.partial(_dil_sample_kernel, d, wb, t_s), grid=(b,),
        in_specs=[blk(t_s), blk(t_s), blk(wb), _full(bias.shape)],
        out_specs=[blk(t_s), blk(t_s), blk(wb)],
        out_shape=[tiles, tiles, jax.ShapeDtypeStruct(buf.shape, F32)],
        scratch_shapes=[pltpu.VMEM((wb + t_s, 8, LANES), F32)] if d < t_s else [],
        compiler_params=_cparams("parallel"), name="dil_sample_attn",
    )(qg, newg, buf, bias)


def _merge_tiles_kernel(ng, *refs):
    o_refs, ml_refs, out_ref = refs[:ng], refs[ng:2 * ng], refs[2 * ng]
    ms = [r[...][:, :, 0:1] for r in ml_refs]
    ls = [r[...][:, :, 1:2] for r in ml_refs]
    m_all = functools.reduce(jnp.maximum, ms)
    es = [l * jnp.exp(m - m_all) for m, l in zip(ms, ls)]
    tot = functools.reduce(jnp.add, es)
    out_ref[...] = functools.reduce(jnp.add, [o[...] * (e / tot) for o, e in zip(o_refs, es)])


def _merge_tiles(os_, mls, tr):
    r = os_[0].shape[0]
    spec = pl.BlockSpec((tr, 8, LANES), lambda i: (i, 0, 0))
    return pl.pallas_call(
        functools.partial(_merge_tiles_kernel, len(os_)), grid=(r // tr,),
        in_specs=[spec] * (2 * len(os_)), out_specs=spec,
        out_shape=jax.ShapeDtypeStruct((r, 8, LANES), F32),
        compiler_params=_cparams("parallel"), name="dil_merge_sample",
    )(*os_, *mls)


def _seg_mats(width, segments):
    s = np.zeros((width, LANES), np.float32)
    inv = np.ones((1, LANES), np.float32)
    for k, (a, n) in enumerate(segments):
        s[a:a + n, k] = 1.0
        inv[0, k] = 1.0 / n
    return jnp.asarray(s, BF16), jnp.asarray(np.concatenate([s.T, s.T]), BF16), jnp.asarray(inv)


def _pad_heads(wm, heads, dim, slab, offset=0):
    r = wm.shape[0]
    out = jnp.pad(wm.reshape(r, heads, dim), ((0, 0), (0, 0), (offset, slab - dim - offset)))
    return out.reshape(r, heads * slab)


def _mem_q_tables(g_qn_mem):
    seg = [(hd * LANES, MEM_DIM) for hd in range(MEM_HEADS)]
    sm, smt, invm = _seg_mats(MEM_W, seg)
    gm = _pad_heads(jnp.tile(g_qn_mem, MEM_HEADS)[None, :], MEM_HEADS, MEM_DIM, LANES) * MEM_SCALE
    return dict(Sm=sm, SmT=smt, invm=invm, gm=gm)


def _rope_cs(pos):
    inv = ROPE_THETA ** (-jnp.arange(0, ROPE_DIM, 2, dtype=F32) / ROPE_DIM)
    ang = pos.astype(F32)[:, None] * inv[None, :]
    return jnp.cos(ang), jnp.sin(ang)


def _mla_tables(pos, g_qn_nope, g_qn_pe, g_kn_pe):
    cos, sin = _rope_cs(pos)
    t = pos.shape[0]
    hr = ROPE_DIM // 2
    ones = jnp.ones((t, 1), F32)
    cslab = jnp.concatenate([ones * g_qn_nope[None, :], cos * g_qn_pe[None, :hr], cos * g_qn_pe[None, hr:],
                             jnp.zeros((t, LANES - NOPE_DIM - ROPE_DIM), F32)], axis=1) * MLA_SCALE
    sslab = jnp.concatenate([jnp.zeros((t, NOPE_DIM), F32), -sin * g_qn_pe[None, hr:], sin * g_qn_pe[None, :hr],
                             jnp.zeros((t, LANES - NOPE_DIM - ROPE_DIM), F32)], axis=1) * MLA_SCALE
    tk = jnp.concatenate([cos * g_kn_pe[None, :hr], cos * g_kn_pe[None, hr:],
                          -sin * g_kn_pe[None, hr:], sin * g_kn_pe[None, :hr],
                          jnp.zeros((t, LANES - 2 * ROPE_DIM), F32)], axis=1)
    return cslab, sslab, tk


def _prep_mla(i, j, g_attn, w_o, w_in_a, g_q_a, w_q_b, g_kv_a, w_uk, w_uv, g_qn_mem):
    d = w_in_a.shape[1]
    hr = ROPE_DIM // 2
    wi = w_in_a[j]
    o1, o2, o3 = Q_LORA, Q_LORA + KV_LORA, Q_LORA + KV_LORA + ROPE_DIM
    kpe = wi[:, o2:o3]
    w_in = jnp.concatenate([wi[:, :o2], _pad_heads(wi[:, o3:], MEM_HEADS, MEM_DIM, LANES),
                            kpe, jnp.concatenate([kpe[:, hr:], kpe[:, :hr]], axis=1),
                            jnp.zeros((d, LANES - 2 * ROPE_DIM), F32)], axis=1).astype(BF16)
    hq = NOPE_DIM + ROPE_DIM
    wq3 = w_q_b[j].reshape(Q_LORA, MLA_HEADS, hq)
    wq = _pad_heads(w_q_b[j], MLA_HEADS, hq, LANES).astype(BF16)
    sw = jnp.concatenate([wq3[:, :, NOPE_DIM + hr:], wq3[:, :, NOPE_DIM:NOPE_DIM + hr]], axis=2)
    wq_sw = _pad_heads(sw.reshape(Q_LORA, MLA_HEADS * ROPE_DIM), MLA_HEADS, ROPE_DIM, LANES, NOPE_DIM).astype(BF16)
    seg = []
    for hd in range(MLA_HEADS):
        seg += [(hd * LANES, NOPE_DIM), (hd * LANES + NOPE_DIM, ROPE_DIM)]
    s, st, inv = _seg_mats(MLA_HEADS * LANES, seg)
    wk_c = _pad_heads(w_uk[j].reshape(KV_LORA, MLA_HEADS * NOPE_DIM), MLA_HEADS, NOPE_DIM, LANES).astype(BF16)
    eye = np.zeros((LANES, MLA_HEADS, LANES), np.float32)
    for r in range(ROPE_DIM):
        eye[r, :, NOPE_DIM + r] = 1.0
    wk_p = jnp.asarray(eye.reshape(LANES, MLA_HEADS * LANES), BF16)
    wv4 = w_uv[j].reshape(KV_LORA, MLA_HEADS // 2, 2, MLA_V_DIM)
    lo = ((0, 0), (0, 0), (0, LANES - MLA_V_DIM))
    hi = ((0, 0), (0, 0), (LANES - MLA_V_DIM, 0))
    wv = jnp.stack([jnp.pad(wv4[:, :, 0], lo), jnp.pad(wv4[:, :, 1], hi)], axis=2)
    wv = wv.reshape(KV_LORA, MLA_HEADS * LANES).astype(BF16)
    vone = np.zeros((1, MLA_HEADS, LANES), np.float32)
    vone[0, 0::2, MLA_V_DIM] = 1.0
    vone[0, 1::2, 0] = 1.0
    vone = jnp.asarray(vone.reshape(1, MLA_HEADS * LANES))
    pe_rows = np.zeros((ROPE_DIM, MLA_QCAT), np.float32)
    pe_rows[np.arange(ROPE_DIM), KV_LORA + np.arange(ROPE_DIM)] = 1.0
    wcat = jnp.concatenate([
        jnp.pad(jnp.transpose(w_uk[j], (1, 2, 0)), ((0, 0), (0, 0), (0, MLA_QCAT - KV_LORA))),
        jnp.broadcast_to(jnp.asarray(pe_rows)[None], (MLA_HEADS, ROPE_DIM, MLA_QCAT)),
        jnp.zeros((MLA_HEADS, LANES - hq, MLA_QCAT), F32)], axis=1)
    wvp = jnp.transpose(wv4, (1, 2, 0, 3))
    uvp = jnp.concatenate([jnp.pad(wvp[:, 0], lo), jnp.pad(wvp[:, 1], hi)], axis=1)
    ntok = MLA_HEADS * MLA_V_DIM
    w = dict(g_attn=g_attn[i][None], w_in=w_in, g_q_a=g_q_a[j][None], wq=wq, wq_sw=wq_sw,
             g_kv_a=g_kv_a[j][None], S=s, ST=st, inv=inv, wk_c=wk_c, wk_p=wk_p, wv=wv, vone=vone,
             wcat=wcat.astype(BF16), uv_pairs=uvp.astype(BF16),
             wo_tok=w_o[i][:ntok].astype(BF16),
             wo_mem=_pad_rows(w_o[i][ntok:], MEM_HEADS, MEM_DIM, LANES, MEM_DIM).astype(BF16))
    w.update(_mem_q_tables(g_qn_mem[i]))
    return w


def _pad_rows(wm, heads, dim, slab, offset):
    c = wm.shape[1]
    out = jnp.pad(wm.reshape(heads, dim, c), ((0, 0), (offset, slab - dim - offset), (0, 0)))
    return out.reshape(heads * slab, c)


def _prep_dil(i, j, g_attn, w_o, w_in_b, g_qn_b, g_kn_b, g_qn_mem):
    wi = w_in_b[j]
    gh = N_GROUPS * DIL_HEADS
    nqk = gh * DIL_QK
    wq = _pad_heads(wi[:, :nqk], gh, DIL_QK, LANES)
    wk = wi[:, nqk:2 * nqk].reshape(-1, gh, DIL_QK)
    wv = wi[:, 2 * nqk:2 * nqk + gh * DIL_V].reshape(-1, gh, DIL_V)
    wkv = jnp.concatenate([wk, wv], axis=2).reshape(-1, gh * DIL_ROW)
    wmq = _pad_heads(wi[:, 2 * nqk + gh * DIL_V:], MEM_HEADS, MEM_DIM, LANES)
    w_in = jnp.concatenate([wq, wkv, wmq], axis=1).astype(BF16)
    sq, sqt, invq = _seg_mats(QW, [(k * LANES, DIL_QK) for k in range(gh)])
    sk, skt, invk = _seg_mats(KVW, [(k * DIL_ROW, DIL_QK) for k in range(gh)])
    gq = _pad_heads(jnp.repeat(g_qn_b[j], DIL_HEADS, axis=0).reshape(1, nqk), gh, DIL_QK, LANES) * DIL_SCALE
    gk = _pad_heads(jnp.repeat(g_kn_b[j], DIL_HEADS, axis=0).reshape(1, nqk), gh, DIL_QK, DIL_ROW)
    vmask = _pad_heads(jnp.ones((1, gh * DIL_V), F32), gh, DIL_V, DIL_ROW, DIL_QK)
    ntok = DIL_HEADS * DIL_V
    w = dict(g_attn=g_attn[i][None], w_in=w_in, Sq=sq, SqT=sqt, invq=invq, gq=gq,
             Sk=sk, SkT=skt, invk=invk, gk=gk, vmask=vmask,
             wo_tok=_pad_rows(w_o[i][:ntok], DIL_HEADS, DIL_V, DIL_ROW, DIL_QK).astype(BF16),
             wo_mem=_pad_rows(w_o[i][ntok:], MEM_HEADS, MEM_DIM, LANES, MEM_DIM).astype(BF16))
    w.update(_mem_q_tables(g_qn_mem[i]))
    return w


def _prep_mem(i, g_mem, w_mem_kv, g_kn_mem):
    sk, skt, invk = _seg_mats(MEM_W, [(hd * LANES, MEM_DIM) for hd in range(MEM_HEADS)])
    gk = _pad_heads(jnp.tile(g_kn_mem[i], MEM_HEADS)[None, :], MEM_HEADS, MEM_DIM, LANES)
    vmask = _pad_heads(jnp.ones((1, MEM_HEADS * MEM_DIM), F32), MEM_HEADS, MEM_DIM, LANES, MEM_DIM)
    return dict(g_mem=g_mem[i][None], w_mem=w_mem_kv[i].astype(BF16), Sk=sk, SkT=skt, invk=invk, gk=gk, vmask=vmask)


def _prep_ffn(i, g_ffn, w_up, conv_w, conv_b, w_down):
    return dict(g_ffn=g_ffn[i][None], w_up=w_up[i].astype(BF16), conv_w=conv_w[i],
                conv_b=conv_b[i][None], w_down=w_down[i].astype(BF16))


def _rel_bucket(dist):
    max_exact = N_BUCKETS // 2
    dd = jnp.maximum(dist.astype(F32), 1.0)
    large = max_exact + (jnp.log(dd / max_exact) / math.log(MAX_DISTANCE / max_exact)
                         * (N_BUCKETS - max_exact)).astype(jnp.int32)
    large = jnp.minimum(large, N_BUCKETS - 1)
    return jnp.where(dist < max_exact, dist, large)


def _dil_bias_tables(rel_bias, tq, t_s):
    band, samp = [], []

    def toeplitz(rev, rows, cols):
        period = cols + rows + 1
        f = jnp.concatenate([rev, jnp.full((DIL_HEADS, period - DIL_KEYS), NEG, F32)], axis=1)
        return jnp.tile(f, (1, rows))[:, :rows * (period - 1)].reshape(DIL_HEADS, rows, period - 1)[:, :, :cols]

    srow = jnp.arange(8)[None, None, None, :] == jnp.arange(DIL_HEADS)[:, None, None, None]
    for g in range(N_GROUPS):
        bk = _rel_bucket(DIL_RATES[g] * jnp.arange(DIL_KEYS, dtype=jnp.int32))
        b = rel_bias[bk][:, g * DIL_HEADS:(g + 1) * DIL_HEADS].T.astype(F32)
        rev = b[:, ::-1]
        band.append(toeplitz(rev, tq, 2 * tq))
        nq, ntok = _dil_classes(DIL_RATES[g], t_s)
        tab = jnp.where(srow, toeplitz(rev, nq, ntok)[..., None], NEG).reshape(DIL_HEADS, nq, ntok * 8)
        tab = jnp.pad(tab.transpose(1, 0, 2), ((0, 0), (0, 8 - DIL_HEADS), (0, 0)), constant_values=NEG)
        samp.append(tab.reshape(nq * 8, ntok * 8))
    return band, samp


def kernel(x_prompt, x_sample, cache_mla, state_win0, state_win1, state_win2, cache_mem, state_conv, page_table, mem_prompt, rel_bias, g_attn, w_o, w_in_a, g_q_a, w_q_b, g_kv_a, w_uk, w_uv, g_qn_nope, g_qn_pe, g_kn_pe, w_in_b, g_qn_b, g_kn_b, g_mem, w_mem_kv, g_qn_mem, g_kn_mem, g_ffn, w_up, conv_w, conv_b, w_down):
    state_wins = (state_win0, state_win1, state_win2)
    n, t_p, d = x_prompt.shape
    b, t_s, _ = x_sample.shape
    n_mem = mem_prompt.shape[1]
    d_ff = w_down.shape[1]
    past = page_table.shape[1] * cache_mla.shape[2]
    mp, ms = n * t_p, b * t_s
    tm_p = min(256, t_p)
    tm_s = min(256, ms)
    tq_band = DIL_KEYS - 1
    assert t_s >= 2 and t_p % tm_p == 0 and ms % tm_s == 0 and tm_s % t_s == 0
    assert all(sw.shape[2] == wdw and wdw == (DIL_KEYS - 1) * r
               for sw, wdw, r in zip(state_wins, DIL_WINDOWS, DIL_RATES))

    xp = x_prompt.reshape(mp, d)
    xs = x_sample.reshape(ms, d)
    mem2 = mem_prompt.reshape(n * n_mem, d)
    pos_p = jnp.tile(jnp.arange(t_p), n)
    pos_s = jnp.tile(past + jnp.arange(t_s), b)
    band_bias, samp_bias = _dil_bias_tables(rel_bias, tq_band, t_s)
    cache_rows = cache_mem.reshape(cache_mem.shape[0], b, n_mem * MEM_HEADS, 2 * MEM_DIM)

    mem_out, conv_p_out, conv_s_out = [], [], []
    for i in range(2):
        j = i // 2
        if i % 2 == 0:
            w = _prep_mla(i, j, g_attn, w_o, w_in_a, g_q_a, w_q_b, g_kv_a, w_uk, w_uv, g_qn_mem)
            cq, sq, tk = _mla_tables(pos_p, g_qn_nope[j], g_qn_pe[j], g_kn_pe[j])
            q_p, rows_p, mq_p, k_p, v_p = _proj_mla(xp, True, tm_p, w, cq, sq, tk)
            cq, sq, tk = _mla_tables(pos_s, g_qn_nope[j], g_qn_pe[j], g_kn_pe[j])
            qc_s, rows_s, mq_s = _proj_mla(xs, False, tm_s, w, cq, sq, tk)
            hw = MLA_HEADS * LANES
            tok_p = _mla_flash(q_p.reshape(n, t_p, hw), k_p.reshape(n, t_p, hw), v_p.reshape(n, t_p, hw),
                               min(512, t_p)).reshape(mp, -1)
            q3 = qc_s.reshape(b, t_s, MLA_HEADS, MLA_QCAT).transpose(0, 2, 1, 3).reshape(b, MLA_HEADS * t_s, MLA_QCAT)
            ctx = _mla_sample_attn(q3, rows_s.reshape(b, t_s, MLA_ROW), cache_mla[j].transpose(0, 2, 1),
                                   page_table, min(MLA_PAGES_PER_STEP, page_table.shape[1]))
            ctx = ctx.reshape(b, MLA_HEADS, t_s, KV_LORA).transpose(0, 2, 1, 3).reshape(ms, MLA_HEADS * KV_LORA)
            tok_s = _mla_uv(ctx, w['uv_pairs'], tm_s)
            mla_rows_p = rows_p.reshape(1, n, t_p, MLA_ROW)
            mla_rows_s = rows_s.reshape(1, b, t_s, MLA_ROW)
        else:
            w = _prep_dil(i, j, g_attn, w_o, w_in_b, g_qn_b, g_kn_b, g_qn_mem)
            res = _proj_dil(xp.reshape(n, t_p, d), True, tm_p, w)
            kv_p, mq_p = res[0], res[1].reshape(mp, MEM_W)
            os_, mls = [], []
            for g, dil in enumerate(DIL_RATES):
                l = t_p // dil
                o, ml = _band_attn(res[2 + g].reshape(n * dil, l, DIL_HEADS * LANES),
                                   res[5 + g].reshape(n * dil, l, DIL_SLAB), band_bias[g], tq_band,
                                   2 if l % (2 * tq_band) == 0 else 1)
                os_.append(o.reshape(n, dil, l, DIL_SLAB))
                mls.append(ml.reshape(n, dil, l, LANES))
            tok_p = _merge(os_, mls, DIL_RATES, n, t_p, tm_p).reshape(mp, DIL_SLAB)
            win_p = [kv_p[:, kv_p.shape[1] - min(wdw, t_p):, g * DIL_SLAB:(g + 1) * DIL_SLAB]
                     .reshape(1, n, min(wdw, t_p), DIL_HEADS, DIL_ROW) for g, wdw in enumerate(DIL_WINDOWS)]
            q_s, kv_s, mq_s = _proj_dil(xs.reshape(b, t_s, d), False, tm_s, w)
            nc = DIL_ROW // LANES
            q_t = jnp.pad(q_s.astype(F32).reshape(b, t_s, N_GROUPS, DIL_HEADS, LANES),
                          ((0, 0), (0, 0), (0, 0), (0, 8 - DIL_HEADS), (0, 0))).transpose(2, 0, 1, 3, 4)
            kv_t = kv_s.reshape(b, t_s, N_GROUPS, DIL_HEADS, nc, LANES).transpose(2, 0, 1, 4, 3, 5)
            kv_t = kv_t.reshape(N_GROUPS, b, t_s, 8, LANES)
            os_, mls, win_s = [], [], []
            for g, dil in enumerate(DIL_RATES):
                buf = state_wins[g][j]
                wb = buf.shape[1]
                buf_t = buf.reshape(b, wb, DIL_HEADS, nc, LANES).transpose(0, 1, 3, 2, 4).reshape(b, wb, 8, LANES)
                o, ml, st = _dil_sample(q_t[g], kv_t[g], buf_t, samp_bias[g], dil)
                os_.append(o.reshape(ms, 8, LANES))
                mls.append(ml.reshape(ms, 8, LANES))
                st = st.reshape(b, wb, nc, DIL_HEADS, LANES).transpose(0, 1, 3, 2, 4)
                win_s.append(st.reshape(1, b, wb, DIL_HEADS, DIL_ROW))
            tok_s = _merge_tiles(os_, mls, tm_s).reshape(ms, nc, DIL_HEADS, LANES).transpose(0, 2, 1, 3)
            tok_s = tok_s.reshape(ms, DIL_SLAB).astype(BF16)

        wm = _prep_mem(i, g_mem, w_mem_kv, g_kn_mem)
        mkv = _mem_kv(mem2, wm, min(256, n * n_mem))
        mem_out.append(mkv.reshape(n, n_mem, MEM_HEADS, 2 * MEM_DIM))
        mo_p = _mem_attn(mq_p.reshape(n, t_p, MEM_W), mkv.reshape(n, n_mem, MEM_W), min(512, t_p)).reshape(mp, MEM_W)
        mo_s = _mem_attn_rows(mq_s.reshape(b, t_s * MEM_HEADS, LANES), cache_rows, i, min(8, b)).reshape(ms, MEM_W)
        xp = _out_proj(xp, tok_p, mo_p, w['wo_tok'], w['wo_mem'], tm_p)
        xs = _out_proj(xs, tok_s, mo_s, w['wo_tok'], w['wo_mem'], tm_s)

        wf = _prep_ffn(i, g_ffn, w_up, conv_w, conv_b, w_down)
        xp, tails = _ffn(xp, t_p, wf, tm_p)
        conv_p_out.append(tails.reshape(n, t_p // tm_p, 8, d_ff)[:, -1, 6:, :])
        prev = state_conv[i]
        zeros = jnp.zeros((b, t_s - 2, d_ff), F32)
        p2 = jnp.concatenate([prev, zeros], axis=1).reshape(ms, d_ff)
        p1 = jnp.concatenate([prev[:, 1:], zeros, jnp.zeros((b, 1, d_ff), F32)], axis=1).reshape(ms, d_ff)
        xs, gates = _ffn(xs, t_s, wf, tm_s, prev=(p1, p2))
        conv_s_out.append(gates.reshape(b, t_s, d_ff)[:, t_s - 2:, :])

    return (xp.reshape(n, t_p, d), xs.reshape(b, t_s, d), mla_rows_p, mla_rows_s,
            win_p[0], win_p[1], win_p[2], win_s[0], win_s[1], win_s[2],
            jnp.stack(mem_out), jnp.stack(conv_p_out), jnp.stack(conv_s_out))
```

```python
import functools
import math

import jax
import jax.numpy as jnp
import numpy as np
from jax import lax
from jax.experimental import pallas as pl
from jax.experimental.pallas import tpu as pltpu

F32 = jnp.float32
BF16 = jnp.bfloat16

EPS = 1e-6
LANES = 128
VMEM_LIMIT = 48 * 1024 * 1024

MLA_HEADS = 12
NOPE_DIM = 64
ROPE_DIM = 32
MLA_V_DIM = 64
Q_LORA = 256
KV_LORA = 256
MLA_ROW = KV_LORA + ROPE_DIM
ROPE_THETA = 10000.0
MLA_SCALE = 1.0 / math.sqrt(NOPE_DIM + ROPE_DIM)
MLA_QCAT = 384
MLA_PAGES_PER_STEP = 64
MLA_PAGE_CHUNK = 16
FLASH_HEAD_GROUP = 4

DIL_WINDOWS = (128, 512, 2048)
DIL_RATES = (1, 4, 16)
N_GROUPS = 3
DIL_HEADS = 4
DIL_QK = 64
DIL_V = 192
DIL_KEYS = DIL_WINDOWS[0] // DIL_RATES[0] + 1
DIL_SCALE = 1.0 / math.sqrt(DIL_QK)
DIL_ROW = DIL_QK + DIL_V
DIL_SLAB = DIL_HEADS * DIL_ROW

MEM_HEADS = 4
MEM_DIM = 64
MEM_SCALE = 1.0 / math.sqrt(MEM_DIM)
MEM_W = MEM_HEADS * LANES

N_BUCKETS = 32
MAX_DISTANCE = 2048
NEG = -1e30


def _cparams(*sem):
    return pltpu.CompilerParams(dimension_semantics=sem, vmem_limit_bytes=VMEM_LIMIT)


def _dot(a, b):
    return jnp.dot(a, b, preferred_element_type=F32)


def _dot_nt(a, b):
    return lax.dot_general(a, b, (((1,), (1,)), ((), ())), preferred_element_type=F32)


def _rms_rows(x, g):
    return x * lax.rsqrt(jnp.mean(x * x, axis=-1, keepdims=True) + EPS) * g


def _seg_scale(x, s_ref, st2_ref, inv_ref):
    ss = _dot((x * x).astype(BF16), s_ref[...])
    r = lax.rsqrt(ss * inv_ref[...] + EPS)
    hi = r.astype(BF16)
    lo = (r - hi.astype(F32)).astype(BF16)
    return _dot(jnp.concatenate([hi, lo], axis=1), st2_ref[...])


def _cols(vals, width):
    rows = vals[0].shape[0]
    lane = lax.broadcasted_iota(jnp.int32, (rows, width), 1)
    out = jnp.zeros((rows, width), F32)
    for k, v in enumerate(vals):
        out = jnp.where(lane == k, v, out)
    return out


def _full(shape):
    return pl.BlockSpec(shape, lambda *_: (0,) * len(shape))


def _proj_mla_kernel(prompt, x_ref, ga_ref, win_ref, gqa_ref, wq_ref, wqs_ref, gkv_ref,
                     cq_ref, sq_ref, tk_ref, s_ref, st_ref, inv_ref,
                     sm_ref, smt_ref, invm_ref, gm_ref, *rest):
    if prompt:
        wkc_ref, wkp_ref, wv_ref, vone_ref, q_out, rows_out, mq_out, k_out, v_out = rest
    else:
        wcat_ref, q_out, rows_out, mq_out = rest
    h = _rms_rows(x_ref[...], ga_ref[...]).astype(BF16)
    z = _dot(h, win_ref[...])
    qa = _rms_rows(z[:, :Q_LORA], gqa_ref[...]).astype(BF16)
    q = _dot(qa, wq_ref[...])
    qs = _dot(qa, wqs_ref[...])
    scale = _seg_scale(q, s_ref, st_ref, inv_ref)
    cq = jnp.concatenate([cq_ref[...]] * MLA_HEADS, axis=1)
    sq = jnp.concatenate([sq_ref[...]] * MLA_HEADS, axis=1)
    qf = ((q * cq + qs * sq) * scale).astype(BF16)
    c = _rms_rows(z[:, Q_LORA:Q_LORA + KV_LORA], gkv_ref[...])
    zk = z[:, 1024:1152]
    kp = zk[:, :ROPE_DIM]
    rk = lax.rsqrt(jnp.mean(kp * kp, axis=-1, keepdims=True) + EPS)
    t = zk * tk_ref[...] * rk
    kr = t + pltpu.roll(t, LANES - ROPE_DIM, 1)
    rows_out[:, :KV_LORA] = c
    rows_out[:, KV_LORA:] = kr[:, :ROPE_DIM]
    mq = z[:, 512:1024]
    mscale = _seg_scale(mq, sm_ref, smt_ref, invm_ref)
    mq_out[...] = (mq * mscale * gm_ref[...]).astype(BF16)
    if prompt:
        q_out[...] = qf
        cb = c.astype(BF16)
        k_out[...] = (_dot(cb, wkc_ref[...]) + _dot(kr.astype(BF16), wkp_ref[...])).astype(BF16)
        v_out[...] = (_dot(cb, wv_ref[...]) + vone_ref[...]).astype(BF16)
    else:
        for hd in range(MLA_HEADS):
            q_out[:, hd * MLA_QCAT:(hd + 1) * MLA_QCAT] = _dot(
                qf[:, hd * LANES:(hd + 1) * LANES], wcat_ref[hd]).astype(BF16)


def _proj_mla(x, prompt, tm, w, cq, sq, tk):
    m, d = x.shape
    hw = MLA_HEADS * LANES
    row = lambda width: pl.BlockSpec((tm, width), lambda i: (i, 0))
    args = [x, w['g_attn'], w['w_in'], w['g_q_a'], w['wq'], w['wq_sw'], w['g_kv_a'],
            cq, sq, tk, w['S'], w['ST'], w['inv'], w['Sm'], w['SmT'], w['invm'], w['gm']]
    ptiles = cq.shape[0] // tm
    pos = pl.BlockSpec((tm, LANES), lambda i: (i % ptiles, 0))
    specs = [row(d)] + [_full(a.shape) for a in args[1:7]] + [pos, pos, pos] \
        + [_full(a.shape) for a in args[10:]]
    if prompt:
        extra = [w['wk_c'], w['wk_p'], w['wv'], w['vone']]
        outs = [jax.ShapeDtypeStruct((m, hw), BF16), jax.ShapeDtypeStruct((m, MLA_ROW), F32),
                jax.ShapeDtypeStruct((m, MEM_W), BF16), jax.ShapeDtypeStruct((m, hw), BF16),
                jax.ShapeDtypeStruct((m, hw), BF16)]
        ospecs = [row(hw), row(MLA_ROW), row(MEM_W), row(hw), row(hw)]
    else:
        extra = [w['wcat']]
        outs = [jax.ShapeDtypeStruct((m, MLA_HEADS * MLA_QCAT), BF16),
                jax.ShapeDtypeStruct((m, MLA_ROW), F32), jax.ShapeDtypeStruct((m, MEM_W), BF16)]
        ospecs = [row(MLA_HEADS * MLA_QCAT), row(MLA_ROW), row(MEM_W)]
    return pl.pallas_call(
        functools.partial(_proj_mla_kernel, prompt),
        grid=(m // tm,),
        in_specs=specs + [_full(a.shape) for a in extra],
        out_specs=ospecs, out_shape=outs,
        compiler_params=_cparams("parallel"),
        name="proj_mla_prompt" if prompt else "proj_mla_sample",
    )(*args, *extra)


def _flash_kernel(tq, qi_ref, kj_ref, q_ref, k_ref, v_ref, o_ref, m_scr, acc_scr):
    i, j = qi_ref[pl.program_id(1)], kj_ref[pl.program_id(1)]

    @pl.when(j == 0)
    def _():
        m_scr[...] = jnp.full(m_scr.shape, NEG, F32)
        acc_scr[...] = jnp.zeros(acc_scr.shape, F32)

    def step(masked):
        if masked:
            keep = lax.broadcasted_iota(jnp.int32, (tq, tq), 1) <= lax.broadcasted_iota(jnp.int32, (tq, tq), 0)
        for h0 in range(0, MLA_HEADS, FLASH_HEAD_GROUP):
            heads = range(h0, h0 + FLASH_HEAD_GROUP)
            scores = [_dot_nt(q_ref[0, :, hd * LANES:(hd + 1) * LANES], k_ref[0, :, hd * LANES:(hd + 1) * LANES])
                      for hd in heads]
            for hd, s in zip(heads, scores):
                if masked:
                    s = jnp.where(keep, s, NEG)
                m_old = m_scr[hd]
                m_new = jnp.maximum(m_old, jnp.max(s, axis=-1, keepdims=True))
                alpha = jnp.exp(m_old - m_new)
                p = jnp.exp(s - jnp.concatenate([m_new] * (tq // LANES), axis=1))
                acc_scr[hd] = alpha * acc_scr[hd] + _dot(p.astype(BF16), v_ref[0, :, hd * LANES:(hd + 1) * LANES])
                m_scr[hd] = m_new

    @pl.when(j < i)
    def _():
        step(False)

    @pl.when(j == i)
    def _():
        step(True)
        low = lax.broadcasted_iota(jnp.int32, (tq, LANES), 1) < MLA_V_DIM
        for pr in range(MLA_HEADS // 2):
            ae, ao = acc_scr[2 * pr], acc_scr[2 * pr + 1]
            o_ref[0, :, pr * LANES:(pr + 1) * LANES] = jnp.where(
                low, ae / ae[:, MLA_V_DIM:MLA_V_DIM + 1], ao / ao[:, 0:1]).astype(BF16)


def _mla_flash(q, k, v, tq):
    n, t, hw = q.shape
    nq = t // tq
    pairs = [(i, j) for i in range(nq) for j in range(i + 1)]
    qi = jnp.asarray([p[0] for p in pairs], jnp.int32)
    kj = jnp.asarray([p[1] for p in pairs], jnp.int32)
    return pl.pallas_call(
        functools.partial(_flash_kernel, tq),
        grid_spec=pltpu.PrefetchScalarGridSpec(
            num_scalar_prefetch=2,
            grid=(n, len(pairs)),
            in_specs=[pl.BlockSpec((1, tq, hw), lambda b, s, qi, kj: (b, qi[s], 0)),
                      pl.BlockSpec((1, tq, hw), lambda b, s, qi, kj: (b, kj[s], 0)),
                      pl.BlockSpec((1, tq, hw), lambda b, s, qi, kj: (b, kj[s], 0))],
            out_specs=pl.BlockSpec((1, tq, MLA_HEADS * MLA_V_DIM), lambda b, s, qi, kj: (b, qi[s], 0)),
            scratch_shapes=[pltpu.VMEM((MLA_HEADS, tq, LANES), F32), pltpu.VMEM((MLA_HEADS, tq, LANES), F32)]),
        out_shape=jax.ShapeDtypeStruct((n, t, MLA_HEADS * MLA_V_DIM), BF16),
        compiler_params=_cparams("parallel", "arbitrary"),
        name="mla_flash_prompt",
    )(qi, kj, q, k, v)


def _mla_sample_kernel(npg, t_s, pt_ref, q_ref, new_ref, *rest):
    pages, (o_ref, m_scr, l_scr, acc_scr) = rest[:npg], rest[npg:]
    c = pl.program_id(1)

    @pl.when(c == 0)
    def _():
        m_scr[...] = jnp.full(m_scr.shape, NEG, F32)
        l_scr[...] = jnp.zeros(l_scr.shape, F32)
        acc_scr[...] = jnp.zeros(acc_scr.shape, F32)

    q = q_ref[0][:, :MLA_ROW]

    def update(state, s, pv):
        m_old, l_old, acc = state
        m_new = jnp.maximum(m_old, jnp.max(s, axis=-1, keepdims=True))
        alpha = jnp.exp(m_old - m_new)
        p = jnp.exp(s - m_new[:, 0:1])
        l_new = alpha * l_old + jnp.sum(p, axis=-1, keepdims=True)
        acc = jnp.concatenate([alpha] * (KV_LORA // LANES), axis=1) * acc + pv(p.astype(BF16))
        return m_new, l_new, acc

    def load(c0):
        return jnp.concatenate([pg[0].astype(BF16) for pg in pages[c0:c0 + MLA_PAGE_CHUNK]], axis=1)

    starts = list(range(0, npg, MLA_PAGE_CHUNK))
    state = (m_scr[...], l_scr[...], acc_scr[...])
    kts = {0: load(starts[0])}
    scores = {0: _dot(q, kts[0])}
    if len(starts) > 1:
        kts[1] = load(starts[1])
    for ch in range(len(starts)):
        if ch + 1 < len(starts):
            scores[ch + 1] = _dot(q, kts[ch + 1])
        if ch + 2 < len(starts):
            kts[ch + 2] = load(starts[ch + 2])
        state = update(state, scores.pop(ch), lambda p, kt=kts.pop(ch): _dot_nt(p, kt[:KV_LORA]))
    m_scr[...], l_scr[...], acc_scr[...] = state

    @pl.when(c == pl.num_programs(1) - 1)
    def _():
        nb = new_ref[0].astype(BF16)
        s = _dot_nt(q, nb)
        rows = q.shape[0]
        tq = lax.broadcasted_iota(jnp.int32, (rows, t_s), 0) % t_s
        s = jnp.where(lax.broadcasted_iota(jnp.int32, (rows, t_s), 1) <= tq, s, NEG)
        _, l_fin, acc = update((m_scr[...], l_scr[...], acc_scr[...]), s, lambda p: _dot(p, nb[:, :KV_LORA]))
        o_ref[0] = (acc / jnp.concatenate([l_fin] * (KV_LORA // LANES), axis=1)).astype(BF16)


def _mla_sample_attn(q3, rows3, pool_t, page_table, npg):
    b, rows, _ = q3.shape
    t_s = rows3.shape[1]
    n_pages = page_table.shape[1]
    psz = pool_t.shape[2]

    def page_spec(k):
        return pl.BlockSpec((1, MLA_ROW, psz), lambda s, c, pt: (pt[s * n_pages + c * npg + k], 0, 0))

    return pl.pallas_call(
        functools.partial(_mla_sample_kernel, npg, t_s),
        grid_spec=pltpu.PrefetchScalarGridSpec(
            num_scalar_prefetch=1,
            grid=(b, n_pages // npg),
            in_specs=[pl.BlockSpec((1, rows, MLA_QCAT), lambda s, c, pt: (s, 0, 0)),
                      pl.BlockSpec((1, t_s, MLA_ROW), lambda s, c, pt: (s, 0, 0))]
            + [page_spec(k) for k in range(npg)],
            out_specs=pl.BlockSpec((1, rows, KV_LORA), lambda s, c, pt: (s, 0, 0)),
            scratch_shapes=[pltpu.VMEM((rows, LANES), F32), pltpu.VMEM((rows, LANES), F32),
                            pltpu.VMEM((rows, KV_LORA), F32)]),
        out_shape=jax.ShapeDtypeStruct((b, rows, KV_LORA), BF16),
        compiler_params=_cparams("parallel", "arbitrary"),
        name="mla_sample_attn",
    )(page_table.reshape(-1), q3, rows3, *([pool_t] * npg))


def _uv_kernel(ctx_ref, w_ref, o_ref):
    for pr in range(MLA_HEADS // 2):
        o_ref[:, pr * LANES:(pr + 1) * LANES] = _dot(
            ctx_ref[:, pr * 2 * KV_LORA:(pr + 1) * 2 * KV_LORA], w_ref[pr]).astype(BF16)


def _mla_uv(ctx, w_pairs, tm):
    m = ctx.shape[0]
    return pl.pallas_call(
        _uv_kernel, grid=(m // tm,),
        in_specs=[pl.BlockSpec((tm, ctx.shape[1]), lambda i: (i, 0)), _full(w_pairs.shape)],
        out_specs=pl.BlockSpec((tm, MLA_HEADS * MLA_V_DIM), lambda i: (i, 0)),
        out_shape=jax.ShapeDtypeStruct((m, MLA_HEADS * MLA_V_DIM), BF16),
        compiler_params=_cparams("parallel"), name="mla_sample_uv",
    )(ctx, w_pairs)


def _mem_kv_kernel(x_ref, g_ref, w_ref, s_ref, st_ref, inv_ref, gk_ref, vm_ref, o_ref):
    h = _rms_rows(x_ref[...], g_ref[...]).astype(BF16)
    kv = _dot(h, w_ref[...])
    scale = _seg_scale(kv, s_ref, st_ref, inv_ref) * gk_ref[...] + vm_ref[...]
    o_ref[...] = kv * scale


def _mem_kv(mem2, w, tm):
    m, d = mem2.shape
    args = [mem2, w['g_mem'], w['w_mem'], w['Sk'], w['SkT'], w['invk'], w['gk'], w['vmask']]
    return pl.pallas_call(
        _mem_kv_kernel, grid=(m // tm,),
        in_specs=[pl.BlockSpec((tm, d), lambda i: (i, 0))] + [_full(a.shape) for a in args[1:]],
        out_specs=pl.BlockSpec((tm, MEM_W), lambda i: (i, 0)),
        out_shape=jax.ShapeDtypeStruct((m, MEM_W), F32),
        compiler_params=_cparams("parallel"), name="mem_kv",
    )(*args)


def _mem_attn_body(q, kv):
    outs = []
    scores = [_dot_nt(q[:, hd * LANES:(hd + 1) * LANES], kv[:, hd * LANES:(hd + 1) * LANES])
              for hd in range(MEM_HEADS)]
    for hd, s in enumerate(scores):
        sl = slice(hd * LANES, (hd + 1) * LANES)
        p = jnp.exp(s - jnp.max(s, axis=-1, keepdims=True))
        p = p / jnp.sum(p, axis=-1, keepdims=True)
        outs.append(_dot(p.astype(BF16), kv[:, sl]))
    return jnp.concatenate(outs, axis=-1)


def _mem_attn_kernel(q_ref, kv_ref, o_ref):
    o_ref[0] = _mem_attn_body(q_ref[0], kv_ref[0].astype(BF16)).astype(BF16)


def _mem_attn(q3, kv3, tq):
    n, t, _ = q3.shape
    return pl.pallas_call(
        _mem_attn_kernel, grid=(n, t // tq),
        in_specs=[pl.BlockSpec((1, tq, MEM_W), lambda b, i: (b, i, 0)),
                  pl.BlockSpec((1, kv3.shape[1], MEM_W), lambda b, i: (b, 0, 0))],
        out_specs=pl.BlockSpec((1, tq, MEM_W), lambda b, i: (b, i, 0)),
        out_shape=jax.ShapeDtypeStruct((n, t, MEM_W), BF16),
        compiler_params=_cparams("parallel", "parallel"), name="mem_attn",
    )(q3, kv3)


def _mem_rows_kernel(sb, q_ref, kv_ref, o_ref):
    nq, nkv = q_ref.shape[1], kv_ref.shape[2]
    same = (lax.broadcasted_iota(jnp.int32, (nq, nkv), 0) % MEM_HEADS
            == lax.broadcasted_iota(jnp.int32, (nq, nkv), 1) % MEM_HEADS)
    for s in range(sb):
        kv = kv_ref[0, s].astype(BF16)
        sc = jnp.where(same, _dot_nt(q_ref[s], kv), NEG)
        p = jnp.exp(sc - jnp.max(sc, axis=-1, keepdims=True))
        p = p / jnp.sum(p, axis=-1, keepdims=True)
        o_ref[s] = _dot(p.astype(BF16), kv).astype(BF16)


def _mem_attn_rows(q3, cache4, layer, sb):
    b, nq, _ = q3.shape
    return pl.pallas_call(
        functools.partial(_mem_rows_kernel, sb), grid=(b // sb,),
        in_specs=[pl.BlockSpec((sb, nq, LANES), lambda i: (i, 0, 0)),
                  pl.BlockSpec((1, sb, cache4.shape[2], LANES), lambda i: (layer, i, 0, 0))],
        out_specs=pl.BlockSpec((sb, nq, LANES), lambda i: (i, 0, 0)),
        out_shape=jax.ShapeDtypeStruct((b, nq, LANES), BF16),
        compiler_params=_cparams("parallel"), name="mem_attn_sample",
    )(q3, cache4)


def _out_proj_kernel(x_ref, a_ref, b_ref, wa_ref, wb_ref, o_ref):
    o_ref[...] = x_ref[...] + _dot(a_ref[...], wa_ref[...]) + _dot(b_ref[...], wb_ref[...])


def _out_proj(x, a, b, wa, wb, tm):
    m, d = x.shape
    row = lambda width: pl.BlockSpec((tm, width), lambda i: (i, 0))
    return pl.pallas_call(
        _out_proj_kernel, grid=(m // tm,),
        in_specs=[row(d), row(a.shape[1]), row(b.shape[1]), _full(wa.shape), _full(wb.shape)],
        out_specs=row(d), out_shape=jax.ShapeDtypeStruct((m, d), F32),
        compiler_params=_cparams("parallel"), name="out_proj",
    )(x, a, b, wa, wb)


def _ffn_kernel(seq_tiles, t_s, d_ff, x_ref, g_ref, wu_ref, cw_ref, cb_ref, wd_ref, *rest):
    if seq_tiles:
        o_ref, st_ref, g_scr = rest
    else:
        p1_ref, p2_ref, o_ref, st_ref = rest
    x = x_ref[...]
    tm = x.shape[0]
    h = _rms_rows(x, g_ref[...]).astype(BF16)
    u = _dot(h, wu_ref[...])
    gate, val = u[:, :d_ff], u[:, d_ff:]
    if seq_tiles:
        @pl.when(pl.program_id(0) % seq_tiles == 0)
        def _():
            g_scr[0:8, :] = jnp.zeros((8, d_ff), F32)
        g_scr[8:, :] = gate
        g1 = g_scr[7:tm + 7, :]
        g2 = g_scr[6:tm + 6, :]
        tail = gate[tm - 8:, :]
        g_scr[0:8, :] = tail
        st_ref[...] = tail
    else:
        tpos = lax.broadcasted_iota(jnp.int32, (tm, 1), 0) % t_s
        g1 = jnp.where(tpos >= 1, pltpu.roll(gate, 1, 0), p1_ref[...])
        g2 = jnp.where(tpos >= 2, pltpu.roll(gate, 2, 0), p2_ref[...])
        st_ref[...] = gate
    cw = cw_ref[...]
    conv = cb_ref[...] + g2 * cw[0:1] + g1 * cw[1:2] + gate * cw[2:3]
    act = (conv * (1.0 / (1.0 + jnp.exp(-conv))) * val).astype(BF16)
    o_ref[...] = x + _dot(act, wd_ref[...])


def _ffn(x, seq_len, w, tm, prev=None):
    m, d = x.shape
    d_ff = w['w_down'].shape[1]
    layer = w['layer']
    row = lambda width: pl.BlockSpec((tm, width), lambda i: (i, 0))
    single = dict(pipeline_mode=pl.Buffered(1))
    wspecs = [_full(w['g_ffn'].shape),
              pl.BlockSpec((None,) + w['w_up'].shape[1:], lambda i: (layer, 0, 0), **single),
              _full(w['conv_w'].shape), _full(w['conv_b'].shape),
              pl.BlockSpec((None,) + w['w_down'].shape[1:], lambda i: (layer, 0, 0), **single)]
    wargs = [w['g_ffn'], w['w_up'], w['conv_w'], w['conv_b'], w['w_down']]
    if prev is None:
        seq_tiles = seq_len // tm
        return pl.pallas_call(
            functools.partial(_ffn_kernel, seq_tiles, 0, d_ff), grid=(m // tm,),
            in_specs=[row(d)] + wspecs,
            out_specs=[row(d), pl.BlockSpec((8, d_ff), lambda i: (i, 0))],
            out_shape=[jax.ShapeDtypeStruct((m, d), F32),
                       jax.ShapeDtypeStruct((m // tm * 8, d_ff), F32)],
            scratch_shapes=[pltpu.VMEM((tm + 8, d_ff), F32)],
            compiler_params=_cparams("arbitrary"), name="ffn_prompt",
        )(x, *wargs)
    p1, p2 = prev
    return pl.pallas_call(
        functools.partial(_ffn_kernel, 0, seq_len, d_ff), grid=(m // tm,),
        in_specs=[row(d)] + wspecs + [row(d_ff), row(d_ff)],
        out_specs=[row(d), row(d_ff)],
        out_shape=[jax.ShapeDtypeStruct((m, d), F32), jax.ShapeDtypeStruct((m, d_ff), F32)],
        compiler_params=_cparams("parallel"), name="ffn_sample",
    )(x, *wargs, p1, p2)


QW = N_GROUPS * DIL_HEADS * LANES
KVW = N_GROUPS * DIL_SLAB


def _proj_dil_kernel(prompt, win_skips, x_ref, ga_ref, win_ref, sq_ref, sqt_ref, invq_ref, gq_ref,
                     sk_ref, skt_ref, invk_ref, gk_ref, vm_ref,
                     sm_ref, smt_ref, invm_ref, gm_ref, *rest):
    x = x_ref[0] if prompt else x_ref[...]
    h = _rms_rows(x, ga_ref[...]).astype(BF16)
    z = _dot(h, win_ref[...])
    q = z[:, :QW]
    qn = q * _seg_scale(q, sq_ref, sqt_ref, invq_ref) * gq_ref[...]
    kv = z[:, QW:QW + KVW]
    kvn = kv * (_seg_scale(kv, sk_ref, skt_ref, invk_ref) * gk_ref[...] + vm_ref[...])
    mq = z[:, QW + KVW:]
    mqn = (mq * _seg_scale(mq, sm_ref, smt_ref, invm_ref) * gm_ref[...]).astype(BF16)
    if not prompt:
        q_out, kv_out, mq_out = rest
        q_out[...] = qn.astype(BF16)
        kv_out[...] = kvn
        mq_out[...] = mqn
        return
    mq_out, win_outs = rest[0], rest[1:4]
    q_outs, kvb_outs, q_scr, kv_scr = rest[4:7], rest[7:10], rest[10], rest[11]
    mq_out[0] = mqn
    for c in range(QW // LANES):
        q_scr[c] = qn[:, c * LANES:(c + 1) * LANES]
    for c in range(KVW // LANES):
        kv_scr[c] = kvn[:, c * LANES:(c + 1) * LANES]
    tm = q.shape[0]
    qc, kc = DIL_HEADS, DIL_SLAB // LANES
    for g in range(N_GROUPS):
        rows = win_outs[g].shape[1]

        @pl.when(pl.program_id(1) >= win_skips[g])
        def _(g=g, rows=rows):
            for s in range(8):
                chunk = g * kc + (s % DIL_HEADS) * (DIL_ROW // LANES) + s // DIL_HEADS
                win_outs[g][0, :, s, :] = kv_scr[chunk, tm - rows:tm, :]
    for g, d in enumerate(DIL_RATES):
        for r in range(d):
            rows = pl.ds(r, tm // d, stride=d) if d > 1 else slice(None)
            for c in range(qc):
                q_outs[g][0, r, :, c * LANES:(c + 1) * LANES] = q_scr[g * qc + c, rows, :].astype(BF16)
            for c in range(kc):
                kvb_outs[g][0, r, :, c * LANES:(c + 1) * LANES] = kv_scr[g * kc + c, rows, :].astype(BF16)


def _proj_dil(x3, prompt, tm, w):
    n, t, d = x3.shape
    args = [x3, w['g_attn'], w['w_in'], w['Sq'], w['SqT'], w['invq'], w['gq'],
            w['Sk'], w['SkT'], w['invk'], w['gk'], w['vmask'], w['Sm'], w['SmT'], w['invm'], w['gm']]
    if not prompt:
        x2 = x3.reshape(n * t, d)
        m = n * t
        row = lambda width: pl.BlockSpec((tm, width), lambda i: (i, 0))
        return pl.pallas_call(
            functools.partial(_proj_dil_kernel, False, None), grid=(m // tm,),
            in_specs=[row(d)] + [_full(a.shape) for a in args[1:]],
            out_specs=[row(QW), row(KVW), row(MEM_W)],
            out_shape=[jax.ShapeDtypeStruct((m, QW), BF16), jax.ShapeDtypeStruct((m, KVW), F32),
                       jax.ShapeDtypeStruct((m, MEM_W), BF16)],
            compiler_params=_cparams("parallel"), name="proj_dil_sample",
        )(x2, *args[1:])
    row3 = lambda width: pl.BlockSpec((1, tm, width), lambda b, i: (b, i, 0))
    outs = [jax.ShapeDtypeStruct((n, t, MEM_W), BF16)]
    ospecs = [row3(MEM_W)]
    skips = []
    for wdw in DIL_WINDOWS:
        tail = min(wdw, t)
        rows = min(tail, tm)
        skip = (t - tail) // tm
        skips.append(skip)
        outs.append(jax.ShapeDtypeStruct((n, tail, 8, LANES), F32))
        ospecs.append(pl.BlockSpec((1, rows, 8, LANES),
                                   lambda b, i, skip=skip, last=tail // rows - 1:
                                   (b, jnp.clip(i - skip, 0, last), 0, 0)))
    for width in (DIL_HEADS * LANES, DIL_SLAB):
        for dil in DIL_RATES:
            outs.append(jax.ShapeDtypeStruct((n, dil, t // dil, width), BF16))
            ospecs.append(pl.BlockSpec((1, dil, tm // dil, width), lambda b, i: (b, 0, i, 0)))
    return pl.pallas_call(
        functools.partial(_proj_dil_kernel, True, tuple(skips)), grid=(n, t // tm),
        in_specs=[row3(d)] + [_full(a.shape) for a in args[1:]],
        out_specs=ospecs, out_shape=outs,
        scratch_shapes=[pltpu.VMEM((QW // LANES, tm, LANES), F32), pltpu.VMEM((KVW // LANES, tm, LANES), F32)],
        compiler_params=_cparams("parallel", "arbitrary"), name="proj_dil_prompt",
    )(*args)


def _band_kernel(tq, nqb, q_ref, kp_ref, kc_ref, bias_ref, o_ref, ml_ref):
    i = pl.program_id(1)
    kcat = jnp.concatenate([kp_ref[0], kc_ref[0]], axis=0)
    first = jnp.logical_and(i == 0, lax.broadcasted_iota(jnp.int32, (tq, 2 * tq), 1) < tq)
    work = [(j, hd) for j in range(nqb) for hd in range(DIL_HEADS)]
    scores = [_dot_nt(q_ref[0, j * tq:(j + 1) * tq, hd * LANES:(hd + 1) * LANES],
                      kcat[j * tq:(j + 2) * tq, hd * DIL_ROW:hd * DIL_ROW + LANES]) for j, hd in work]
    stats = [[] for _ in range(nqb)]
    for (j, hd), s in zip(work, scores):
        s = s + bias_ref[hd]
        if j == 0:
            s = jnp.where(first, NEG, s)
        m = jnp.max(s, axis=-1, keepdims=True)
        p = jnp.exp(s - m)
        l = jnp.sum(p, axis=-1, keepdims=True)
        o_ref[0, j * tq:(j + 1) * tq, hd * DIL_ROW:(hd + 1) * DIL_ROW] = _dot(
            (p / l).astype(BF16), kcat[j * tq:(j + 2) * tq, hd * DIL_ROW:(hd + 1) * DIL_ROW]).astype(BF16)
        stats[j].append((m, l))
    for j in range(nqb):
        ml_ref[0, j * tq:(j + 1) * tq] = _cols([m for m, _ in stats[j]] + [l for _, l in stats[j]], LANES)


def _band_attn(q, kv, bias, tq, nqb):
    s, l, _ = q.shape
    tb = tq * nqb
    return pl.pallas_call(
        functools.partial(_band_kernel, tq, nqb), grid=(s, l // tb),
        in_specs=[pl.BlockSpec((1, tb, DIL_HEADS * LANES), lambda b, i: (b, i, 0)),
                  pl.BlockSpec((1, tq, DIL_SLAB), lambda b, i: (b, jnp.maximum(i * nqb - 1, 0), 0)),
                  pl.BlockSpec((1, tb, DIL_SLAB), lambda b, i: (b, i, 0)),
                  _full(bias.shape)],
        out_specs=[pl.BlockSpec((1, tb, DIL_SLAB), lambda b, i: (b, i, 0)),
                   pl.BlockSpec((1, tb, LANES), lambda b, i: (b, i, 0))],
        out_shape=[jax.ShapeDtypeStruct((s, l, DIL_SLAB), BF16),
                   jax.ShapeDtypeStruct((s, l, LANES), F32)],
        compiler_params=_cparams("parallel", "parallel"), name="dil_band_attn",
    )(q, kv, kv, bias)


def _merge_kernel(rates, *refs):
    ng = len(rates)
    o_refs, ml_refs, out_ref = refs[:ng], refs[ng:2 * ng], refs[2 * ng]
    scr = refs[2 * ng + 1:]
    nch = DIL_SLAB // LANES
    chunks, mls = [], []
    for g, d in enumerate(rates):
        if d == 1:
            chunks.append([o_refs[g][0, 0, :, c * LANES:(c + 1) * LANES].astype(F32) for c in range(nch)])
            mls.append(ml_refs[g][0, 0])
        else:
            o_scr, ml_scr = scr[2 * g], scr[2 * g + 1]
            tm = ml_scr.shape[0]
            for r in range(d):
                rows = pl.ds(r, tm // d, stride=d)
                for c in range(nch):
                    o_scr[c, rows, :] = o_refs[g][0, r, :, c * LANES:(c + 1) * LANES].astype(F32)
                ml_scr[rows, :] = ml_refs[g][0, r]
            chunks.append([o_scr[c] for c in range(nch)])
            mls.append(ml_scr[...])
    m_all = functools.reduce(jnp.maximum, [ml[:, :DIL_HEADS] for ml in mls])
    es = [ml[:, DIL_HEADS:2 * DIL_HEADS] * jnp.exp(ml[:, :DIL_HEADS] - m_all) for ml in mls]
    tot = functools.reduce(jnp.add, es)
    ws = [e / tot for e in es]
    for c in range(nch):
        hd = c // (DIL_ROW // LANES)
        acc = None
        for g in range(ng):
            term = chunks[g][c] * ws[g][:, hd:hd + 1]
            acc = term if acc is None else acc + term
        out_ref[0, :, c * LANES:(c + 1) * LANES] = acc.astype(BF16)


def _merge(os_, mls, rates, n, t, tm):
    specs = [pl.BlockSpec((1, d, tm // d, DIL_SLAB), lambda b, i: (b, 0, i, 0)) for d in rates] \
        + [pl.BlockSpec((1, d, tm // d, LANES), lambda b, i: (b, 0, i, 0)) for d in rates]
    scratch = []
    for d in rates:
        scratch += [pltpu.VMEM((DIL_SLAB // LANES, tm, LANES), F32), pltpu.VMEM((tm, LANES), F32)]
    return pl.pallas_call(
        functools.partial(_merge_kernel, tuple(rates)), grid=(n, t // tm),
        in_specs=specs,
        out_specs=pl.BlockSpec((1, tm, DIL_SLAB), lambda b, i: (b, i, 0)),
        out_shape=jax.ShapeDtypeStruct((n, t, DIL_SLAB), BF16),
        scratch_shapes=scratch,
        compiler_params=_cparams("parallel", "parallel"), name="dil_merge",
    )(*os_, *mls)


def _dil_sample_kernel(d, wb, t_s, q_ref, new_ref, buf_ref, bias_ref, o_ref, ml_ref, st_ref, *scr):
    nk = DIL_KEYS - 1
    new = new_ref[0]
    st_ref[0, 0:wb - t_s] = buf_ref[0, t_s:wb]
    st_ref[0, wb - t_s:wb] = new
    if d < t_s:
        src = scr[0]
        src[0:wb] = buf_ref[0]
        src[wb:wb + t_s] = new
    else:
        src = buf_ref.at[0]
    top = lax.broadcasted_iota(jnp.int32, (8, LANES), 0) < DIL_HEADS
    lane = lax.broadcasted_iota(jnp.int32, (8, LANES), 1)
    bias = bias_ref[...]
    nq, ntok = _dil_classes(d, t_s)

    def both_halves(col):
        x = jnp.where(top, jnp.broadcast_to(col, (8, LANES)), 0.0)
        return x + pltpu.roll(x, DIL_HEADS, 0)

    classes = range(t_s // nq)
    toks, scores, stats, outs = [], [], [], []
    for r in classes:
        if d < t_s:
            a3 = src[pl.ds(r, ntok, stride=d)] if d > 1 else src[pl.ds(r, ntok)]
        else:
            pad = jnp.zeros((ntok - nk - 1, 8, LANES), F32)
            a3 = jnp.concatenate([src[pl.ds(r, nk, stride=d)], new[r:r + 1], pad], axis=0)
        a = a3.reshape(ntok * 8, LANES).astype(BF16)
        qc = q_ref[0, pl.ds(r, nq, stride=d)] if nq > 1 else q_ref[0, r:r + 1]
        qc = qc.reshape(nq * 8, LANES).astype(BF16)
        toks.append(a)
        scores.append(_dot_nt(qc, a) + bias)
    for r in classes:
        s = scores[r]
        m = jnp.max(s, axis=-1, keepdims=True)
        p = jnp.exp(s - m)
        l = jnp.sum(p, axis=-1, keepdims=True)
        pn = p / l
        stats.append((m, l))
        p2 = jnp.concatenate([pn, pltpu.roll(pn, DIL_HEADS, 1)], axis=0).astype(BF16)
        outs.append(_dot(p2, toks[r]))
    for r in classes:
        o2, (m, l) = outs[r], stats[r]
        for k in range(nq):
            i = r + k * d
            lo, hi = o2[8 * k:8 * k + 8], o2[8 * (nq + k):8 * (nq + k) + 8]
            o_ref[0, i] = jnp.where(top, lo, pltpu.roll(hi, DIL_HEADS, 0))
            ml_ref[0, i] = jnp.where(lane == 0, both_halves(m[8 * k:8 * k + 8]),
                                     jnp.where(lane == 1, both_halves(l[8 * k:8 * k + 8]), 0.0))


def _dil_classes(d, t_s):
    nq = max(t_s // d, 1)
    ntok = DIL_KEYS - 1 + nq
    return nq, ntok + ntok % 2


def _dil_sample(qg, newg, buf, bias, d):
    b, t_s = qg.shape[:2]
    wb = buf.shape[1]
    blk = lambda r: pl.BlockSpec((1, r, 8, LANES), lambda s: (s, 0, 0, 0))
    tiles = jax.ShapeDtypeStruct((b, t_s, 8, LANES), F32)
    return pl.pallas_call(
        functools.partial(_dil_sample_kernel, d, wb, t_s), grid=(b,),
        in_specs=[blk(t_s), blk(t_s), blk(wb), _full(bias.shape)],
        out_specs=[blk(t_s), blk(t_s), blk(wb)],
        out_shape=[tiles, tiles, jax.ShapeDtypeStruct(buf.shape, F32)],
        scratch_shapes=[pltpu.VMEM((wb + t_s, 8, LANES), F32)] if d < t_s else [],
        compiler_params=_cparams("parallel"), name="dil_sample_attn",
    )(qg, newg, buf, bias)


def _merge_tiles_kernel(ng, *refs):
    o_refs, ml_refs, out_ref = refs[:ng], refs[ng:2 * ng], refs[2 * ng]
    ms = [r[...][:, :, 0:1] for r in ml_refs]
    ls = [r[...][:, :, 1:2] for r in ml_refs]
    m_all = functools.reduce(jnp.maximum, ms)
    es = [l * jnp.exp(m - m_all) for m, l in zip(ms, ls)]
    tot = functools.reduce(jnp.add, es)
    out_ref[...] = functools.reduce(jnp.add, [o[...] * (e / tot) for o, e in zip(o_refs, es)])


def _merge_tiles(os_, mls, tr):
    r = os_[0].shape[0]
    spec = pl.BlockSpec((tr, 8, LANES), lambda i: (i, 0, 0))
    return pl.pallas_call(
        functools.partial(_merge_tiles_kernel, len(os_)), grid=(r // tr,),
        in_specs=[spec] * (2 * len(os_)), out_specs=spec,
        out_shape=jax.ShapeDtypeStruct((r, 8, LANES), F32),
        compiler_params=_cparams("parallel"), name="dil_merge_sample",
    )(*os_, *mls)


def _seg_mats(width, segments):
    s = np.zeros((width, LANES), np.float32)
    inv = np.ones((1, LANES), np.float32)
    for k, (a, n) in enumerate(segments):
        s[a:a + n, k] = 1.0
        inv[0, k] = 1.0 / n
    return jnp.asarray(s, BF16), jnp.asarray(np.concatenate([s.T, s.T]), BF16), jnp.asarray(inv)


def _pad_heads(wm, heads, dim, slab, offset=0):
    r = wm.shape[0]
    out = jnp.pad(wm.reshape(r, heads, dim), ((0, 0), (0, 0), (offset, slab - dim - offset)))
    return out.reshape(r, heads * slab)


def _mem_q_tables(g_qn_mem):
    seg = [(hd * LANES, MEM_DIM) for hd in range(MEM_HEADS)]
    sm, smt, invm = _seg_mats(MEM_W, seg)
    gm = _pad_heads(jnp.tile(g_qn_mem, MEM_HEADS)[None, :], MEM_HEADS, MEM_DIM, LANES) * MEM_SCALE
    return dict(Sm=sm, SmT=smt, invm=invm, gm=gm)


def _mla_tables(pos, g_qn_nope, g_qn_pe, g_kn_pe):
    hr = ROPE_DIM // 2
    inv = ROPE_THETA ** (-jnp.arange(0, ROPE_DIM, 2, dtype=F32) / ROPE_DIM)
    z = lambda k: jnp.zeros((k,), F32)
    rest = LANES - NOPE_DIM - ROPE_DIM
    fq = jnp.concatenate([z(NOPE_DIM), inv, inv, z(rest)])
    cq_amp = jnp.concatenate([g_qn_nope, g_qn_pe, z(rest)]) * MLA_SCALE
    sq_amp = jnp.concatenate([z(NOPE_DIM), -g_qn_pe[hr:], g_qn_pe[:hr], z(rest)]) * MLA_SCALE
    fk = jnp.concatenate([inv, inv, inv, inv, z(LANES - 2 * ROPE_DIM)])
    ck_amp = jnp.concatenate([g_kn_pe, z(LANES - ROPE_DIM)])
    sk_amp = jnp.concatenate([z(ROPE_DIM), -g_kn_pe[hr:], g_kn_pe[:hr], z(LANES - 2 * ROPE_DIM)])
    p = pos.astype(F32)[:, None]
    aq, ak = p * fq[None, :], p * fk[None, :]
    return (jnp.cos(aq) * cq_amp[None, :], jnp.sin(aq) * sq_amp[None, :],
            jnp.cos(ak) * ck_amp[None, :] + jnp.sin(ak) * sk_amp[None, :])


def _prep_mla(i, j, g_attn, w_o, w_in_a, g_q_a, w_q_b, g_kv_a, w_uk, w_uv, g_qn_mem):
    d = w_in_a.shape[1]
    hr = ROPE_DIM // 2
    wi = w_in_a[j]
    o1, o2, o3 = Q_LORA, Q_LORA + KV_LORA, Q_LORA + KV_LORA + ROPE_DIM
    kpe = wi[:, o2:o3]
    w_in = jnp.concatenate([wi[:, :o2], _pad_heads(wi[:, o3:], MEM_HEADS, MEM_DIM, LANES),
                            kpe, jnp.concatenate([kpe[:, hr:], kpe[:, :hr]], axis=1),
                            jnp.zeros((d, LANES - 2 * ROPE_DIM), F32)], axis=1).astype(BF16)
    hq = NOPE_DIM + ROPE_DIM
    wq3 = w_q_b[j].reshape(Q_LORA, MLA_HEADS, hq)
    wq = _pad_heads(w_q_b[j], MLA_HEADS, hq, LANES).astype(BF16)
    sw = jnp.concatenate([wq3[:, :, NOPE_DIM + hr:], wq3[:, :, NOPE_DIM:NOPE_DIM + hr]], axis=2)
    wq_sw = _pad_heads(sw.reshape(Q_LORA, MLA_HEADS * ROPE_DIM), MLA_HEADS, ROPE_DIM, LANES, NOPE_DIM).astype(BF16)
    seg = []
    for hd in range(MLA_HEADS):
        seg += [(hd * LANES, NOPE_DIM), (hd * LANES + NOPE_DIM, ROPE_DIM)]
    s, st, inv = _seg_mats(MLA_HEADS * LANES, seg)
    wk_c = _pad_heads(w_uk[j].reshape(KV_LORA, MLA_HEADS * NOPE_DIM), MLA_HEADS, NOPE_DIM, LANES).astype(BF16)
    eye = np.zeros((LANES, MLA_HEADS, LANES), np.float32)
    for r in range(ROPE_DIM):
        eye[r, :, NOPE_DIM + r] = 1.0
    wk_p = jnp.asarray(eye.reshape(LANES, MLA_HEADS * LANES), BF16)
    wv4 = w_uv[j].reshape(KV_LORA, MLA_HEADS // 2, 2, MLA_V_DIM)
    lo = ((0, 0), (0, 0), (0, LANES - MLA_V_DIM))
    hi = ((0, 0), (0, 0), (LANES - MLA_V_DIM, 0))
    wv = jnp.stack([jnp.pad(wv4[:, :, 0], lo), jnp.pad(wv4[:, :, 1], hi)], axis=2)
    wv = wv.reshape(KV_LORA, MLA_HEADS * LANES).astype(BF16)
    vone = np.zeros((1, MLA_HEADS, LANES), np.float32)
    vone[0, 0::2, MLA_V_DIM] = 1.0
    vone[0, 1::2, 0] = 1.0
    vone = jnp.asarray(vone.reshape(1, MLA_HEADS * LANES))
    pe_rows = np.zeros((ROPE_DIM, MLA_QCAT), np.float32)
    pe_rows[np.arange(ROPE_DIM), KV_LORA + np.arange(ROPE_DIM)] = 1.0
    wcat = jnp.concatenate([
        jnp.pad(jnp.transpose(w_uk[j], (1, 2, 0)), ((0, 0), (0, 0), (0, MLA_QCAT - KV_LORA))),
        jnp.broadcast_to(jnp.asarray(pe_rows)[None], (MLA_HEADS, ROPE_DIM, MLA_QCAT)),
        jnp.zeros((MLA_HEADS, LANES - hq, MLA_QCAT), F32)], axis=1)
    wvp = jnp.transpose(wv4, (1, 2, 0, 3))
    uvp = jnp.concatenate([jnp.pad(wvp[:, 0], lo), jnp.pad(wvp[:, 1], hi)], axis=1)
    ntok = MLA_HEADS * MLA_V_DIM
    w = dict(g_attn=g_attn[i][None], w_in=w_in, g_q_a=g_q_a[j][None], wq=wq, wq_sw=wq_sw,
             g_kv_a=g_kv_a[j][None], S=s, ST=st, inv=inv, wk_c=wk_c, wk_p=wk_p, wv=wv, vone=vone,
             wcat=wcat.astype(BF16), uv_pairs=uvp.astype(BF16),
             wo_tok=w_o[i][:ntok].astype(BF16),
             wo_mem=_pad_rows(w_o[i][ntok:], MEM_HEADS, MEM_DIM, LANES, MEM_DIM).astype(BF16))
    w.update(_mem_q_tables(g_qn_mem[i]))
    return w


def _pad_rows(wm, heads, dim, slab, offset):
    c = wm.shape[1]
    out = jnp.pad(wm.reshape(heads, dim, c), ((0, 0), (offset, slab - dim - offset), (0, 0)))
    return out.reshape(heads * slab, c)


def _prep_dil(i, j, g_attn, w_o, w_in_b, g_qn_b, g_kn_b, g_qn_mem):
    wi = w_in_b[j]
    gh = N_GROUPS * DIL_HEADS
    nqk = gh * DIL_QK
    wq = _pad_heads(wi[:, :nqk], gh, DIL_QK, LANES)
    wk = wi[:, nqk:2 * nqk].reshape(-1, gh, DIL_QK)
    wv = wi[:, 2 * nqk:2 * nqk + gh * DIL_V].reshape(-1, gh, DIL_V)
    wkv = jnp.concatenate([wk, wv], axis=2).reshape(-1, gh * DIL_ROW)
    wmq = _pad_heads(wi[:, 2 * nqk + gh * DIL_V:], MEM_HEADS, MEM_DIM, LANES)
    w_in = jnp.concatenate([wq, wkv, wmq], axis=1).astype(BF16)
    sq, sqt, invq = _seg_mats(QW, [(k * LANES, DIL_QK) for k in range(gh)])
    sk, skt, invk = _seg_mats(KVW, [(k * DIL_ROW, DIL_QK) for k in range(gh)])
    gq = _pad_heads(jnp.repeat(g_qn_b[j], DIL_HEADS, axis=0).reshape(1, nqk), gh, DIL_QK, LANES) * DIL_SCALE
    gk = _pad_heads(jnp.repeat(g_kn_b[j], DIL_HEADS, axis=0).reshape(1, nqk), gh, DIL_QK, DIL_ROW)
    vmask = _pad_heads(jnp.ones((1, gh * DIL_V), F32), gh, DIL_V, DIL_ROW, DIL_QK)
    ntok = DIL_HEADS * DIL_V
    w = dict(g_attn=g_attn[i][None], w_in=w_in, Sq=sq, SqT=sqt, invq=invq, gq=gq,
             Sk=sk, SkT=skt, invk=invk, gk=gk, vmask=vmask,
             wo_tok=_pad_rows(w_o[i][:ntok], DIL_HEADS, DIL_V, DIL_ROW, DIL_QK).astype(BF16),
             wo_mem=_pad_rows(w_o[i][ntok:], MEM_HEADS, MEM_DIM, LANES, MEM_DIM).astype(BF16))
    w.update(_mem_q_tables(g_qn_mem[i]))
    return w


def _prep_mem(i, g_mem, w_mem_kv, g_kn_mem):
    sk, skt, invk = _seg_mats(MEM_W, [(hd * LANES, MEM_DIM) for hd in range(MEM_HEADS)])
    gk = _pad_heads(jnp.tile(g_kn_mem[i], MEM_HEADS)[None, :], MEM_HEADS, MEM_DIM, LANES)
    vmask = _pad_heads(jnp.ones((1, MEM_HEADS * MEM_DIM), F32), MEM_HEADS, MEM_DIM, LANES, MEM_DIM)
    return dict(g_mem=g_mem[i][None], w_mem=w_mem_kv[i].astype(BF16), Sk=sk, SkT=skt, invk=invk, gk=gk, vmask=vmask)


def _prep_ffn(i, g_ffn, w_up_b, conv_w, conv_b, w_down_b):
    return dict(layer=i, g_ffn=g_ffn[i][None], w_up=w_up_b, conv_w=conv_w[i],
                conv_b=conv_b[i][None], w_down=w_down_b)


def _rel_bucket(dist):
    max_exact = N_BUCKETS // 2
    dd = jnp.maximum(dist.astype(F32), 1.0)
    large = max_exact + (jnp.log(dd / max_exact) / math.log(MAX_DISTANCE / max_exact)
                         * (N_BUCKETS - max_exact)).astype(jnp.int32)
    large = jnp.minimum(large, N_BUCKETS - 1)
    return jnp.where(dist < max_exact, dist, large)


def _dil_bias_tables(rel_bias, tq, t_s):
    band, samp = [], []

    def toeplitz(rev, rows, cols):
        period = cols + rows + 1
        f = jnp.concatenate([rev, jnp.full((DIL_HEADS, period - DIL_KEYS), NEG, F32)], axis=1)
        return jnp.tile(f, (1, rows))[:, :rows * (period - 1)].reshape(DIL_HEADS, rows, period - 1)[:, :, :cols]

    srow = jnp.arange(8)[None, None, None, :] == jnp.arange(DIL_HEADS)[:, None, None, None]
    for g in range(N_GROUPS):
        bk = _rel_bucket(DIL_RATES[g] * jnp.arange(DIL_KEYS, dtype=jnp.int32))
        b = rel_bias[bk][:, g * DIL_HEADS:(g + 1) * DIL_HEADS].T.astype(F32)
        rev = b[:, ::-1]
        band.append(toeplitz(rev, tq, 2 * tq))
        nq, ntok = _dil_classes(DIL_RATES[g], t_s)
        tab = jnp.where(srow, toeplitz(rev, nq, ntok)[..., None], NEG).reshape(DIL_HEADS, nq, ntok * 8)
        tab = jnp.pad(tab.transpose(1, 0, 2), ((0, 0), (0, 8 - DIL_HEADS), (0, 0)), constant_values=NEG)
        samp.append(tab.reshape(nq * 8, ntok * 8))
    return band, samp


def kernel(x_prompt, x_sample, cache_mla, state_win0, state_win1, state_win2, cache_mem, state_conv, page_table, mem_prompt, rel_bias, g_attn, w_o, w_in_a, g_q_a, w_q_b, g_kv_a, w_uk, w_uv, g_qn_nope, g_qn_pe, g_kn_pe, w_in_b, g_qn_b, g_kn_b, g_mem, w_mem_kv, g_qn_mem, g_kn_mem, g_ffn, w_up, conv_w, conv_b, w_down):
    state_wins = (state_win0, state_win1, state_win2)
    n, t_p, d = x_prompt.shape
    b, t_s, _ = x_sample.shape
    n_mem = mem_prompt.shape[1]
    d_ff = w_down.shape[1]
    past = page_table.shape[1] * cache_mla.shape[2]
    mp, ms = n * t_p, b * t_s
    tm_p = min(256, t_p)
    tm_s = min(256, ms)
    tq_band = DIL_KEYS - 1
    assert t_s >= 2 and t_p % tm_p == 0 and ms % tm_s == 0 and tm_s % t_s == 0
    assert all(sw.shape[2] == wdw and wdw == (DIL_KEYS - 1) * r
               for sw, wdw, r in zip(state_wins, DIL_WINDOWS, DIL_RATES))

    xp = x_prompt.reshape(mp, d)
    xs = x_sample.reshape(ms, d)
    mem2 = mem_prompt.reshape(n * n_mem, d)
    pos_p = jnp.arange(t_p)
    pos_s = jnp.tile(past + jnp.arange(t_s), tm_s // t_s)
    band_bias, samp_bias = _dil_bias_tables(rel_bias, tq_band, t_s)
    cache_rows = cache_mem.reshape(cache_mem.shape[0], b, n_mem * MEM_HEADS, 2 * MEM_DIM)

    w_up_b, w_down_b = w_up.astype(BF16), w_down.astype(BF16)
    mem_out, conv_p_out, conv_s_out = [], [], []
    for i in range(2):
        j = i // 2
        if i % 2 == 0:
            w = _prep_mla(i, j, g_attn, w_o, w_in_a, g_q_a, w_q_b, g_kv_a, w_uk, w_uv, g_qn_mem)
            cq, sq, tk = _mla_tables(pos_p, g_qn_nope[j], g_qn_pe[j], g_kn_pe[j])
            q_p, rows_p, mq_p, k_p, v_p = _proj_mla(xp, True, tm_p, w, cq, sq, tk)
            cq, sq, tk = _mla_tables(pos_s, g_qn_nope[j], g_qn_pe[j], g_kn_pe[j])
            qc_s, rows_s, mq_s = _proj_mla(xs, False, tm_s, w, cq, sq, tk)
            hw = MLA_HEADS * LANES
            tok_p = _mla_flash(q_p.reshape(n, t_p, hw), k_p.reshape(n, t_p, hw), v_p.reshape(n, t_p, hw),
                               min(512, t_p)).reshape(mp, -1)
            q3 = qc_s.reshape(b, t_s, MLA_HEADS, MLA_QCAT).transpose(0, 2, 1, 3).reshape(b, MLA_HEADS * t_s, MLA_QCAT)
            ctx = _mla_sample_attn(q3, rows_s.reshape(b, t_s, MLA_ROW), cache_mla[j].transpose(0, 2, 1),
                                   page_table, min(MLA_PAGES_PER_STEP, page_table.shape[1]))
            ctx = ctx.reshape(b, MLA_HEADS, t_s, KV_LORA).transpose(0, 2, 1, 3).reshape(ms, MLA_HEADS * KV_LORA)
            tok_s = _mla_uv(ctx, w['uv_pairs'], tm_s)
            mla_rows_p = rows_p.reshape(1, n, t_p, MLA_ROW)
            mla_rows_s = rows_s.reshape(1, b, t_s, MLA_ROW)
        else:
            w = _prep_dil(i, j, g_attn, w_o, w_in_b, g_qn_b, g_kn_b, g_qn_mem)
            res = _proj_dil(xp.reshape(n, t_p, d), True, tm_p, w)
            mq_p = res[0].reshape(mp, MEM_W)
            nc = DIL_ROW // LANES
            os_, mls = [], []
            for g, dil in enumerate(DIL_RATES):
                l = t_p // dil
                o, ml = _band_attn(res[4 + g].reshape(n * dil, l, DIL_HEADS * LANES),
                                   res[7 + g].reshape(n * dil, l, DIL_SLAB), band_bias[g], tq_band,
                                   2 if l % (2 * tq_band) == 0 else 1)
                os_.append(o.reshape(n, dil, l, DIL_SLAB))
                mls.append(ml.reshape(n, dil, l, LANES))
            tok_p = _merge(os_, mls, DIL_RATES, n, t_p, tm_p).reshape(mp, DIL_SLAB)
            win_p = [wt.reshape(n, wt.shape[1], nc, DIL_HEADS, LANES).transpose(0, 1, 3, 2, 4)
                     .reshape(1, n, wt.shape[1], DIL_HEADS, DIL_ROW) for wt in res[1:4]]
            q_s, kv_s, mq_s = _proj_dil(xs.reshape(b, t_s, d), False, tm_s, w)
            nc = DIL_ROW // LANES
            q_t = jnp.pad(q_s.astype(F32).reshape(b, t_s, N_GROUPS, DIL_HEADS, LANES),
                          ((0, 0), (0, 0), (0, 0), (0, 8 - DIL_HEADS), (0, 0))).transpose(2, 0, 1, 3, 4)
            kv_t = kv_s.reshape(b, t_s, N_GROUPS, DIL_HEADS, nc, LANES).transpose(2, 0, 1, 4, 3, 5)
            kv_t = kv_t.reshape(N_GROUPS, b, t_s, 8, LANES)
            os_, mls, win_s = [], [], []
            for g, dil in enumerate(DIL_RATES):
                buf = state_wins[g][j]
                wb = buf.shape[1]
                buf_t = buf.reshape(b, wb, DIL_HEADS, nc, LANES).transpose(0, 1, 3, 2, 4).reshape(b, wb, 8, LANES)
                o, ml, st = _dil_sample(q_t[g], kv_t[g], buf_t, samp_bias[g], dil)
                os_.append(o.reshape(ms, 8, LANES))
                mls.append(ml.reshape(ms, 8, LANES))
                st = st.reshape(b, wb, nc, DIL_HEADS, LANES).transpose(0, 1, 3, 2, 4)
                win_s.append(st.reshape(1, b, wb, DIL_HEADS, DIL_ROW))
            tok_s = _merge_tiles(os_, mls, tm_s).reshape(ms, nc, DIL_HEADS, LANES).transpose(0, 2, 1, 3)
            tok_s = tok_s.reshape(ms, DIL_SLAB).astype(BF16)

        wm = _prep_mem(i, g_mem, w_mem_kv, g_kn_mem)
        mkv = _mem_kv(mem2, wm, min(256, n * n_mem))
        mem_out.append(mkv.reshape(n, n_mem, MEM_HEADS, 2 * MEM_DIM))
        mo_p = _mem_attn(mq_p.reshape(n, t_p, MEM_W), mkv.reshape(n, n_mem, MEM_W), min(512, t_p)).reshape(mp, MEM_W)
        mo_s = _mem_attn_rows(mq_s.reshape(b, t_s * MEM_HEADS, LANES), cache_rows, i, min(8, b)).reshape(ms, MEM_W)
        xp = _out_proj(xp, tok_p, mo_p, w['wo_tok'], w['wo_mem'], tm_p)
        xs = _out_proj(xs, tok_s, mo_s, w['wo_tok'], w['wo_mem'], tm_s)

        wf = _prep_ffn(i, g_ffn, w_up_b, conv_w, conv_b, w_down_b)
        xp, tails = _ffn(xp, t_p, wf, tm_p)
        conv_p_out.append(tails.reshape(n, t_p // tm_p, 8, d_ff)[:, -1, 6:, :])
        prev = state_conv[i]
        zeros = jnp.zeros((b, t_s - 2, d_ff), F32)
        p2 = jnp.concatenate([prev, zeros], axis=1).reshape(ms, d_ff)
        p1 = jnp.concatenate([prev[:, 1:], zeros, jnp.zeros((b, 1, d_ff), F32)], axis=1).reshape(ms, d_ff)
        xs, gates = _ffn(xs, t_s, wf, tm_s, prev=(p1, p2))
        conv_s_out.append(gates.reshape(b, t_s, d_ff)[:, t_s - 2:, :])

    return (xp.reshape(n, t_p, d), xs.reshape(b, t_s, d), mla_rows_p, mla_rows_s,
            win_p[0], win_p[1], win_p[2], win_s[0], win_s[1], win_s[2],
            jnp.stack(mem_out), jnp.stack(conv_p_out), jnp.stack(conv_s_out))
```

```python
import functools
import math

import jax
import jax.numpy as jnp
import numpy as np
from jax import lax
from jax.experimental import pallas as pl
from jax.experimental.pallas import tpu as pltpu

F32 = jnp.float32
BF16 = jnp.bfloat16

EPS = 1e-6
LANES = 128
VMEM_LIMIT = 48 * 1024 * 1024

MLA_HEADS = 12
NOPE_DIM = 64
ROPE_DIM = 32
MLA_V_DIM = 64
Q_LORA = 256
KV_LORA = 256
MLA_ROW = KV_LORA + ROPE_DIM
ROPE_THETA = 10000.0
MLA_SCALE = 1.0 / math.sqrt(NOPE_DIM + ROPE_DIM)
MLA_QCAT = 384
MLA_PAGES_PER_STEP = 128
MLA_PAGE_CHUNK = 16
DIL_SAMPLE_STEP_TOKENS = 1024
FLASH_HEAD_GROUP = 4

DIL_WINDOWS = (128, 512, 2048)
DIL_RATES = (1, 4, 16)
N_GROUPS = 3
DIL_HEADS = 4
DIL_QK = 64
DIL_V = 192
DIL_KEYS = DIL_WINDOWS[0] // DIL_RATES[0] + 1
DIL_SCALE = 1.0 / math.sqrt(DIL_QK)
DIL_ROW = DIL_QK + DIL_V
DIL_SLAB = DIL_HEADS * DIL_ROW

MEM_HEADS = 4
MEM_DIM = 64
MEM_SCALE = 1.0 / math.sqrt(MEM_DIM)
MEM_W = MEM_HEADS * LANES

N_BUCKETS = 32
MAX_DISTANCE = 2048
NEG = -1e30


def _cparams(*sem):
    return pltpu.CompilerParams(dimension_semantics=sem, vmem_limit_bytes=VMEM_LIMIT)


def _dot(a, b):
    return jnp.dot(a, b, preferred_element_type=F32)


def _dot_nt(a, b):
    return lax.dot_general(a, b, (((1,), (1,)), ((), ())), preferred_element_type=F32)


def _rms_rows(x, g):
    return x * lax.rsqrt(jnp.mean(x * x, axis=-1, keepdims=True) + EPS) * g


def _seg_scale(x, s_ref, st2_ref, inv_ref):
    ss = _dot((x * x).astype(BF16), s_ref[...])
    r = lax.rsqrt(ss * inv_ref[...] + EPS)
    hi = r.astype(BF16)
    lo = (r - hi.astype(F32)).astype(BF16)
    return _dot(jnp.concatenate([hi, lo], axis=1), st2_ref[...])


def _cols(vals, width):
    rows = vals[0].shape[0]
    lane = lax.broadcasted_iota(jnp.int32, (rows, width), 1)
    out = jnp.zeros((rows, width), F32)
    for k, v in enumerate(vals):
        out = jnp.where(lane == k, v, out)
    return out


def _full(shape):
    return pl.BlockSpec(shape, lambda *_: (0,) * len(shape))


def _proj_mla_kernel(prompt, x_ref, ga_ref, win_ref, gqa_ref, wq_ref, wqs_ref, gkv_ref,
                     cq_ref, sq_ref, tk_ref, s_ref, st_ref, inv_ref,
                     sm_ref, smt_ref, invm_ref, gm_ref, *rest):
    if prompt:
        wkc_ref, wkp_ref, wv_ref, vone_ref, q_out, rows_out, mq_out, k_out, v_out = rest
    else:
        wcat_ref, q_out, rows_out, mq_out = rest
    h = _rms_rows(x_ref[...], ga_ref[...]).astype(BF16)
    z = _dot(h, win_ref[...])
    qa = _rms_rows(z[:, :Q_LORA], gqa_ref[...]).astype(BF16)
    q = _dot(qa, wq_ref[...])
    qs = _dot(qa, wqs_ref[...])
    scale = _seg_scale(q, s_ref, st_ref, inv_ref)
    cq = jnp.concatenate([cq_ref[...]] * MLA_HEADS, axis=1)
    sq = jnp.concatenate([sq_ref[...]] * MLA_HEADS, axis=1)
    qf = ((q * cq + qs * sq) * scale).astype(BF16)
    c = _rms_rows(z[:, Q_LORA:Q_LORA + KV_LORA], gkv_ref[...])
    zk = z[:, 1024:1152]
    kp = zk[:, :ROPE_DIM]
    rk = lax.rsqrt(jnp.mean(kp * kp, axis=-1, keepdims=True) + EPS)
    t = zk * tk_ref[...] * rk
    kr = t + pltpu.roll(t, LANES - ROPE_DIM, 1)
    rows_out[:, :KV_LORA] = c
    rows_out[:, KV_LORA:] = kr[:, :ROPE_DIM]
    mq = z[:, 512:1024]
    mscale = _seg_scale(mq, sm_ref, smt_ref, invm_ref)
    mq_out[...] = (mq * mscale * gm_ref[...]).astype(BF16)
    if prompt:
        q_out[...] = qf
        cb = c.astype(BF16)
        k_out[...] = (_dot(cb, wkc_ref[...]) + _dot(kr.astype(BF16), wkp_ref[...])).astype(BF16)
        v_out[...] = (_dot(cb, wv_ref[...]) + vone_ref[...]).astype(BF16)
    else:
        for hd in range(MLA_HEADS):
            q_out[:, hd * MLA_QCAT:(hd + 1) * MLA_QCAT] = _dot(
                qf[:, hd * LANES:(hd + 1) * LANES], wcat_ref[hd]).astype(BF16)


def _proj_mla(x, prompt, tm, w, cq, sq, tk):
    m, d = x.shape
    hw = MLA_HEADS * LANES
    row = lambda width: pl.BlockSpec((tm, width), lambda i: (i, 0))
    args = [x, w['g_attn'], w['w_in'], w['g_q_a'], w['wq'], w['wq_sw'], w['g_kv_a'],
            cq, sq, tk, w['S'], w['ST'], w['inv'], w['Sm'], w['SmT'], w['invm'], w['gm']]
    ptiles = cq.shape[0] // tm
    pos = pl.BlockSpec((tm, LANES), lambda i: (i % ptiles, 0))
    specs = [row(d)] + [_full(a.shape) for a in args[1:7]] + [pos, pos, pos] \
        + [_full(a.shape) for a in args[10:]]
    if prompt:
        extra = [w['wk_c'], w['wk_p'], w['wv'], w['vone']]
        outs = [jax.ShapeDtypeStruct((m, hw), BF16), jax.ShapeDtypeStruct((m, MLA_ROW), F32),
                jax.ShapeDtypeStruct((m, MEM_W), BF16), jax.ShapeDtypeStruct((m, hw), BF16),
                jax.ShapeDtypeStruct((m, hw), BF16)]
        ospecs = [row(hw), row(MLA_ROW), row(MEM_W), row(hw), row(hw)]
    else:
        extra = [w['wcat']]
        outs = [jax.ShapeDtypeStruct((m, MLA_HEADS * MLA_QCAT), BF16),
                jax.ShapeDtypeStruct((m, MLA_ROW), F32), jax.ShapeDtypeStruct((m, MEM_W), BF16)]
        ospecs = [row(MLA_HEADS * MLA_QCAT), row(MLA_ROW), row(MEM_W)]
    return pl.pallas_call(
        functools.partial(_proj_mla_kernel, prompt),
        grid=(m // tm,),
        in_specs=specs + [_full(a.shape) for a in extra],
        out_specs=ospecs, out_shape=outs,
        compiler_params=_cparams("parallel"),
        name="proj_mla_prompt" if prompt else "proj_mla_sample",
    )(*args, *extra)


def _flash_kernel(tq, qi_ref, kj_ref, q_ref, k_ref, v_ref, o_ref, m_scr, acc_scr):
    i, j = qi_ref[pl.program_id(1)], kj_ref[pl.program_id(1)]

    @pl.when(j == 0)
    def _():
        m_scr[...] = jnp.full(m_scr.shape, NEG, F32)
        acc_scr[...] = jnp.zeros(acc_scr.shape, F32)

    def step(masked):
        if masked:
            keep = lax.broadcasted_iota(jnp.int32, (tq, tq), 1) <= lax.broadcasted_iota(jnp.int32, (tq, tq), 0)
        for h0 in range(0, MLA_HEADS, FLASH_HEAD_GROUP):
            heads = range(h0, h0 + FLASH_HEAD_GROUP)
            scores = [_dot_nt(q_ref[0, :, hd * LANES:(hd + 1) * LANES], k_ref[0, :, hd * LANES:(hd + 1) * LANES])
                      for hd in heads]
            for hd, s in zip(heads, scores):
                if masked:
                    s = jnp.where(keep, s, NEG)
                m_old = m_scr[hd]
                m_new = jnp.maximum(m_old, jnp.max(s, axis=-1, keepdims=True))
                alpha = jnp.exp(m_old - m_new)
                p = jnp.exp(s - jnp.concatenate([m_new] * (tq // LANES), axis=1))
                acc_scr[hd] = alpha * acc_scr[hd] + _dot(p.astype(BF16), v_ref[0, :, hd * LANES:(hd + 1) * LANES])
                m_scr[hd] = m_new

    @pl.when(j < i)
    def _():
        step(False)

    @pl.when(j == i)
    def _():
        step(True)
        low = lax.broadcasted_iota(jnp.int32, (tq, LANES), 1) < MLA_V_DIM
        for pr in range(MLA_HEADS // 2):
            ae, ao = acc_scr[2 * pr], acc_scr[2 * pr + 1]
            o_ref[0, :, pr * LANES:(pr + 1) * LANES] = jnp.where(
                low, ae / ae[:, MLA_V_DIM:MLA_V_DIM + 1], ao / ao[:, 0:1]).astype(BF16)


def _mla_flash(q, k, v, tq):
    n, t, hw = q.shape
    nq = t // tq
    pairs = [(i, j) for i in range(nq) for j in range(i + 1)]
    qi = jnp.asarray([p[0] for p in pairs], jnp.int32)
    kj = jnp.asarray([p[1] for p in pairs], jnp.int32)
    return pl.pallas_call(
        functools.partial(_flash_kernel, tq),
        grid_spec=pltpu.PrefetchScalarGridSpec(
            num_scalar_prefetch=2,
            grid=(n, len(pairs)),
            in_specs=[pl.BlockSpec((1, tq, hw), lambda b, s, qi, kj: (b, qi[s], 0)),
                      pl.BlockSpec((1, tq, hw), lambda b, s, qi, kj: (b, kj[s], 0)),
                      pl.BlockSpec((1, tq, hw), lambda b, s, qi, kj: (b, kj[s], 0))],
            out_specs=pl.BlockSpec((1, tq, MLA_HEADS * MLA_V_DIM), lambda b, s, qi, kj: (b, qi[s], 0)),
            scratch_shapes=[pltpu.VMEM((MLA_HEADS, tq, LANES), F32), pltpu.VMEM((MLA_HEADS, tq, LANES), F32)]),
        out_shape=jax.ShapeDtypeStruct((n, t, MLA_HEADS * MLA_V_DIM), BF16),
        compiler_params=_cparams("parallel", "arbitrary"),
        name="mla_flash_prompt",
    )(qi, kj, q, k, v)


def _mla_sample_kernel(npg, t_s, pt_ref, q_ref, new_ref, *rest):
    pages, (o_ref, m_scr, l_scr, acc_scr) = rest[:npg], rest[npg:]
    c = pl.program_id(1)

    @pl.when(c == 0)
    def _():
        m_scr[...] = jnp.full(m_scr.shape, NEG, F32)
        l_scr[...] = jnp.zeros(l_scr.shape, F32)
        acc_scr[...] = jnp.zeros(acc_scr.shape, F32)

    q = q_ref[0][:, :MLA_ROW]

    def update(state, s, pv):
        m_old, l_old, acc = state
        m_new = jnp.maximum(m_old, jnp.max(s, axis=-1, keepdims=True))
        alpha = jnp.exp(m_old - m_new)
        p = jnp.exp(s - m_new[:, 0:1])
        l_new = alpha * l_old + jnp.sum(p, axis=-1, keepdims=True)
        acc = jnp.concatenate([alpha] * (KV_LORA // LANES), axis=1) * acc + pv(p.astype(BF16))
        return m_new, l_new, acc

    def load(c0):
        return jnp.concatenate([pg[0].astype(BF16) for pg in pages[c0:c0 + MLA_PAGE_CHUNK]], axis=1)

    starts = list(range(0, npg, MLA_PAGE_CHUNK))
    state = (m_scr[...], l_scr[...], acc_scr[...])
    kts = {0: load(starts[0])}
    scores = {0: _dot(q, kts[0])}
    if len(starts) > 1:
        kts[1] = load(starts[1])
    for ch in range(len(starts)):
        if ch + 1 < len(starts):
            scores[ch + 1] = _dot(q, kts[ch + 1])
        if ch + 2 < len(starts):
            kts[ch + 2] = load(starts[ch + 2])
        state = update(state, scores.pop(ch), lambda p, kt=kts.pop(ch): _dot_nt(p, kt[:KV_LORA]))
    m_scr[...], l_scr[...], acc_scr[...] = state

    @pl.when(c == pl.num_programs(1) - 1)
    def _():
        nb = new_ref[0].astype(BF16)
        s = _dot_nt(q, nb)
        rows = q.shape[0]
        tq = lax.broadcasted_iota(jnp.int32, (rows, t_s), 0) % t_s
        s = jnp.where(lax.broadcasted_iota(jnp.int32, (rows, t_s), 1) <= tq, s, NEG)
        _, l_fin, acc = update((m_scr[...], l_scr[...], acc_scr[...]), s, lambda p: _dot(p, nb[:, :KV_LORA]))
        o_ref[0] = (acc / jnp.concatenate([l_fin] * (KV_LORA // LANES), axis=1)).astype(BF16)


def _mla_sample_attn(q3, rows3, pool_t, page_table, npg):
    b, rows, _ = q3.shape
    t_s = rows3.shape[1]
    n_pages = page_table.shape[1]
    psz = pool_t.shape[2]

    def page_spec(k):
        return pl.BlockSpec((1, MLA_ROW, psz), lambda s, c, pt: (pt[s * n_pages + c * npg + k], 0, 0))

    return pl.pallas_call(
        functools.partial(_mla_sample_kernel, npg, t_s),
        grid_spec=pltpu.PrefetchScalarGridSpec(
            num_scalar_prefetch=1,
            grid=(b, n_pages // npg),
            in_specs=[pl.BlockSpec((1, rows, MLA_QCAT), lambda s, c, pt: (s, 0, 0)),
                      pl.BlockSpec((1, t_s, MLA_ROW), lambda s, c, pt: (s, 0, 0))]
            + [page_spec(k) for k in range(npg)],
            out_specs=pl.BlockSpec((1, rows, KV_LORA), lambda s, c, pt: (s, 0, 0)),
            scratch_shapes=[pltpu.VMEM((rows, LANES), F32), pltpu.VMEM((rows, LANES), F32),
                            pltpu.VMEM((rows, KV_LORA), F32)]),
        out_shape=jax.ShapeDtypeStruct((b, rows, KV_LORA), BF16),
        compiler_params=_cparams("parallel", "arbitrary"),
        name="mla_sample_attn",
    )(page_table.reshape(-1), q3, rows3, *([pool_t] * npg))


def _uv_kernel(ctx_ref, w_ref, o_ref):
    for pr in range(MLA_HEADS // 2):
        o_ref[:, pr * LANES:(pr + 1) * LANES] = _dot(
            ctx_ref[:, pr * 2 * KV_LORA:(pr + 1) * 2 * KV_LORA], w_ref[pr]).astype(BF16)


def _mla_uv(ctx, w_pairs, tm):
    m = ctx.shape[0]
    return pl.pallas_call(
        _uv_kernel, grid=(m // tm,),
        in_specs=[pl.BlockSpec((tm, ctx.shape[1]), lambda i: (i, 0)), _full(w_pairs.shape)],
        out_specs=pl.BlockSpec((tm, MLA_HEADS * MLA_V_DIM), lambda i: (i, 0)),
        out_shape=jax.ShapeDtypeStruct((m, MLA_HEADS * MLA_V_DIM), BF16),
        compiler_params=_cparams("parallel"), name="mla_sample_uv",
    )(ctx, w_pairs)


def _mem_kv_kernel(x_ref, g_ref, w_ref, s_ref, st_ref, inv_ref, gk_ref, vm_ref, o_ref):
    h = _rms_rows(x_ref[...], g_ref[...]).astype(BF16)
    kv = _dot(h, w_ref[...])
    scale = _seg_scale(kv, s_ref, st_ref, inv_ref) * gk_ref[...] + vm_ref[...]
    o_ref[...] = kv * scale


def _mem_kv(mem2, w, tm):
    m, d = mem2.shape
    args = [mem2, w['g_mem'], w['w_mem'], w['Sk'], w['SkT'], w['invk'], w['gk'], w['vmask']]
    return pl.pallas_call(
        _mem_kv_kernel, grid=(m // tm,),
        in_specs=[pl.BlockSpec((tm, d), lambda i: (i, 0))] + [_full(a.shape) for a in args[1:]],
        out_specs=pl.BlockSpec((tm, MEM_W), lambda i: (i, 0)),
        out_shape=jax.ShapeDtypeStruct((m, MEM_W), F32),
        compiler_params=_cparams("parallel"), name="mem_kv",
    )(*args)


def _mem_attn_body(q, kv):
    outs = []
    scores = [_dot_nt(q[:, hd * LANES:(hd + 1) * LANES], kv[:, hd * LANES:(hd + 1) * LANES])
              for hd in range(MEM_HEADS)]
    for hd, s in enumerate(scores):
        sl = slice(hd * LANES, (hd + 1) * LANES)
        p = jnp.exp(s - jnp.max(s, axis=-1, keepdims=True))
        p = p / jnp.sum(p, axis=-1, keepdims=True)
        outs.append(_dot(p.astype(BF16), kv[:, sl]))
    return jnp.concatenate(outs, axis=-1)


def _mem_attn_kernel(q_ref, kv_ref, o_ref):
    o_ref[0] = _mem_attn_body(q_ref[0], kv_ref[0].astype(BF16)).astype(BF16)


def _mem_attn(q3, kv3, tq):
    n, t, _ = q3.shape
    return pl.pallas_call(
        _mem_attn_kernel, grid=(n, t // tq),
        in_specs=[pl.BlockSpec((1, tq, MEM_W), lambda b, i: (b, i, 0)),
                  pl.BlockSpec((1, kv3.shape[1], MEM_W), lambda b, i: (b, 0, 0))],
        out_specs=pl.BlockSpec((1, tq, MEM_W), lambda b, i: (b, i, 0)),
        out_shape=jax.ShapeDtypeStruct((n, t, MEM_W), BF16),
        compiler_params=_cparams("parallel", "parallel"), name="mem_attn",
    )(q3, kv3)


def _mem_rows_kernel(sb, q_ref, kv_ref, o_ref):
    nq, nkv = q_ref.shape[1], kv_ref.shape[2]
    same = (lax.broadcasted_iota(jnp.int32, (nq, nkv), 0) % MEM_HEADS
            == lax.broadcasted_iota(jnp.int32, (nq, nkv), 1) % MEM_HEADS)
    kvs = [kv_ref[0, s].astype(BF16) for s in range(sb)]
    scores = [_dot_nt(q_ref[s], kvs[s]) for s in range(sb)]
    for s in range(sb):
        sc = jnp.where(same, scores[s], NEG)
        p = jnp.exp(sc - jnp.max(sc, axis=-1, keepdims=True))
        p = p / jnp.sum(p, axis=-1, keepdims=True)
        o_ref[s] = _dot(p.astype(BF16), kvs[s]).astype(BF16)


def _mem_attn_rows(q3, cache4, layer, sb):
    b, nq, _ = q3.shape
    return pl.pallas_call(
        functools.partial(_mem_rows_kernel, sb), grid=(b // sb,),
        in_specs=[pl.BlockSpec((sb, nq, LANES), lambda i: (i, 0, 0)),
                  pl.BlockSpec((1, sb, cache4.shape[2], LANES), lambda i: (layer, i, 0, 0))],
        out_specs=pl.BlockSpec((sb, nq, LANES), lambda i: (i, 0, 0)),
        out_shape=jax.ShapeDtypeStruct((b, nq, LANES), BF16),
        compiler_params=_cparams("parallel"), name="mem_attn_sample",
    )(q3, cache4)


def _out_proj_kernel(x_ref, a_ref, b_ref, wa_ref, wb_ref, o_ref):
    o_ref[...] = x_ref[...] + _dot(a_ref[...], wa_ref[...]) + _dot(b_ref[...], wb_ref[...])


def _out_proj(x, a, b, wa, wb, tm):
    m, d = x.shape
    row = lambda width: pl.BlockSpec((tm, width), lambda i: (i, 0))
    return pl.pallas_call(
        _out_proj_kernel, grid=(m // tm,),
        in_specs=[row(d), row(a.shape[1]), row(b.shape[1]), _full(wa.shape), _full(wb.shape)],
        out_specs=row(d), out_shape=jax.ShapeDtypeStruct((m, d), F32),
        compiler_params=_cparams("parallel"), name="out_proj",
    )(x, a, b, wa, wb)


def _ffn_kernel(seq_tiles, t_s, d_ff, x_ref, g_ref, wu_ref, cw_ref, cb_ref, wd_ref, *rest):
    if seq_tiles:
        o_ref, st_ref, g_scr = rest
    else:
        p1_ref, p2_ref, o_ref, st_ref = rest
    x = x_ref[...]
    tm = x.shape[0]
    h = _rms_rows(x, g_ref[...]).astype(BF16)
    u = _dot(h, wu_ref[...])
    gate, val = u[:, :d_ff], u[:, d_ff:]
    if seq_tiles:
        @pl.when(pl.program_id(0) % seq_tiles == 0)
        def _():
            g_scr[0:8, :] = jnp.zeros((8, d_ff), F32)
        g_scr[8:, :] = gate
        g1 = g_scr[7:tm + 7, :]
        g2 = g_scr[6:tm + 6, :]
        tail = gate[tm - 8:, :]
        g_scr[0:8, :] = tail
        st_ref[...] = tail
    else:
        tpos = lax.broadcasted_iota(jnp.int32, (tm, 1), 0) % t_s
        g1 = jnp.where(tpos >= 1, pltpu.roll(gate, 1, 0), p1_ref[...])
        g2 = jnp.where(tpos >= 2, pltpu.roll(gate, 2, 0), p2_ref[...])
        st_ref[...] = gate
    cw = cw_ref[...]
    conv = cb_ref[...] + g2 * cw[0:1] + g1 * cw[1:2] + gate * cw[2:3]
    act = (conv * (1.0 / (1.0 + jnp.exp(-conv))) * val).astype(BF16)
    o_ref[...] = x + _dot(act, wd_ref[...])


def _ffn(x, seq_len, w, tm, prev=None):
    m, d = x.shape
    d_ff = w['w_down'].shape[1]
    layer = w['layer']
    row = lambda width: pl.BlockSpec((tm, width), lambda i: (i, 0))
    single = dict(pipeline_mode=pl.Buffered(1))
    wspecs = [_full(w['g_ffn'].shape),
              pl.BlockSpec((None,) + w['w_up'].shape[1:], lambda i: (layer, 0, 0), **single),
              _full(w['conv_w'].shape), _full(w['conv_b'].shape),
              pl.BlockSpec((None,) + w['w_down'].shape[1:], lambda i: (layer, 0, 0), **single)]
    wargs = [w['g_ffn'], w['w_up'], w['conv_w'], w['conv_b'], w['w_down']]
    if prev is None:
        seq_tiles = seq_len // tm
        return pl.pallas_call(
            functools.partial(_ffn_kernel, seq_tiles, 0, d_ff), grid=(m // tm,),
            in_specs=[row(d)] + wspecs,
            out_specs=[row(d), pl.BlockSpec((8, d_ff), lambda i: (i, 0))],
            out_shape=[jax.ShapeDtypeStruct((m, d), F32),
                       jax.ShapeDtypeStruct((m // tm * 8, d_ff), F32)],
            scratch_shapes=[pltpu.VMEM((tm + 8, d_ff), F32)],
            compiler_params=_cparams("arbitrary"), name="ffn_prompt",
        )(x, *wargs)
    p1, p2 = prev
    return pl.pallas_call(
        functools.partial(_ffn_kernel, 0, seq_len, d_ff), grid=(m // tm,),
        in_specs=[row(d)] + wspecs + [row(d_ff), row(d_ff)],
        out_specs=[row(d), row(d_ff)],
        out_shape=[jax.ShapeDtypeStruct((m, d), F32), jax.ShapeDtypeStruct((m, d_ff), F32)],
        compiler_params=_cparams("parallel"), name="ffn_sample",
    )(x, *wargs, p1, p2)


QW = N_GROUPS * DIL_HEADS * LANES
KVW = N_GROUPS * DIL_SLAB


def _proj_dil_kernel(prompt, win_skips, x_ref, ga_ref, win_ref, sq_ref, sqt_ref, invq_ref, gq_ref,
                     sk_ref, skt_ref, invk_ref, gk_ref, vm_ref,
                     sm_ref, smt_ref, invm_ref, gm_ref, *rest):
    x = x_ref[0] if prompt else x_ref[...]
    h = _rms_rows(x, ga_ref[...]).astype(BF16)
    z = _dot(h, win_ref[...])
    q = z[:, :QW]
    qn = q * _seg_scale(q, sq_ref, sqt_ref, invq_ref) * gq_ref[...]
    kv = z[:, QW:QW + KVW]
    kvn = kv * (_seg_scale(kv, sk_ref, skt_ref, invk_ref) * gk_ref[...] + vm_ref[...])
    mq = z[:, QW + KVW:]
    mqn = (mq * _seg_scale(mq, sm_ref, smt_ref, invm_ref) * gm_ref[...]).astype(BF16)
    if not prompt:
        q_out, kv_out, mq_out = rest
        q_out[...] = qn.astype(BF16)
        kv_out[...] = kvn
        mq_out[...] = mqn
        return
    mq_out, win_outs = rest[0], rest[1:4]
    q_outs, kvb_outs, q_scr, kv_scr = rest[4:7], rest[7:10], rest[10], rest[11]
    mq_out[0] = mqn
    for c in range(QW // LANES):
        q_scr[c] = qn[:, c * LANES:(c + 1) * LANES]
    for c in range(KVW // LANES):
        kv_scr[c] = kvn[:, c * LANES:(c + 1) * LANES]
    tm = q.shape[0]
    qc, kc = DIL_HEADS, DIL_SLAB // LANES
    for g in range(N_GROUPS):
        rows = win_outs[g].shape[1]

        @pl.when(pl.program_id(1) >= win_skips[g])
        def _(g=g, rows=rows):
            for s in range(8):
                chunk = g * kc + (s % DIL_HEADS) * (DIL_ROW // LANES) + s // DIL_HEADS
                win_outs[g][0, :, s, :] = kv_scr[chunk, tm - rows:tm, :]
    for g, d in enumerate(DIL_RATES):
        for r in range(d):
            rows = pl.ds(r, tm // d, stride=d) if d > 1 else slice(None)
            for c in range(qc):
                q_outs[g][0, r, :, c * LANES:(c + 1) * LANES] = q_scr[g * qc + c, rows, :].astype(BF16)
            for c in range(kc):
                kvb_outs[g][0, r, :, c * LANES:(c + 1) * LANES] = kv_scr[g * kc + c, rows, :].astype(BF16)


def _proj_dil(x3, prompt, tm, w):
    n, t, d = x3.shape
    args = [x3, w['g_attn'], w['w_in'], w['Sq'], w['SqT'], w['invq'], w['gq'],
            w['Sk'], w['SkT'], w['invk'], w['gk'], w['vmask'], w['Sm'], w['SmT'], w['invm'], w['gm']]
    if not prompt:
        x2 = x3.reshape(n * t, d)
        m = n * t
        row = lambda width: pl.BlockSpec((tm, width), lambda i: (i, 0))
        return pl.pallas_call(
            functools.partial(_proj_dil_kernel, False, None), grid=(m // tm,),
            in_specs=[row(d)] + [_full(a.shape) for a in args[1:]],
            out_specs=[row(QW), row(KVW), row(MEM_W)],
            out_shape=[jax.ShapeDtypeStruct((m, QW), BF16), jax.ShapeDtypeStruct((m, KVW), F32),
                       jax.ShapeDtypeStruct((m, MEM_W), BF16)],
            compiler_params=_cparams("parallel"), name="proj_dil_sample",
        )(x2, *args[1:])
    row3 = lambda width: pl.BlockSpec((1, tm, width), lambda b, i: (b, i, 0))
    outs = [jax.ShapeDtypeStruct((n, t, MEM_W), BF16)]
    ospecs = [row3(MEM_W)]
    skips = []
    for wdw in DIL_WINDOWS:
        tail = min(wdw, t)
        rows = min(tail, tm)
        skip = (t - tail) // tm
        skips.append(skip)
        outs.append(jax.ShapeDtypeStruct((n, tail, 8, LANES), F32))
        ospecs.append(pl.BlockSpec((1, rows, 8, LANES),
                                   lambda b, i, skip=skip, last=tail // rows - 1:
                                   (b, jnp.clip(i - skip, 0, last), 0, 0)))
    for width in (DIL_HEADS * LANES, DIL_SLAB):
        for dil in DIL_RATES:
            outs.append(jax.ShapeDtypeStruct((n, dil, t // dil, width), BF16))
            ospecs.append(pl.BlockSpec((1, dil, tm // dil, width), lambda b, i: (b, 0, i, 0)))
    return pl.pallas_call(
        functools.partial(_proj_dil_kernel, True, tuple(skips)), grid=(n, t // tm),
        in_specs=[row3(d)] + [_full(a.shape) for a in args[1:]],
        out_specs=ospecs, out_shape=outs,
        scratch_shapes=[pltpu.VMEM((QW // LANES, tm, LANES), F32), pltpu.VMEM((KVW // LANES, tm, LANES), F32)],
        compiler_params=_cparams("parallel", "arbitrary"), name="proj_dil_prompt",
    )(*args)


def _band_kernel(tq, nqb, q_ref, kp_ref, kc_ref, bias_ref, o_ref, ml_ref):
    i = pl.program_id(1)
    kcat = jnp.concatenate([kp_ref[0], kc_ref[0]], axis=0)
    first = jnp.logical_and(i == 0, lax.broadcasted_iota(jnp.int32, (tq, 2 * tq), 1) < tq)
    work = [(j, hd) for j in range(nqb) for hd in range(DIL_HEADS)]
    scores = [_dot_nt(q_ref[0, j * tq:(j + 1) * tq, hd * LANES:(hd + 1) * LANES],
                      kcat[j * tq:(j + 2) * tq, hd * DIL_ROW:hd * DIL_ROW + LANES]) for j, hd in work]
    stats = [[] for _ in range(nqb)]
    for (j, hd), s in zip(work, scores):
        s = s + bias_ref[hd]
        if j == 0:
            s = jnp.where(first, NEG, s)
        m = jnp.max(s, axis=-1, keepdims=True)
        p = jnp.exp(s - m)
        l = jnp.sum(p, axis=-1, keepdims=True)
        o_ref[0, j * tq:(j + 1) * tq, hd * DIL_ROW:(hd + 1) * DIL_ROW] = _dot(
            (p / l).astype(BF16), kcat[j * tq:(j + 2) * tq, hd * DIL_ROW:(hd + 1) * DIL_ROW]).astype(BF16)
        stats[j].append((m, l))
    for j in range(nqb):
        ml_ref[0, j * tq:(j + 1) * tq] = _cols([m for m, _ in stats[j]] + [l for _, l in stats[j]], LANES)


def _band_attn(q, kv, bias, tq, nqb):
    s, l, _ = q.shape
    tb = tq * nqb
    return pl.pallas_call(
        functools.partial(_band_kernel, tq, nqb), grid=(s, l // tb),
        in_specs=[pl.BlockSpec((1, tb, DIL_HEADS * LANES), lambda b, i: (b, i, 0)),
                  pl.BlockSpec((1, tq, DIL_SLAB), lambda b, i: (b, jnp.maximum(i * nqb - 1, 0), 0)),
                  pl.BlockSpec((1, tb, DIL_SLAB), lambda b, i: (b, i, 0)),
                  _full(bias.shape)],
        out_specs=[pl.BlockSpec((1, tb, DIL_SLAB), lambda b, i: (b, i, 0)),
                   pl.BlockSpec((1, tb, LANES), lambda b, i: (b, i, 0))],
        out_shape=[jax.ShapeDtypeStruct((s, l, DIL_SLAB), BF16),
                   jax.ShapeDtypeStruct((s, l, LANES), F32)],
        compiler_params=_cparams("parallel", "parallel"), name="dil_band_attn",
    )(q, kv, kv, bias)


def _merge_kernel(rates, *refs):
    ng = len(rates)
    o_refs, ml_refs, out_ref = refs[:ng], refs[ng:2 * ng], refs[2 * ng]
    scr = refs[2 * ng + 1:]
    nch = DIL_SLAB // LANES
    chunks, mls = [], []
    for g, d in enumerate(rates):
        if d == 1:
            chunks.append([o_refs[g][0, 0, :, c * LANES:(c + 1) * LANES].astype(F32) for c in range(nch)])
            mls.append(ml_refs[g][0, 0])
        else:
            o_scr, ml_scr = scr[2 * g], scr[2 * g + 1]
            tm = ml_scr.shape[0]
            for r in range(d):
                rows = pl.ds(r, tm // d, stride=d)
                for c in range(nch):
                    o_scr[c, rows, :] = o_refs[g][0, r, :, c * LANES:(c + 1) * LANES].astype(F32)
                ml_scr[rows, :] = ml_refs[g][0, r]
            chunks.append([o_scr[c] for c in range(nch)])
            mls.append(ml_scr[...])
    m_all = functools.reduce(jnp.maximum, [ml[:, :DIL_HEADS] for ml in mls])
    es = [ml[:, DIL_HEADS:2 * DIL_HEADS] * jnp.exp(ml[:, :DIL_HEADS] - m_all) for ml in mls]
    tot = functools.reduce(jnp.add, es)
    ws = [e / tot for e in es]
    for c in range(nch):
        hd = c // (DIL_ROW // LANES)
        acc = None
        for g in range(ng):
            term = chunks[g][c] * ws[g][:, hd:hd + 1]
            acc = term if acc is None else acc + term
        out_ref[0, :, c * LANES:(c + 1) * LANES] = acc.astype(BF16)


def _merge(os_, mls, rates, n, t, tm):
    specs = [pl.BlockSpec((1, d, tm // d, DIL_SLAB), lambda b, i: (b, 0, i, 0)) for d in rates] \
        + [pl.BlockSpec((1, d, tm // d, LANES), lambda b, i: (b, 0, i, 0)) for d in rates]
    scratch = []
    for d in rates:
        scratch += [pltpu.VMEM((DIL_SLAB // LANES, tm, LANES), F32), pltpu.VMEM((tm, LANES), F32)]
    return pl.pallas_call(
        functools.partial(_merge_kernel, tuple(rates)), grid=(n, t // tm),
        in_specs=specs,
        out_specs=pl.BlockSpec((1, tm, DIL_SLAB), lambda b, i: (b, i, 0)),
        out_shape=jax.ShapeDtypeStruct((n, t, DIL_SLAB), BF16),
        scratch_shapes=scratch,
        compiler_params=_cparams("parallel", "parallel"), name="dil_merge",
    )(*os_, *mls)


def _dil_sample_kernel(d, wb, t_s, sb, q_ref, new_ref, buf_ref, bias_ref, o_ref, ml_ref, st_ref, *scr):
    nk = DIL_KEYS - 1
    news, srcs = [], []
    for sq in range(sb):
        new = new_ref[sq]
        st_ref[sq, 0:wb - t_s] = buf_ref[sq, t_s:wb]
        st_ref[sq, wb - t_s:wb] = new
        if d < t_s:
            src = scr[0].at[sq]
            src[0:wb] = buf_ref[sq]
            src[wb:wb + t_s] = new
        else:
            src = buf_ref.at[sq]
        news.append(new)
        srcs.append(src)
    top = lax.broadcasted_iota(jnp.int32, (8, LANES), 0) < DIL_HEADS
    lane = lax.broadcasted_iota(jnp.int32, (8, LANES), 1)
    bias = bias_ref[...]
    nq, ntok = _dil_classes(d, t_s)

    def both_halves(col):
        x = jnp.where(top, jnp.broadcast_to(col, (8, LANES)), 0.0)
        return x + pltpu.roll(x, DIL_HEADS, 0)

    work = [(sq, r) for sq in range(sb) for r in range(t_s // nq)]
    toks, scores, stats, outs = [], [], [], []
    for sq, r in work:
        src = srcs[sq]
        if d < t_s:
            a3 = src[pl.ds(r, ntok, stride=d)] if d > 1 else src[pl.ds(r, ntok)]
        else:
            pad = jnp.zeros((ntok - nk - 1, 8, LANES), F32)
            a3 = jnp.concatenate([src[pl.ds(r, nk, stride=d)], news[sq][r:r + 1], pad], axis=0)
        a = a3.reshape(ntok * 8, LANES).astype(BF16)
        qc = q_ref[sq, pl.ds(r, nq, stride=d)] if nq > 1 else q_ref[sq, r:r + 1]
        qc = qc.reshape(nq * 8, LANES).astype(BF16)
        toks.append(a)
        scores.append(_dot_nt(qc, a) + bias)
    for w in range(len(work)):
        s = scores[w]
        m = jnp.max(s, axis=-1, keepdims=True)
        p = jnp.exp(s - m)
        l = jnp.sum(p, axis=-1, keepdims=True)
        pn = p / l
        stats.append((m, l))
        p2 = jnp.concatenate([pn, pltpu.roll(pn, DIL_HEADS, 1)], axis=0).astype(BF16)
        outs.append(_dot(p2, toks[w]))
    for w, (sq, r) in enumerate(work):
        o2, (m, l) = outs[w], stats[w]
        for k in range(nq):
            i = r + k * d
            lo, hi = o2[8 * k:8 * k + 8], o2[8 * (nq + k):8 * (nq + k) + 8]
            o_ref[sq, i] = jnp.where(top, lo, pltpu.roll(hi, DIL_HEADS, 0))
            ml_ref[sq, i] = jnp.where(lane == 0, both_halves(m[8 * k:8 * k + 8]),
                                      jnp.where(lane == 1, both_halves(l[8 * k:8 * k + 8]), 0.0))


def _dil_classes(d, t_s):
    nq = max(t_s // d, 1)
    ntok = DIL_KEYS - 1 + nq
    return nq, ntok + ntok % 2


def _dil_sample(qg, newg, buf, bias, d):
    b, t_s = qg.shape[:2]
    wb = buf.shape[1]
    sb = max(1, min(b, DIL_SAMPLE_STEP_TOKENS // wb))
    blk = lambda r: pl.BlockSpec((sb, r, 8, LANES), lambda s: (s, 0, 0, 0))
    tiles = jax.ShapeDtypeStruct((b, t_s, 8, LANES), F32)
    return pl.pallas_call(
        functools.partial(_dil_sample_kernel, d, wb, t_s, sb), grid=(b // sb,),
        in_specs=[blk(t_s), blk(t_s), blk(wb), _full(bias.shape)],
        out_specs=[blk(t_s), blk(t_s), blk(wb)],
        out_shape=[tiles, tiles, jax.ShapeDtypeStruct(buf.shape, F32)],
        scratch_shapes=[pltpu.VMEM((sb, wb + t_s, 8, LANES), F32)] if d < t_s else [],
        compiler_params=_cparams("parallel"), name="dil_sample_attn",
    )(qg, newg, buf, bias)


def _merge_tiles_kernel(ng, *refs):
    o_refs, ml_refs, out_ref = refs[:ng], refs[ng:2 * ng], refs[2 * ng]
    ms = [r[...][:, :, 0:1] for r in ml_refs]
    ls = [r[...][:, :, 1:2] for r in ml_refs]
    m_all = functools.reduce(jnp.maximum, ms)
    es = [l * jnp.exp(m - m_all) for m, l in zip(ms, ls)]
    tot = functools.reduce(jnp.add, es)
    out_ref[...] = functools.reduce(jnp.add, [o[...] * (e / tot) for o, e in zip(o_refs, es)])


def _merge_tiles(os_, mls, tr):
    r = os_[0].shape[0]
    spec = pl.BlockSpec((tr, 8, LANES), lambda i: (i, 0, 0))
    return pl.pallas_call(
        functools.partial(_merge_tiles_kernel, len(os_)), grid=(r // tr,),
        in_specs=[spec] * (2 * len(os_)), out_specs=spec,
        out_shape=jax.ShapeDtypeStruct((r, 8, LANES), F32),
        compiler_params=_cparams("parallel"), name="dil_merge_sample",
    )(*os_, *mls)


def _seg_mats(width, segments):
    s = np.zeros((width, LANES), np.float32)
    inv = np.ones((1, LANES), np.float32)
    for k, (a, n) in enumerate(segments):
        s[a:a + n, k] = 1.0
        inv[0, k] = 1.0 / n
    return jnp.asarray(s, BF16), jnp.asarray(np.concatenate([s.T, s.T]), BF16), jnp.asarray(inv)


def _pad_heads(wm, heads, dim, slab, offset=0):
    r = wm.shape[0]
    out = jnp.pad(wm.reshape(r, heads, dim), ((0, 0), (0, 0), (offset, slab - dim - offset)))
    return out.reshape(r, heads * slab)


def _mem_q_tables(g_qn_mem):
    seg = [(hd * LANES, MEM_DIM) for hd in range(MEM_HEADS)]
    sm, smt, invm = _seg_mats(MEM_W, seg)
    gm = _pad_heads(jnp.tile(g_qn_mem, MEM_HEADS)[None, :], MEM_HEADS, MEM_DIM, LANES) * MEM_SCALE
    return dict(Sm=sm, SmT=smt, invm=invm, gm=gm)


def _mla_tables(pos, g_qn_nope, g_qn_pe, g_kn_pe):
    hr = ROPE_DIM // 2
    inv = ROPE_THETA ** (-jnp.arange(0, ROPE_DIM, 2, dtype=F32) / ROPE_DIM)
    z = lambda k: jnp.zeros((k,), F32)
    rest = LANES - NOPE_DIM - ROPE_DIM
    fq = jnp.concatenate([z(NOPE_DIM), inv, inv, z(rest)])
    cq_amp = jnp.concatenate([g_qn_nope, g_qn_pe, z(rest)]) * MLA_SCALE
    sq_amp = jnp.concatenate([z(NOPE_DIM), -g_qn_pe[hr:], g_qn_pe[:hr], z(rest)]) * MLA_SCALE
    fk = jnp.concatenate([inv, inv, inv, inv, z(LANES - 2 * ROPE_DIM)])
    ck_amp = jnp.concatenate([g_kn_pe, z(LANES - ROPE_DIM)])
    sk_amp = jnp.concatenate([z(ROPE_DIM), -g_kn_pe[hr:], g_kn_pe[:hr], z(LANES - 2 * ROPE_DIM)])
    p = pos.astype(F32)[:, None]
    aq, ak = p * fq[None, :], p * fk[None, :]
    return (jnp.cos(aq) * cq_amp[None, :], jnp.sin(aq) * sq_amp[None, :],
            jnp.cos(ak) * ck_amp[None, :] + jnp.sin(ak) * sk_amp[None, :])


def _prep_mla(i, j, g_attn, w_o, w_in_a, g_q_a, w_q_b, g_kv_a, w_uk, w_uv, g_qn_mem):
    d = w_in_a.shape[1]
    hr = ROPE_DIM // 2
    wi = w_in_a[j]
    o1, o2, o3 = Q_LORA, Q_LORA + KV_LORA, Q_LORA + KV_LORA + ROPE_DIM
    kpe = wi[:, o2:o3]
    w_in = jnp.concatenate([wi[:, :o2], _pad_heads(wi[:, o3:], MEM_HEADS, MEM_DIM, LANES),
                            kpe, jnp.concatenate([kpe[:, hr:], kpe[:, :hr]], axis=1),
                            jnp.zeros((d, LANES - 2 * ROPE_DIM), F32)], axis=1).astype(BF16)
    hq = NOPE_DIM + ROPE_DIM
    wq3 = w_q_b[j].reshape(Q_LORA, MLA_HEADS, hq)
    wq = _pad_heads(w_q_b[j], MLA_HEADS, hq, LANES).astype(BF16)
    sw = jnp.concatenate([wq3[:, :, NOPE_DIM + hr:], wq3[:, :, NOPE_DIM:NOPE_DIM + hr]], axis=2)
    wq_sw = _pad_heads(sw.reshape(Q_LORA, MLA_HEADS * ROPE_DIM), MLA_HEADS, ROPE_DIM, LANES, NOPE_DIM).astype(BF16)
    seg = []
    for hd in range(MLA_HEADS):
        seg += [(hd * LANES, NOPE_DIM), (hd * LANES + NOPE_DIM, ROPE_DIM)]
    s, st, inv = _seg_mats(MLA_HEADS * LANES, seg)
    wk_c = _pad_heads(w_uk[j].reshape(KV_LORA, MLA_HEADS * NOPE_DIM), MLA_HEADS, NOPE_DIM, LANES).astype(BF16)
    eye = np.zeros((LANES, MLA_HEADS, LANES), np.float32)
    for r in range(ROPE_DIM):
        eye[r, :, NOPE_DIM + r] = 1.0
    wk_p = jnp.asarray(eye.reshape(LANES, MLA_HEADS * LANES), BF16)
    wv4 = w_uv[j].reshape(KV_LORA, MLA_HEADS // 2, 2, MLA_V_DIM)
    lo = ((0, 0), (0, 0), (0, LANES - MLA_V_DIM))
    hi = ((0, 0), (0, 0), (LANES - MLA_V_DIM, 0))
    wv = jnp.stack([jnp.pad(wv4[:, :, 0], lo), jnp.pad(wv4[:, :, 1], hi)], axis=2)
    wv = wv.reshape(KV_LORA, MLA_HEADS * LANES).astype(BF16)
    vone = np.zeros((1, MLA_HEADS, LANES), np.float32)
    vone[0, 0::2, MLA_V_DIM] = 1.0
    vone[0, 1::2, 0] = 1.0
    vone = jnp.asarray(vone.reshape(1, MLA_HEADS * LANES))
    pe_rows = np.zeros((ROPE_DIM, MLA_QCAT), np.float32)
    pe_rows[np.arange(ROPE_DIM), KV_LORA + np.arange(ROPE_DIM)] = 1.0
    wcat = jnp.concatenate([
        jnp.pad(jnp.transpose(w_uk[j], (1, 2, 0)), ((0, 0), (0, 0), (0, MLA_QCAT - KV_LORA))),
        jnp.broadcast_to(jnp.asarray(pe_rows)[None], (MLA_HEADS, ROPE_DIM, MLA_QCAT)),
        jnp.zeros((MLA_HEADS, LANES - hq, MLA_QCAT), F32)], axis=1)
    wvp = jnp.transpose(wv4, (1, 2, 0, 3))
    uvp = jnp.concatenate([jnp.pad(wvp[:, 0], lo), jnp.pad(wvp[:, 1], hi)], axis=1)
    ntok = MLA_HEADS * MLA_V_DIM
    w = dict(g_attn=g_attn[i][None], w_in=w_in, g_q_a=g_q_a[j][None], wq=wq, wq_sw=wq_sw,
             g_kv_a=g_kv_a[j][None], S=s, ST=st, inv=inv, wk_c=wk_c, wk_p=wk_p, wv=wv, vone=vone,
             wcat=wcat.astype(BF16), uv_pairs=uvp.astype(BF16),
             wo_tok=w_o[i][:ntok].astype(BF16),
             wo_mem=_pad_rows(w_o[i][ntok:], MEM_HEADS, MEM_DIM, LANES, MEM_DIM).astype(BF16))
    w.update(_mem_q_tables(g_qn_mem[i]))
    return w


def _pad_rows(wm, heads, dim, slab, offset):
    c = wm.shape[1]
    out = jnp.pad(wm.reshape(heads, dim, c), ((0, 0), (offset, slab - dim - offset), (0, 0)))
    return out.reshape(heads * slab, c)


def _prep_dil(i, j, g_attn, w_o, w_in_b, g_qn_b, g_kn_b, g_qn_mem):
    wi = w_in_b[j]
    gh = N_GROUPS * DIL_HEADS
    nqk = gh * DIL_QK
    wq = _pad_heads(wi[:, :nqk], gh, DIL_QK, LANES)
    wk = wi[:, nqk:2 * nqk].reshape(-1, gh, DIL_QK)
    wv = wi[:, 2 * nqk:2 * nqk + gh * DIL_V].reshape(-1, gh, DIL_V)
    wkv = jnp.concatenate([wk, wv], axis=2).reshape(-1, gh * DIL_ROW)
    wmq = _pad_heads(wi[:, 2 * nqk + gh * DIL_V:], MEM_HEADS, MEM_DIM, LANES)
    w_in = jnp.concatenate([wq, wkv, wmq], axis=1).astype(BF16)
    sq, sqt, invq = _seg_mats(QW, [(k * LANES, DIL_QK) for k in range(gh)])
    sk, skt, invk = _seg_mats(KVW, [(k * DIL_ROW, DIL_QK) for k in range(gh)])
    gq = _pad_heads(jnp.repeat(g_qn_b[j], DIL_HEADS, axis=0).reshape(1, nqk), gh, DIL_QK, LANES) * DIL_SCALE
    gk = _pad_heads(jnp.repeat(g_kn_b[j], DIL_HEADS, axis=0).reshape(1, nqk), gh, DIL_QK, DIL_ROW)
    vmask = _pad_heads(jnp.ones((1, gh * DIL_V), F32), gh, DIL_V, DIL_ROW, DIL_QK)
    ntok = DIL_HEADS * DIL_V
    w = dict(g_attn=g_attn[i][None], w_in=w_in, Sq=sq, SqT=sqt, invq=invq, gq=gq,
             Sk=sk, SkT=skt, invk=invk, gk=gk, vmask=vmask,
             wo_tok=_pad_rows(w_o[i][:ntok], DIL_HEADS, DIL_V, DIL_ROW, DIL_QK).astype(BF16),
             wo_mem=_pad_rows(w_o[i][ntok:], MEM_HEADS, MEM_DIM, LANES, MEM_DIM).astype(BF16))
    w.update(_mem_q_tables(g_qn_mem[i]))
    return w


def _prep_mem(i, g_mem, w_mem_kv, g_kn_mem):
    sk, skt, invk = _seg_mats(MEM_W, [(hd * LANES, MEM_DIM) for hd in range(MEM_HEADS)])
    gk = _pad_heads(jnp.tile(g_kn_mem[i], MEM_HEADS)[None, :], MEM_HEADS, MEM_DIM, LANES)
    vmask = _pad_heads(jnp.ones((1, MEM_HEADS * MEM_DIM), F32), MEM_HEADS, MEM_DIM, LANES, MEM_DIM)
    return dict(g_mem=g_mem[i][None], w_mem=w_mem_kv[i].astype(BF16), Sk=sk, SkT=skt, invk=invk, gk=gk, vmask=vmask)


def _prep_ffn(i, g_ffn, w_up_b, conv_w, conv_b, w_down_b):
    return dict(layer=i, g_ffn=g_ffn[i][None], w_up=w_up_b, conv_w=conv_w[i],
                conv_b=conv_b[i][None], w_down=w_down_b)


def _rel_bucket(dist):
    max_exact = N_BUCKETS // 2
    dd = jnp.maximum(dist.astype(F32), 1.0)
    large = max_exact + (jnp.log(dd / max_exact) / math.log(MAX_DISTANCE / max_exact)
                         * (N_BUCKETS - max_exact)).astype(jnp.int32)
    large = jnp.minimum(large, N_BUCKETS - 1)
    return jnp.where(dist < max_exact, dist, large)


def _dil_bias_tables(rel_bias, tq, t_s):
    band, samp = [], []

    def toeplitz(rev, rows, cols):
        period = cols + rows + 1
        f = jnp.concatenate([rev, jnp.full((DIL_HEADS, period - DIL_KEYS), NEG, F32)], axis=1)
        return jnp.tile(f, (1, rows))[:, :rows * (period - 1)].reshape(DIL_HEADS, rows, period - 1)[:, :, :cols]

    srow = jnp.arange(8)[None, None, None, :] == jnp.arange(DIL_HEADS)[:, None, None, None]
    for g in range(N_GROUPS):
        bk = _rel_bucket(DIL_RATES[g] * jnp.arange(DIL_KEYS, dtype=jnp.int32))
        b = rel_bias[bk][:, g * DIL_HEADS:(g + 1) * DIL_HEADS].T.astype(F32)
        rev = b[:, ::-1]
        band.append(toeplitz(rev, tq, 2 * tq))
        nq, ntok = _dil_classes(DIL_RATES[g], t_s)
        tab = jnp.where(srow, toeplitz(rev, nq, ntok)[..., None], NEG).reshape(DIL_HEADS, nq, ntok * 8)
        tab = jnp.pad(tab.transpose(1, 0, 2), ((0, 0), (0, 8 - DIL_HEADS), (0, 0)), constant_values=NEG)
        samp.append(tab.reshape(nq * 8, ntok * 8))
    return band, samp


def kernel(x_prompt, x_sample, cache_mla, state_win0, state_win1, state_win2, cache_mem, state_conv, page_table, mem_prompt, rel_bias, g_attn, w_o, w_in_a, g_q_a, w_q_b, g_kv_a, w_uk, w_uv, g_qn_nope, g_qn_pe, g_kn_pe, w_in_b, g_qn_b, g_kn_b, g_mem, w_mem_kv, g_qn_mem, g_kn_mem, g_ffn, w_up, conv_w, conv_b, w_down):
    state_wins = (state_win0, state_win1, state_win2)
    n, t_p, d = x_prompt.shape
    b, t_s, _ = x_sample.shape
    n_mem = mem_prompt.shape[1]
    d_ff = w_down.shape[1]
    past = page_table.shape[1] * cache_mla.shape[2]
    mp, ms = n * t_p, b * t_s
    tm_p = min(256, t_p)
    tm_s = min(256, ms)
    tq_band = DIL_KEYS - 1
    assert t_s >= 2 and t_p % tm_p == 0 and ms % tm_s == 0 and tm_s % t_s == 0
    assert all(sw.shape[2] == wdw and wdw == (DIL_KEYS - 1) * r
               for sw, wdw, r in zip(state_wins, DIL_WINDOWS, DIL_RATES))

    xp = x_prompt.reshape(mp, d)
    xs = x_sample.reshape(ms, d)
    mem2 = mem_prompt.reshape(n * n_mem, d)
    pos_p = jnp.arange(t_p)
    pos_s = jnp.tile(past + jnp.arange(t_s), tm_s // t_s)
    band_bias, samp_bias = _dil_bias_tables(rel_bias, tq_band, t_s)
    cache_rows = cache_mem.reshape(cache_mem.shape[0], b, n_mem * MEM_HEADS, 2 * MEM_DIM)

    w_up_b, w_down_b = w_up.astype(BF16), w_down.astype(BF16)
    mem_out, conv_p_out, conv_s_out = [], [], []
    for i in range(2):
        j = i // 2
        if i % 2 == 0:
            w = _prep_mla(i, j, g_attn, w_o, w_in_a, g_q_a, w_q_b, g_kv_a, w_uk, w_uv, g_qn_mem)
            cq, sq, tk = _mla_tables(pos_p, g_qn_nope[j], g_qn_pe[j], g_kn_pe[j])
            q_p, rows_p, mq_p, k_p, v_p = _proj_mla(xp, True, tm_p, w, cq, sq, tk)
            cq, sq, tk = _mla_tables(pos_s, g_qn_nope[j], g_qn_pe[j], g_kn_pe[j])
            qc_s, rows_s, mq_s = _proj_mla(xs, False, tm_s, w, cq, sq, tk)
            hw = MLA_HEADS * LANES
            tok_p = _mla_flash(q_p.reshape(n, t_p, hw), k_p.reshape(n, t_p, hw), v_p.reshape(n, t_p, hw),
                               min(512, t_p)).reshape(mp, -1)
            q3 = qc_s.reshape(b, t_s, MLA_HEADS, MLA_QCAT).transpose(0, 2, 1, 3).reshape(b, MLA_HEADS * t_s, MLA_QCAT)
            ctx = _mla_sample_attn(q3, rows_s.reshape(b, t_s, MLA_ROW), cache_mla[j].transpose(0, 2, 1),
                                   page_table, min(MLA_PAGES_PER_STEP, page_table.shape[1]))
            ctx = ctx.reshape(b, MLA_HEADS, t_s, KV_LORA).transpose(0, 2, 1, 3).reshape(ms, MLA_HEADS * KV_LORA)
            tok_s = _mla_uv(ctx, w['uv_pairs'], tm_s)
            mla_rows_p = rows_p.reshape(1, n, t_p, MLA_ROW)
            mla_rows_s = rows_s.reshape(1, b, t_s, MLA_ROW)
        else:
            w = _prep_dil(i, j, g_attn, w_o, w_in_b, g_qn_b, g_kn_b, g_qn_mem)
            res = _proj_dil(xp.reshape(n, t_p, d), True, tm_p, w)
            mq_p = res[0].reshape(mp, MEM_W)
            nc = DIL_ROW // LANES
            os_, mls = [], []
            for g, dil in enumerate(DIL_RATES):
                l = t_p // dil
                o, ml = _band_attn(res[4 + g].reshape(n * dil, l, DIL_HEADS * LANES),
                                   res[7 + g].reshape(n * dil, l, DIL_SLAB), band_bias[g], tq_band,
                                   2 if l % (2 * tq_band) == 0 else 1)
                os_.append(o.reshape(n, dil, l, DIL_SLAB))
                mls.append(ml.reshape(n, dil, l, LANES))
            tok_p = _merge(os_, mls, DIL_RATES, n, t_p, tm_p).reshape(mp, DIL_SLAB)
            win_p = [wt.reshape(n, wt.shape[1], nc, DIL_HEADS, LANES).transpose(0, 1, 3, 2, 4)
                     .reshape(1, n, wt.shape[1], DIL_HEADS, DIL_ROW) for wt in res[1:4]]
            q_s, kv_s, mq_s = _proj_dil(xs.reshape(b, t_s, d), False, tm_s, w)
            nc = DIL_ROW // LANES
            q_t = jnp.pad(q_s.astype(F32).reshape(b, t_s, N_GROUPS, DIL_HEADS, LANES),
                          ((0, 0), (0, 0), (0, 0), (0, 8 - DIL_HEADS), (0, 0))).transpose(2, 0, 1, 3, 4)
            kv_t = kv_s.reshape(b, t_s, N_GROUPS, DIL_HEADS, nc, LANES).transpose(2, 0, 1, 4, 3, 5)
            kv_t = kv_t.reshape(N_GROUPS, b, t_s, 8, LANES)
            os_, mls, win_s = [], [], []
            for g, dil in enumerate(DIL_RATES):
                buf = state_wins[g][j]
                wb = buf.shape[1]
                buf_t = buf.reshape(b, wb, DIL_HEADS, nc, LANES).transpose(0, 1, 3, 2, 4).reshape(b, wb, 8, LANES)
                o, ml, st = _dil_sample(q_t[g], kv_t[g], buf_t, samp_bias[g], dil)
                os_.append(o.reshape(ms, 8, LANES))
                mls.append(ml.reshape(ms, 8, LANES))
                st = st.reshape(b, wb, nc, DIL_HEADS, LANES).transpose(0, 1, 3, 2, 4)
                win_s.append(st.reshape(1, b, wb, DIL_HEADS, DIL_ROW))
            tok_s = _merge_tiles(os_, mls, tm_s).reshape(ms, nc, DIL_HEADS, LANES).transpose(0, 2, 1, 3)
            tok_s = tok_s.reshape(ms, DIL_SLAB).astype(BF16)

        wm = _prep_mem(i, g_mem, w_mem_kv, g_kn_mem)
        mkv = _mem_kv(mem2, wm, min(256, n * n_mem))
        mem_out.append(mkv.reshape(n, n_mem, MEM_HEADS, 2 * MEM_DIM))
        mo_p = _mem_attn(mq_p.reshape(n, t_p, MEM_W), mkv.reshape(n, n_mem, MEM_W), min(512, t_p)).reshape(mp, MEM_W)
        mo_s = _mem_attn_rows(mq_s.reshape(b, t_s * MEM_HEADS, LANES), cache_rows, i, min(8, b)).reshape(ms, MEM_W)
        xp = _out_proj(xp, tok_p, mo_p, w['wo_tok'], w['wo_mem'], 2 * tm_p if mp % (2 * tm_p) == 0 else tm_p)
        xs = _out_proj(xs, tok_s, mo_s, w['wo_tok'], w['wo_mem'], tm_s)

        wf = _prep_ffn(i, g_ffn, w_up_b, conv_w, conv_b, w_down_b)
        xp, tails = _ffn(xp, t_p, wf, tm_p)
        conv_p_out.append(tails.reshape(n, t_p // tm_p, 8, d_ff)[:, -1, 6:, :])
        prev = state_conv[i]
        zeros = jnp.zeros((b, t_s - 2, d_ff), F32)
        p2 = jnp.concatenate([prev, zeros], axis=1).reshape(ms, d_ff)
        p1 = jnp.concatenate([prev[:, 1:], zeros, jnp.zeros((b, 1, d_ff), F32)], axis=1).reshape(ms, d_ff)
        xs, gates = _ffn(xs, t_s, wf, tm_s, prev=(p1, p2))
        conv_s_out.append(gates.reshape(b, t_s, d_ff)[:, t_s - 2:, :])

    return (xp.reshape(n, t_p, d), xs.reshape(b, t_s, d), mla_rows_p, mla_rows_s,
            win_p[0], win_p[1], win_p[2], win_s[0], win_s[1], win_s[2],
            jnp.stack(mem_out), jnp.stack(conv_p_out), jnp.stack(conv_s_out))
```

```python
import functools
import math

import jax
import jax.numpy as jnp
import numpy as np
from jax import lax
from jax.experimental import pallas as pl
from jax.experimental.pallas import tpu as pltpu

F32 = jnp.float32
BF16 = jnp.bfloat16

EPS = 1e-6
LANES = 128
VMEM_LIMIT = 48 * 1024 * 1024

MLA_HEADS = 12
NOPE_DIM = 64
ROPE_DIM = 32
MLA_V_DIM = 64
Q_LORA = 256
KV_LORA = 256
MLA_ROW = KV_LORA + ROPE_DIM
ROPE_THETA = 10000.0
MLA_SCALE = 1.0 / math.sqrt(NOPE_DIM + ROPE_DIM)
MLA_QCAT = 384
MLA_PAGES_PER_STEP = 128
MLA_PAGE_CHUNK = 16
DIL_SAMPLE_STEP_TOKENS = 1024
FLASH_HEAD_GROUP = 4

DIL_WINDOWS = (128, 512, 2048)
DIL_RATES = (1, 4, 16)
N_GROUPS = 3
DIL_HEADS = 4
DIL_QK = 64
DIL_V = 192
DIL_KEYS = DIL_WINDOWS[0] // DIL_RATES[0] + 1
DIL_SCALE = 1.0 / math.sqrt(DIL_QK)
DIL_ROW = DIL_QK + DIL_V
DIL_SLAB = DIL_HEADS * DIL_ROW

MEM_HEADS = 4
MEM_DIM = 64
MEM_SCALE = 1.0 / math.sqrt(MEM_DIM)
MEM_W = MEM_HEADS * LANES

N_BUCKETS = 32
MAX_DISTANCE = 2048
NEG = -1e30


def _cparams(*sem):
    return pltpu.CompilerParams(dimension_semantics=sem, vmem_limit_bytes=VMEM_LIMIT)


def _dot(a, b):
    return jnp.dot(a, b, preferred_element_type=F32)


def _dot_nt(a, b):
    return lax.dot_general(a, b, (((1,), (1,)), ((), ())), preferred_element_type=F32)


def _rms_rows(x, g):
    return x * lax.rsqrt(jnp.mean(x * x, axis=-1, keepdims=True) + EPS) * g


def _seg_scale(x, s_ref, st2_ref, inv_ref):
    ss = _dot((x * x).astype(BF16), s_ref[...])
    r = lax.rsqrt(ss * inv_ref[...] + EPS)
    hi = r.astype(BF16)
    lo = (r - hi.astype(F32)).astype(BF16)
    return _dot(jnp.concatenate([hi, lo], axis=1), st2_ref[...])


def _cols(vals, width):
    rows = vals[0].shape[0]
    lane = lax.broadcasted_iota(jnp.int32, (rows, width), 1)
    out = jnp.zeros((rows, width), F32)
    for k, v in enumerate(vals):
        out = jnp.where(lane == k, v, out)
    return out


def _full(shape):
    return pl.BlockSpec(shape, lambda *_: (0,) * len(shape))


def _proj_mla_kernel(prompt, x_ref, ga_ref, win_ref, gqa_ref, wq_ref, wqs_ref, gkv_ref,
                     cq_ref, sq_ref, tk_ref, s_ref, st_ref, inv_ref,
                     sm_ref, smt_ref, invm_ref, gm_ref, *rest):
    if prompt:
        wkc_ref, wkp_ref, wv_ref, vone_ref, q_out, rows_out, mq_out, k_out, v_out = rest
    else:
        wcat_ref, q_out, rows_out, mq_out = rest
    h = _rms_rows(x_ref[...], ga_ref[...]).astype(BF16)
    z = _dot(h, win_ref[...])
    qa = _rms_rows(z[:, :Q_LORA], gqa_ref[...]).astype(BF16)
    q = _dot(qa, wq_ref[...])
    qs = _dot(qa, wqs_ref[...])
    scale = _seg_scale(q, s_ref, st_ref, inv_ref)
    cq = jnp.concatenate([cq_ref[...]] * MLA_HEADS, axis=1)
    sq = jnp.concatenate([sq_ref[...]] * MLA_HEADS, axis=1)
    qf = ((q * cq + qs * sq) * scale).astype(BF16)
    c = _rms_rows(z[:, Q_LORA:Q_LORA + KV_LORA], gkv_ref[...])
    zk = z[:, 1024:1152]
    kp = zk[:, :ROPE_DIM]
    rk = lax.rsqrt(jnp.mean(kp * kp, axis=-1, keepdims=True) + EPS)
    t = zk * tk_ref[...] * rk
    kr = t + pltpu.roll(t, LANES - ROPE_DIM, 1)
    rows_out[:, :KV_LORA] = c
    rows_out[:, KV_LORA:] = kr[:, :ROPE_DIM]
    mq = z[:, 512:1024]
    mscale = _seg_scale(mq, sm_ref, smt_ref, invm_ref)
    mq_out[...] = (mq * mscale * gm_ref[...]).astype(BF16)
    if prompt:
        q_out[...] = qf
        cb = c.astype(BF16)
        k_out[...] = (_dot(cb, wkc_ref[...]) + _dot(kr.astype(BF16), wkp_ref[...])).astype(BF16)
        v_out[...] = (_dot(cb, wv_ref[...]) + vone_ref[...]).astype(BF16)
    else:
        for hd in range(MLA_HEADS):
            q_out[:, hd * MLA_QCAT:(hd + 1) * MLA_QCAT] = _dot(
                qf[:, hd * LANES:(hd + 1) * LANES], wcat_ref[hd]).astype(BF16)


def _proj_mla(x, prompt, tm, w, cq, sq, tk):
    m, d = x.shape
    hw = MLA_HEADS * LANES
    row = lambda width: pl.BlockSpec((tm, width), lambda i: (i, 0))
    args = [x, w['g_attn'], w['w_in'], w['g_q_a'], w['wq'], w['wq_sw'], w['g_kv_a'],
            cq, sq, tk, w['S'], w['ST'], w['inv'], w['Sm'], w['SmT'], w['invm'], w['gm']]
    ptiles = cq.shape[0] // tm
    pos = pl.BlockSpec((tm, LANES), lambda i: (i % ptiles, 0))
    specs = [row(d)] + [_full(a.shape) for a in args[1:7]] + [pos, pos, pos] \
        + [_full(a.shape) for a in args[10:]]
    if prompt:
        extra = [w['wk_c'], w['wk_p'], w['wv'], w['vone']]
        outs = [jax.ShapeDtypeStruct((m, hw), BF16), jax.ShapeDtypeStruct((m, MLA_ROW), F32),
                jax.ShapeDtypeStruct((m, MEM_W), BF16), jax.ShapeDtypeStruct((m, hw), BF16),
                jax.ShapeDtypeStruct((m, hw), BF16)]
        ospecs = [row(hw), row(MLA_ROW), row(MEM_W), row(hw), row(hw)]
    else:
        extra = [w['wcat']]
        outs = [jax.ShapeDtypeStruct((m, MLA_HEADS * MLA_QCAT), BF16),
                jax.ShapeDtypeStruct((m, MLA_ROW), F32), jax.ShapeDtypeStruct((m, MEM_W), BF16)]
        ospecs = [row(MLA_HEADS * MLA_QCAT), row(MLA_ROW), row(MEM_W)]
    return pl.pallas_call(
        functools.partial(_proj_mla_kernel, prompt),
        grid=(m // tm,),
        in_specs=specs + [_full(a.shape) for a in extra],
        out_specs=ospecs, out_shape=outs,
        compiler_params=_cparams("parallel"),
        name="proj_mla_prompt" if prompt else "proj_mla_sample",
    )(*args, *extra)


def _flash_kernel(tq, qi_ref, kj_ref, q_ref, k_ref, v_ref, o_ref, m_scr, acc_scr):
    i, j = qi_ref[pl.program_id(1)], kj_ref[pl.program_id(1)]

    @pl.when(j == 0)
    def _():
        m_scr[...] = jnp.full(m_scr.shape, NEG, F32)
        acc_scr[...] = jnp.zeros(acc_scr.shape, F32)

    def step(masked):
        if masked:
            keep = lax.broadcasted_iota(jnp.int32, (tq, tq), 1) <= lax.broadcasted_iota(jnp.int32, (tq, tq), 0)
        for h0 in range(0, MLA_HEADS, FLASH_HEAD_GROUP):
            heads = range(h0, h0 + FLASH_HEAD_GROUP)
            scores = [_dot_nt(q_ref[0, :, hd * LANES:(hd + 1) * LANES], k_ref[0, :, hd * LANES:(hd + 1) * LANES])
                      for hd in heads]
            for hd, s in zip(heads, scores):
                if masked:
                    s = jnp.where(keep, s, NEG)
                m_old = m_scr[hd]
                m_new = jnp.maximum(m_old, jnp.max(s, axis=-1, keepdims=True))
                alpha = jnp.exp(m_old - m_new)
                p = jnp.exp(s - jnp.concatenate([m_new] * (tq // LANES), axis=1))
                acc_scr[hd] = alpha * acc_scr[hd] + _dot(p.astype(BF16), v_ref[0, :, hd * LANES:(hd + 1) * LANES])
                m_scr[hd] = m_new

    @pl.when(j < i)
    def _():
        step(False)

    @pl.when(j == i)
    def _():
        step(True)
        low = lax.broadcasted_iota(jnp.int32, (tq, LANES), 1) < MLA_V_DIM
        for pr in range(MLA_HEADS // 2):
            ae, ao = acc_scr[2 * pr], acc_scr[2 * pr + 1]
            o_ref[0, :, pr * LANES:(pr + 1) * LANES] = jnp.where(
                low, ae / ae[:, MLA_V_DIM:MLA_V_DIM + 1], ao / ao[:, 0:1]).astype(BF16)


def _mla_flash(q, k, v, tq):
    n, t, hw = q.shape
    nq = t // tq
    pairs = [(i, j) for i in range(nq) for j in range(i + 1)]
    qi = jnp.asarray([p[0] for p in pairs], jnp.int32)
    kj = jnp.asarray([p[1] for p in pairs], jnp.int32)
    return pl.pallas_call(
        functools.partial(_flash_kernel, tq),
        grid_spec=pltpu.PrefetchScalarGridSpec(
            num_scalar_prefetch=2,
            grid=(n, len(pairs)),
            in_specs=[pl.BlockSpec((1, tq, hw), lambda b, s, qi, kj: (b, qi[s], 0)),
                      pl.BlockSpec((1, tq, hw), lambda b, s, qi, kj: (b, kj[s], 0)),
                      pl.BlockSpec((1, tq, hw), lambda b, s, qi, kj: (b, kj[s], 0))],
            out_specs=pl.BlockSpec((1, tq, MLA_HEADS * MLA_V_DIM), lambda b, s, qi, kj: (b, qi[s], 0)),
            scratch_shapes=[pltpu.VMEM((MLA_HEADS, tq, LANES), F32), pltpu.VMEM((MLA_HEADS, tq, LANES), F32)]),
        out_shape=jax.ShapeDtypeStruct((n, t, MLA_HEADS * MLA_V_DIM), BF16),
        compiler_params=_cparams("parallel", "arbitrary"),
        name="mla_flash_prompt",
    )(qi, kj, q, k, v)


def _mla_sample_kernel(npg, t_s, pt_ref, q_ref, new_ref, *rest):
    pages, (o_ref, m_scr, l_scr, acc_scr) = rest[:npg], rest[npg:]
    c = pl.program_id(1)

    @pl.when(c == 0)
    def _():
        m_scr[...] = jnp.full(m_scr.shape, NEG, F32)
        l_scr[...] = jnp.zeros(l_scr.shape, F32)
        acc_scr[...] = jnp.zeros(acc_scr.shape, F32)

    q = q_ref[0][:, :MLA_ROW]

    def update(state, s, pv):
        m_old, l_old, acc = state
        m_new = jnp.maximum(m_old, jnp.max(s, axis=-1, keepdims=True))
        alpha = jnp.exp(m_old - m_new)
        p = jnp.exp(s - m_new[:, 0:1])
        l_new = alpha * l_old + jnp.sum(p, axis=-1, keepdims=True)
        acc = jnp.concatenate([alpha] * (KV_LORA // LANES), axis=1) * acc + pv(p.astype(BF16))
        return m_new, l_new, acc

    def load(c0):
        return jnp.concatenate([pg[0].astype(BF16) for pg in pages[c0:c0 + MLA_PAGE_CHUNK]], axis=1)

    starts = list(range(0, npg, MLA_PAGE_CHUNK))
    state = (m_scr[...], l_scr[...], acc_scr[...])
    kts = {0: load(starts[0])}
    scores = {0: _dot(q, kts[0])}
    if len(starts) > 1:
        kts[1] = load(starts[1])
    for ch in range(len(starts)):
        if ch + 1 < len(starts):
            scores[ch + 1] = _dot(q, kts[ch + 1])
        if ch + 2 < len(starts):
            kts[ch + 2] = load(starts[ch + 2])
        state = update(state, scores.pop(ch), lambda p, kt=kts.pop(ch): _dot_nt(p, kt[:KV_LORA]))
    m_scr[...], l_scr[...], acc_scr[...] = state

    @pl.when(c == pl.num_programs(1) - 1)
    def _():
        nb = new_ref[0].astype(BF16)
        s = _dot_nt(q, nb)
        rows = q.shape[0]
        tq = lax.broadcasted_iota(jnp.int32, (rows, t_s), 0) % t_s
        s = jnp.where(lax.broadcasted_iota(jnp.int32, (rows, t_s), 1) <= tq, s, NEG)
        _, l_fin, acc = update((m_scr[...], l_scr[...], acc_scr[...]), s, lambda p: _dot(p, nb[:, :KV_LORA]))
        o_ref[0] = (acc / jnp.concatenate([l_fin] * (KV_LORA // LANES), axis=1)).astype(BF16)


def _mla_sample_attn(q3, rows3, pool_t, page_table, npg):
    b, rows, _ = q3.shape
    t_s = rows3.shape[1]
    n_pages = page_table.shape[1]
    psz = pool_t.shape[2]

    def page_spec(k):
        return pl.BlockSpec((1, MLA_ROW, psz), lambda s, c, pt: (pt[s * n_pages + c * npg + k], 0, 0))

    return pl.pallas_call(
        functools.partial(_mla_sample_kernel, npg, t_s),
        grid_spec=pltpu.PrefetchScalarGridSpec(
            num_scalar_prefetch=1,
            grid=(b, n_pages // npg),
            in_specs=[pl.BlockSpec((1, rows, MLA_QCAT), lambda s, c, pt: (s, 0, 0)),
                      pl.BlockSpec((1, t_s, MLA_ROW), lambda s, c, pt: (s, 0, 0))]
            + [page_spec(k) for k in range(npg)],
            out_specs=pl.BlockSpec((1, rows, KV_LORA), lambda s, c, pt: (s, 0, 0)),
            scratch_shapes=[pltpu.VMEM((rows, LANES), F32), pltpu.VMEM((rows, LANES), F32),
                            pltpu.VMEM((rows, KV_LORA), F32)]),
        out_shape=jax.ShapeDtypeStruct((b, rows, KV_LORA), BF16),
        compiler_params=_cparams("parallel", "arbitrary"),
        name="mla_sample_attn",
    )(page_table.reshape(-1), q3, rows3, *([pool_t] * npg))


def _uv_kernel(ctx_ref, w_ref, o_ref):
    for pr in range(MLA_HEADS // 2):
        o_ref[:, pr * LANES:(pr + 1) * LANES] = _dot(
            ctx_ref[:, pr * 2 * KV_LORA:(pr + 1) * 2 * KV_LORA], w_ref[pr]).astype(BF16)


def _mla_uv(ctx, w_pairs, tm):
    m = ctx.shape[0]
    return pl.pallas_call(
        _uv_kernel, grid=(m // tm,),
        in_specs=[pl.BlockSpec((tm, ctx.shape[1]), lambda i: (i, 0)), _full(w_pairs.shape)],
        out_specs=pl.BlockSpec((tm, MLA_HEADS * MLA_V_DIM), lambda i: (i, 0)),
        out_shape=jax.ShapeDtypeStruct((m, MLA_HEADS * MLA_V_DIM), BF16),
        compiler_params=_cparams("parallel"), name="mla_sample_uv",
    )(ctx, w_pairs)


def _mem_kv_kernel(x_ref, g_ref, w_ref, s_ref, st_ref, inv_ref, gk_ref, vm_ref, o_ref):
    h = _rms_rows(x_ref[...], g_ref[...]).astype(BF16)
    kv = _dot(h, w_ref[...])
    scale = _seg_scale(kv, s_ref, st_ref, inv_ref) * gk_ref[...] + vm_ref[...]
    o_ref[...] = kv * scale


def _mem_kv(mem2, w, tm):
    m, d = mem2.shape
    args = [mem2, w['g_mem'], w['w_mem'], w['Sk'], w['SkT'], w['invk'], w['gk'], w['vmask']]
    return pl.pallas_call(
        _mem_kv_kernel, grid=(m // tm,),
        in_specs=[pl.BlockSpec((tm, d), lambda i: (i, 0))] + [_full(a.shape) for a in args[1:]],
        out_specs=pl.BlockSpec((tm, MEM_W), lambda i: (i, 0)),
        out_shape=jax.ShapeDtypeStruct((m, MEM_W), F32),
        compiler_params=_cparams("parallel"), name="mem_kv",
    )(*args)


def _mem_attn_body(q, kv):
    outs = []
    scores = [_dot_nt(q[:, hd * LANES:(hd + 1) * LANES], kv[:, hd * LANES:(hd + 1) * LANES])
              for hd in range(MEM_HEADS)]
    for hd, s in enumerate(scores):
        sl = slice(hd * LANES, (hd + 1) * LANES)
        p = jnp.exp(s - jnp.max(s, axis=-1, keepdims=True))
        p = p / jnp.sum(p, axis=-1, keepdims=True)
        outs.append(_dot(p.astype(BF16), kv[:, sl]))
    return jnp.concatenate(outs, axis=-1)


def _mem_attn_kernel(q_ref, kv_ref, o_ref):
    o_ref[0] = _mem_attn_body(q_ref[0], kv_ref[0].astype(BF16)).astype(BF16)


def _mem_attn(q3, kv3, tq):
    n, t, _ = q3.shape
    return pl.pallas_call(
        _mem_attn_kernel, grid=(n, t // tq),
        in_specs=[pl.BlockSpec((1, tq, MEM_W), lambda b, i: (b, i, 0)),
                  pl.BlockSpec((1, kv3.shape[1], MEM_W), lambda b, i: (b, 0, 0))],
        out_specs=pl.BlockSpec((1, tq, MEM_W), lambda b, i: (b, i, 0)),
        out_shape=jax.ShapeDtypeStruct((n, t, MEM_W), BF16),
        compiler_params=_cparams("parallel", "parallel"), name="mem_attn",
    )(q3, kv3)


def _mem_rows_kernel(sb, q_ref, kv_ref, o_ref):
    nq, nkv = q_ref.shape[1], kv_ref.shape[2]
    same = (lax.broadcasted_iota(jnp.int32, (nq, nkv), 0) % MEM_HEADS
            == lax.broadcasted_iota(jnp.int32, (nq, nkv), 1) % MEM_HEADS)
    kvs = [kv_ref[0, s].astype(BF16) for s in range(sb)]
    scores = [_dot_nt(q_ref[s], kvs[s]) for s in range(sb)]
    for s in range(sb):
        sc = jnp.where(same, scores[s], NEG)
        p = jnp.exp(sc - jnp.max(sc, axis=-1, keepdims=True))
        p = p / jnp.sum(p, axis=-1, keepdims=True)
        o_ref[s] = _dot(p.astype(BF16), kvs[s]).astype(BF16)


def _mem_attn_rows(q3, cache4, layer, sb):
    b, nq, _ = q3.shape
    return pl.pallas_call(
        functools.partial(_mem_rows_kernel, sb), grid=(b // sb,),
        in_specs=[pl.BlockSpec((sb, nq, LANES), lambda i: (i, 0, 0)),
                  pl.BlockSpec((1, sb, cache4.shape[2], LANES), lambda i: (layer, i, 0, 0))],
        out_specs=pl.BlockSpec((sb, nq, LANES), lambda i: (i, 0, 0)),
        out_shape=jax.ShapeDtypeStruct((b, nq, LANES), BF16),
        compiler_params=_cparams("parallel"), name="mem_attn_sample",
    )(q3, cache4)


def _out_proj_kernel(x_ref, a_ref, b_ref, wa_ref, wb_ref, o_ref):
    o_ref[...] = x_ref[...] + _dot(a_ref[...], wa_ref[...]) + _dot(b_ref[...], wb_ref[...])


def _out_proj(x, a, b, wa, wb, tm):
    m, d = x.shape
    row = lambda width: pl.BlockSpec((tm, width), lambda i: (i, 0))
    return pl.pallas_call(
        _out_proj_kernel, grid=(m // tm,),
        in_specs=[row(d), row(a.shape[1]), row(b.shape[1]), _full(wa.shape), _full(wb.shape)],
        out_specs=row(d), out_shape=jax.ShapeDtypeStruct((m, d), F32),
        compiler_params=_cparams("parallel"), name="out_proj",
    )(x, a, b, wa, wb)


def _ffn_kernel(seq_tiles, t_s, d_ff, x_ref, g_ref, wu_ref, cw_ref, cb_ref, wd_ref, *rest):
    if seq_tiles:
        o_ref, st_ref, g_scr = rest
    else:
        p1_ref, p2_ref, o_ref, st_ref = rest
    x = x_ref[...]
    tm = x.shape[0]
    h = _rms_rows(x, g_ref[...]).astype(BF16)
    u = _dot(h, wu_ref[...])
    gate, val = u[:, :d_ff], u[:, d_ff:]
    if seq_tiles:
        @pl.when(pl.program_id(0) % seq_tiles == 0)
        def _():
            g_scr[0:8, :] = jnp.zeros((8, d_ff), F32)
        g_scr[8:, :] = gate
        g1 = g_scr[7:tm + 7, :]
        g2 = g_scr[6:tm + 6, :]
        tail = gate[tm - 8:, :]
        g_scr[0:8, :] = tail
        st_ref[...] = tail
    else:
        tpos = lax.broadcasted_iota(jnp.int32, (tm, 1), 0) % t_s
        g1 = jnp.where(tpos >= 1, pltpu.roll(gate, 1, 0), p1_ref[...])
        g2 = jnp.where(tpos >= 2, pltpu.roll(gate, 2, 0), p2_ref[...])
        st_ref[...] = gate
    cw = cw_ref[...]
    conv = cb_ref[...] + g2 * cw[0:1] + g1 * cw[1:2] + gate * cw[2:3]
    act = (conv * (1.0 / (1.0 + jnp.exp(-conv))) * val).astype(BF16)
    o_ref[...] = x + _dot(act, wd_ref[...])


def _ffn(x, seq_len, w, tm, prev=None):
    m, d = x.shape
    d_ff = w['w_down'].shape[1]
    layer = w['layer']
    row = lambda width: pl.BlockSpec((tm, width), lambda i: (i, 0))
    single = dict(pipeline_mode=pl.Buffered(1))
    wspecs = [_full(w['g_ffn'].shape),
              pl.BlockSpec((None,) + w['w_up'].shape[1:], lambda i: (layer, 0, 0), **single),
              _full(w['conv_w'].shape), _full(w['conv_b'].shape),
              pl.BlockSpec((None,) + w['w_down'].shape[1:], lambda i: (layer, 0, 0), **single)]
    wargs = [w['g_ffn'], w['w_up'], w['conv_w'], w['conv_b'], w['w_down']]
    if prev is None:
        seq_tiles = seq_len // tm
        return pl.pallas_call(
            functools.partial(_ffn_kernel, seq_tiles, 0, d_ff), grid=(m // tm,),
            in_specs=[row(d)] + wspecs,
            out_specs=[row(d), pl.BlockSpec((8, d_ff), lambda i: (i, 0))],
            out_shape=[jax.ShapeDtypeStruct((m, d), F32),
                       jax.ShapeDtypeStruct((m // tm * 8, d_ff), F32)],
            scratch_shapes=[pltpu.VMEM((tm + 8, d_ff), F32)],
            compiler_params=_cparams("arbitrary"), name="ffn_prompt",
        )(x, *wargs)
    p1, p2 = prev
    return pl.pallas_call(
        functools.partial(_ffn_kernel, 0, seq_len, d_ff), grid=(m // tm,),
        in_specs=[row(d)] + wspecs + [row(d_ff), row(d_ff)],
        out_specs=[row(d), row(d_ff)],
        out_shape=[jax.ShapeDtypeStruct((m, d), F32), jax.ShapeDtypeStruct((m, d_ff), F32)],
        compiler_params=_cparams("parallel"), name="ffn_sample",
    )(x, *wargs, p1, p2)


QW = N_GROUPS * DIL_HEADS * LANES
KVW = N_GROUPS * DIL_SLAB


def _proj_dil_kernel(prompt, win_skips, x_ref, ga_ref, win_ref, sq_ref, sqt_ref, invq_ref, gq_ref,
                     sk_ref, skt_ref, invk_ref, gk_ref, vm_ref,
                     sm_ref, smt_ref, invm_ref, gm_ref, *rest):
    x = x_ref[0] if prompt else x_ref[...]
    h = _rms_rows(x, ga_ref[...]).astype(BF16)
    z = _dot(h, win_ref[...])
    q = z[:, :QW]
    qn = q * _seg_scale(q, sq_ref, sqt_ref, invq_ref) * gq_ref[...]
    kv = z[:, QW:QW + KVW]
    kvn = kv * (_seg_scale(kv, sk_ref, skt_ref, invk_ref) * gk_ref[...] + vm_ref[...])
    mq = z[:, QW + KVW:]
    mqn = (mq * _seg_scale(mq, sm_ref, smt_ref, invm_ref) * gm_ref[...]).astype(BF16)
    if not prompt:
        q_out, kv_out, mq_out = rest
        q_out[...] = qn.astype(BF16)
        kv_out[...] = kvn
        mq_out[...] = mqn
        return
    mq_out, win_outs = rest[0], rest[1:4]
    q_outs, kvb_outs, q_scr, kv_scr = rest[4:7], rest[7:10], rest[10], rest[11]
    mq_out[0] = mqn
    for c in range(QW // LANES):
        q_scr[c] = qn[:, c * LANES:(c + 1) * LANES]
    for c in range(KVW // LANES):
        kv_scr[c] = kvn[:, c * LANES:(c + 1) * LANES]
    tm = q.shape[0]
    qc, kc = DIL_HEADS, DIL_SLAB // LANES
    for g in range(N_GROUPS):
        rows = win_outs[g].shape[1]

        @pl.when(pl.program_id(1) >= win_skips[g])
        def _(g=g, rows=rows):
            for s in range(8):
                chunk = g * kc + (s % DIL_HEADS) * (DIL_ROW // LANES) + s // DIL_HEADS
                win_outs[g][0, :, s, :] = kv_scr[chunk, tm - rows:tm, :]
    for g, d in enumerate(DIL_RATES):
        for r in range(d):
            rows = pl.ds(r, tm // d, stride=d) if d > 1 else slice(None)
            for c in range(qc):
                q_outs[g][0, r, :, c * LANES:(c + 1) * LANES] = q_scr[g * qc + c, rows, :].astype(BF16)
            for c in range(kc):
                kvb_outs[g][0, r, :, c * LANES:(c + 1) * LANES] = kv_scr[g * kc + c, rows, :].astype(BF16)


def _proj_dil(x3, prompt, tm, w):
    n, t, d = x3.shape
    args = [x3, w['g_attn'], w['w_in'], w['Sq'], w['SqT'], w['invq'], w['gq'],
            w['Sk'], w['SkT'], w['invk'], w['gk'], w['vmask'], w['Sm'], w['SmT'], w['invm'], w['gm']]
    if not prompt:
        x2 = x3.reshape(n * t, d)
        m = n * t
        row = lambda width: pl.BlockSpec((tm, width), lambda i: (i, 0))
        return pl.pallas_call(
            functools.partial(_proj_dil_kernel, False, None), grid=(m // tm,),
            in_specs=[row(d)] + [_full(a.shape) for a in args[1:]],
            out_specs=[row(QW), row(KVW), row(MEM_W)],
            out_shape=[jax.ShapeDtypeStruct((m, QW), BF16), jax.ShapeDtypeStruct((m, KVW), F32),
                       jax.ShapeDtypeStruct((m, MEM_W), BF16)],
            compiler_params=_cparams("parallel"), name="proj_dil_sample",
        )(x2, *args[1:])
    row3 = lambda width: pl.BlockSpec((1, tm, width), lambda b, i: (b, i, 0))
    outs = [jax.ShapeDtypeStruct((n, t, MEM_W), BF16)]
    ospecs = [row3(MEM_W)]
    skips = []
    for wdw in DIL_WINDOWS:
        tail = min(wdw, t)
        rows = min(tail, tm)
        skip = (t - tail) // tm
        skips.append(skip)
        outs.append(jax.ShapeDtypeStruct((n, tail, 8, LANES), F32))
        ospecs.append(pl.BlockSpec((1, rows, 8, LANES),
                                   lambda b, i, skip=skip, last=tail // rows - 1:
                                   (b, jnp.clip(i - skip, 0, last), 0, 0)))
    for width in (DIL_HEADS * LANES, DIL_SLAB):
        for dil in DIL_RATES:
            outs.append(jax.ShapeDtypeStruct((n, dil, t // dil, width), BF16))
            ospecs.append(pl.BlockSpec((1, dil, tm // dil, width), lambda b, i: (b, 0, i, 0)))
    return pl.pallas_call(
        functools.partial(_proj_dil_kernel, True, tuple(skips)), grid=(n, t // tm),
        in_specs=[row3(d)] + [_full(a.shape) for a in args[1:]],
        out_specs=ospecs, out_shape=outs,
        scratch_shapes=[pltpu.VMEM((QW // LANES, tm, LANES), F32), pltpu.VMEM((KVW // LANES, tm, LANES), F32)],
        compiler_params=_cparams("parallel", "arbitrary"), name="proj_dil_prompt",
    )(*args)


def _band_kernel(tq, nqb, q_ref, kp_ref, kc_ref, bias_ref, o_ref, ml_ref):
    i = pl.program_id(1)
    kcat = jnp.concatenate([kp_ref[0], kc_ref[0]], axis=0)
    first = jnp.logical_and(i == 0, lax.broadcasted_iota(jnp.int32, (tq, 2 * tq), 1) < tq)
    work = [(j, hd) for j in range(nqb) for hd in range(DIL_HEADS)]
    scores = [_dot_nt(q_ref[0, j * tq:(j + 1) * tq, hd * LANES:(hd + 1) * LANES],
                      kcat[j * tq:(j + 2) * tq, hd * DIL_ROW:hd * DIL_ROW + LANES]) for j, hd in work]
    stats = [[] for _ in range(nqb)]
    for (j, hd), s in zip(work, scores):
        s = s + bias_ref[hd]
        if j == 0:
            s = jnp.where(first, NEG, s)
        m = jnp.max(s, axis=-1, keepdims=True)
        p = jnp.exp(s - m)
        l = jnp.sum(p, axis=-1, keepdims=True)
        o_ref[0, j * tq:(j + 1) * tq, hd * DIL_ROW:(hd + 1) * DIL_ROW] = _dot(
            (p / l).astype(BF16), kcat[j * tq:(j + 2) * tq, hd * DIL_ROW:(hd + 1) * DIL_ROW]).astype(BF16)
        stats[j].append((m, l))
    for j in range(nqb):
        ml_ref[0, j * tq:(j + 1) * tq] = _cols([m for m, _ in stats[j]] + [l for _, l in stats[j]], LANES)


def _band_attn(q, kv, bias, tq, nqb):
    s, l, _ = q.shape
    tb = tq * nqb
    return pl.pallas_call(
        functools.partial(_band_kernel, tq, nqb), grid=(s, l // tb),
        in_specs=[pl.BlockSpec((1, tb, DIL_HEADS * LANES), lambda b, i: (b, i, 0)),
                  pl.BlockSpec((1, tq, DIL_SLAB), lambda b, i: (b, jnp.maximum(i * nqb - 1, 0), 0)),
                  pl.BlockSpec((1, tb, DIL_SLAB), lambda b, i: (b, i, 0)),
                  _full(bias.shape)],
        out_specs=[pl.BlockSpec((1, tb, DIL_SLAB), lambda b, i: (b, i, 0)),
                   pl.BlockSpec((1, tb, LANES), lambda b, i: (b, i, 0))],
        out_shape=[jax.ShapeDtypeStruct((s, l, DIL_SLAB), BF16),
                   jax.ShapeDtypeStruct((s, l, LANES), F32)],
        compiler_params=_cparams("parallel", "parallel"), name="dil_band_attn",
    )(q, kv, kv, bias)


def _merge_kernel(rates, *refs):
    ng = len(rates)
    o_refs, ml_refs, out_ref = refs[:ng], refs[ng:2 * ng], refs[2 * ng]
    scr = refs[2 * ng + 1:]
    nch = DIL_SLAB // LANES
    chunks, mls = [], []
    for g, d in enumerate(rates):
        if d == 1:
            chunks.append([o_refs[g][0, 0, :, c * LANES:(c + 1) * LANES].astype(F32) for c in range(nch)])
            mls.append(ml_refs[g][0, 0])
        else:
            o_scr, ml_scr = scr[2 * g], scr[2 * g + 1]
            tm = ml_scr.shape[0]
            for r in range(d):
                rows = pl.ds(r, tm // d, stride=d)
                for c in range(nch):
                    o_scr[c, rows, :] = o_refs[g][0, r, :, c * LANES:(c + 1) * LANES].astype(F32)
                ml_scr[rows, :] = ml_refs[g][0, r]
            chunks.append([o_scr[c] for c in range(nch)])
            mls.append(ml_scr[...])
    m_all = functools.reduce(jnp.maximum, [ml[:, :DIL_HEADS] for ml in mls])
    es = [ml[:, DIL_HEADS:2 * DIL_HEADS] * jnp.exp(ml[:, :DIL_HEADS] - m_all) for ml in mls]
    tot = functools.reduce(jnp.add, es)
    ws = [e / tot for e in es]
    for c in range(nch):
        hd = c // (DIL_ROW // LANES)
        acc = None
        for g in range(ng):
            term = chunks[g][c] * ws[g][:, hd:hd + 1]
            acc = term if acc is None else acc + term
        out_ref[0, :, c * LANES:(c + 1) * LANES] = acc.astype(BF16)


def _merge(os_, mls, rates, n, t, tm):
    specs = [pl.BlockSpec((1, d, tm // d, DIL_SLAB), lambda b, i: (b, 0, i, 0)) for d in rates] \
        + [pl.BlockSpec((1, d, tm // d, LANES), lambda b, i: (b, 0, i, 0)) for d in rates]
    scratch = []
    for d in rates:
        scratch += [pltpu.VMEM((DIL_SLAB // LANES, tm, LANES), F32), pltpu.VMEM((tm, LANES), F32)]
    return pl.pallas_call(
        functools.partial(_merge_kernel, tuple(rates)), grid=(n, t // tm),
        in_specs=specs,
        out_specs=pl.BlockSpec((1, tm, DIL_SLAB), lambda b, i: (b, i, 0)),
        out_shape=jax.ShapeDtypeStruct((n, t, DIL_SLAB), BF16),
        scratch_shapes=scratch,
        compiler_params=_cparams("parallel", "parallel"), name="dil_merge",
    )(*os_, *mls)


def _dil_sample_kernel(d, wb, t_s, sb, q_ref, new_ref, buf_ref, bias_ref, o_ref, ml_ref, st_ref, *scr):
    nk = DIL_KEYS - 1
    news, srcs = [], []
    for sq in range(sb):
        new = new_ref[sq]
        st_ref[sq, 0:wb - t_s] = buf_ref[sq, t_s:wb]
        st_ref[sq, wb - t_s:wb] = new
        if d < t_s:
            src = scr[0].at[sq]
            src[0:wb] = buf_ref[sq]
            src[wb:wb + t_s] = new
        else:
            src = buf_ref.at[sq]
        news.append(new)
        srcs.append(src)
    top = lax.broadcasted_iota(jnp.int32, (8, LANES), 0) < DIL_HEADS
    lane = lax.broadcasted_iota(jnp.int32, (8, LANES), 1)
    bias = bias_ref[...]
    nq, ntok = _dil_classes(d, t_s)

    def both_halves(col):
        x = jnp.where(top, jnp.broadcast_to(col, (8, LANES)), 0.0)
        return x + pltpu.roll(x, DIL_HEADS, 0)

    work = [(sq, r) for sq in range(sb) for r in range(t_s // nq)]
    toks, scores, stats, outs = [], [], [], []
    for sq, r in work:
        src = srcs[sq]
        if d < t_s:
            a3 = src[pl.ds(r, ntok, stride=d)] if d > 1 else src[pl.ds(r, ntok)]
        else:
            pad = jnp.zeros((ntok - nk - 1, 8, LANES), F32)
            a3 = jnp.concatenate([src[pl.ds(r, nk, stride=d)], news[sq][r:r + 1], pad], axis=0)
        a = a3.reshape(ntok * 8, LANES).astype(BF16)
        qc = q_ref[sq, pl.ds(r, nq, stride=d)] if nq > 1 else q_ref[sq, r:r + 1]
        qc = qc.reshape(nq * 8, LANES).astype(BF16)
        toks.append(a)
        scores.append(_dot_nt(qc, a) + bias)
    for w in range(len(work)):
        s = scores[w]
        m = jnp.max(s, axis=-1, keepdims=True)
        p = jnp.exp(s - m)
        l = jnp.sum(p, axis=-1, keepdims=True)
        pn = p / l
        stats.append((m, l))
        p2 = jnp.concatenate([pn, pltpu.roll(pn, DIL_HEADS, 1)], axis=0).astype(BF16)
        outs.append(_dot(p2, toks[w]))
    for w, (sq, r) in enumerate(work):
        o2, (m, l) = outs[w], stats[w]
        for k in range(nq):
            i = r + k * d
            lo, hi = o2[8 * k:8 * k + 8], o2[8 * (nq + k):8 * (nq + k) + 8]
            o_ref[sq, i] = jnp.where(top, lo, pltpu.roll(hi, DIL_HEADS, 0))
            ml_ref[sq, i] = jnp.where(lane == 0, both_halves(m[8 * k:8 * k + 8]),
                                      jnp.where(lane == 1, both_halves(l[8 * k:8 * k + 8]), 0.0))


def _dil_classes(d, t_s):
    nq = max(t_s // d, 1)
    ntok = DIL_KEYS - 1 + nq
    return nq, ntok + ntok % 2


def _dil_sample(qg, newg, buf, bias, d):
    b, t_s = qg.shape[:2]
    wb = buf.shape[1]
    sb = max(1, min(b, DIL_SAMPLE_STEP_TOKENS // wb))
    blk = lambda r: pl.BlockSpec((sb, r, 8, LANES), lambda s: (s, 0, 0, 0))
    tiles = jax.ShapeDtypeStruct((b, t_s, 8, LANES), F32)
    return pl.pallas_call(
        functools.partial(_dil_sample_kernel, d, wb, t_s, sb), grid=(b // sb,),
        in_specs=[blk(t_s), blk(t_s), blk(wb), _full(bias.shape)],
        out_specs=[blk(t_s), blk(t_s), blk(wb)],
        out_shape=[tiles, tiles, jax.ShapeDtypeStruct(buf.shape, F32)],
        scratch_shapes=[pltpu.VMEM((sb, wb + t_s, 8, LANES), F32)] if d < t_s else [],
        compiler_params=_cparams("parallel"), name="dil_sample_attn",
    )(qg, newg, buf, bias)


def _merge_tiles_kernel(ng, *refs):
    o_refs, ml_refs, out_ref = refs[:ng], refs[ng:2 * ng], refs[2 * ng]
    ms = [r[...][:, :, 0:1] for r in ml_refs]
    ls = [r[...][:, :, 1:2] for r in ml_refs]
    m_all = functools.reduce(jnp.maximum, ms)
    es = [l * jnp.exp(m - m_all) for m, l in zip(ms, ls)]
    tot = functools.reduce(jnp.add, es)
    out_ref[...] = functools.reduce(jnp.add, [o[...] * (e / tot) for o, e in zip(o_refs, es)])


def _merge_tiles(os_, mls, tr):
    r = os_[0].shape[0]
    spec = pl.BlockSpec((tr, 8, LANES), lambda i: (i, 0, 0))
    return pl.pallas_call(
        functools.partial(_merge_tiles_kernel, len(os_)), grid=(r // tr,),
        in_specs=[spec] * (2 * len(os_)), out_specs=spec,
        out_shape=jax.ShapeDtypeStruct((r, 8, LANES), F32),
        compiler_params=_cparams("parallel"), name="dil_merge_sample",
    )(*os_, *mls)


def _seg_mats(width, segments):
    s = np.zeros((width, LANES), np.float32)
    inv = np.ones((1, LANES), np.float32)
    for k, (a, n) in enumerate(segments):
        s[a:a + n, k] = 1.0
        inv[0, k] = 1.0 / n
    return jnp.asarray(s, BF16), jnp.asarray(np.concatenate([s.T, s.T]), BF16), jnp.asarray(inv)


def _pad_heads(wm, heads, dim, slab, offset=0):
    r = wm.shape[0]
    out = jnp.pad(wm.reshape(r, heads, dim), ((0, 0), (0, 0), (offset, slab - dim - offset)))
    return out.reshape(r, heads * slab)


def _mem_q_tables(g_qn_mem):
    seg = [(hd * LANES, MEM_DIM) for hd in range(MEM_HEADS)]
    sm, smt, invm = _seg_mats(MEM_W, seg)
    gm = _pad_heads(jnp.tile(g_qn_mem, MEM_HEADS)[None, :], MEM_HEADS, MEM_DIM, LANES) * MEM_SCALE
    return dict(Sm=sm, SmT=smt, invm=invm, gm=gm)


def _mla_tables(pos, g_qn_nope, g_qn_pe, g_kn_pe):
    hr = ROPE_DIM // 2
    inv = ROPE_THETA ** (-jnp.arange(0, ROPE_DIM, 2, dtype=F32) / ROPE_DIM)
    z = lambda k: jnp.zeros((k,), F32)
    rest = LANES - NOPE_DIM - ROPE_DIM
    fq = jnp.concatenate([z(NOPE_DIM), inv, inv, z(rest)])
    cq_amp = jnp.concatenate([g_qn_nope, g_qn_pe, z(rest)]) * MLA_SCALE
    sq_amp = jnp.concatenate([z(NOPE_DIM), -g_qn_pe[hr:], g_qn_pe[:hr], z(rest)]) * MLA_SCALE
    fk = jnp.concatenate([inv, inv, inv, inv, z(LANES - 2 * ROPE_DIM)])
    ck_amp = jnp.concatenate([g_kn_pe, z(LANES - ROPE_DIM)])
    sk_amp = jnp.concatenate([z(ROPE_DIM), -g_kn_pe[hr:], g_kn_pe[:hr], z(LANES - 2 * ROPE_DIM)])
    p = pos.astype(F32)[:, None]
    aq, ak = p * fq[None, :], p * fk[None, :]
    return (jnp.cos(aq) * cq_amp[None, :], jnp.sin(aq) * sq_amp[None, :],
            jnp.cos(ak) * ck_amp[None, :] + jnp.sin(ak) * sk_amp[None, :])


def _prep_mla(i, j, g_attn, w_o, w_in_a, g_q_a, w_q_b, g_kv_a, w_uk, w_uv, g_qn_mem):
    d = w_in_a.shape[1]
    hr = ROPE_DIM // 2
    wi = w_in_a[j]
    o1, o2, o3 = Q_LORA, Q_LORA + KV_LORA, Q_LORA + KV_LORA + ROPE_DIM
    kpe = wi[:, o2:o3]
    w_in = jnp.concatenate([wi[:, :o2], _pad_heads(wi[:, o3:], MEM_HEADS, MEM_DIM, LANES),
                            kpe, jnp.concatenate([kpe[:, hr:], kpe[:, :hr]], axis=1),
                            jnp.zeros((d, LANES - 2 * ROPE_DIM), F32)], axis=1).astype(BF16)
    hq = NOPE_DIM + ROPE_DIM
    wq3 = w_q_b[j].reshape(Q_LORA, MLA_HEADS, hq)
    wq = _pad_heads(w_q_b[j], MLA_HEADS, hq, LANES).astype(BF16)
    sw = jnp.concatenate([wq3[:, :, NOPE_DIM + hr:], wq3[:, :, NOPE_DIM:NOPE_DIM + hr]], axis=2)
    wq_sw = _pad_heads(sw.reshape(Q_LORA, MLA_HEADS * ROPE_DIM), MLA_HEADS, ROPE_DIM, LANES, NOPE_DIM).astype(BF16)
    seg = []
    for hd in range(MLA_HEADS):
        seg += [(hd * LANES, NOPE_DIM), (hd * LANES + NOPE_DIM, ROPE_DIM)]
    s, st, inv = _seg_mats(MLA_HEADS * LANES, seg)
    wk_c = _pad_heads(w_uk[j].reshape(KV_LORA, MLA_HEADS * NOPE_DIM), MLA_HEADS, NOPE_DIM, LANES).astype(BF16)
    eye = np.zeros((LANES, MLA_HEADS, LANES), np.float32)
    for r in range(ROPE_DIM):
        eye[r, :, NOPE_DIM + r] = 1.0
    wk_p = jnp.asarray(eye.reshape(LANES, MLA_HEADS * LANES), BF16)
    wv4 = w_uv[j].reshape(KV_LORA, MLA_HEADS // 2, 2, MLA_V_DIM)
    lo = ((0, 0), (0, 0), (0, LANES - MLA_V_DIM))
    hi = ((0, 0), (0, 0), (LANES - MLA_V_DIM, 0))
    wv = jnp.stack([jnp.pad(wv4[:, :, 0], lo), jnp.pad(wv4[:, :, 1], hi)], axis=2)
    wv = wv.reshape(KV_LORA, MLA_HEADS * LANES).astype(BF16)
    vone = np.zeros((1, MLA_HEADS, LANES), np.float32)
    vone[0, 0::2, MLA_V_DIM] = 1.0
    vone[0, 1::2, 0] = 1.0
    vone = jnp.asarray(vone.reshape(1, MLA_HEADS * LANES))
    pe_rows = np.zeros((ROPE_DIM, MLA_QCAT), np.float32)
    pe_rows[np.arange(ROPE_DIM), KV_LORA + np.arange(ROPE_DIM)] = 1.0
    wcat = jnp.concatenate([
        jnp.pad(jnp.transpose(w_uk[j], (1, 2, 0)), ((0, 0), (0, 0), (0, MLA_QCAT - KV_LORA))),
        jnp.broadcast_to(jnp.asarray(pe_rows)[None], (MLA_HEADS, ROPE_DIM, MLA_QCAT)),
        jnp.zeros((MLA_HEADS, LANES - hq, MLA_QCAT), F32)], axis=1)
    wvp = jnp.transpose(wv4, (1, 2, 0, 3))
    uvp = jnp.concatenate([jnp.pad(wvp[:, 0], lo), jnp.pad(wvp[:, 1], hi)], axis=1)
    ntok = MLA_HEADS * MLA_V_DIM
    w = dict(g_attn=g_attn[i][None], w_in=w_in, g_q_a=g_q_a[j][None], wq=wq, wq_sw=wq_sw,
             g_kv_a=g_kv_a[j][None], S=s, ST=st, inv=inv, wk_c=wk_c, wk_p=wk_p, wv=wv, vone=vone,
             wcat=wcat.astype(BF16), uv_pairs=uvp.astype(BF16),
             wo_tok=w_o[i][:ntok].astype(BF16),
             wo_mem=_pad_rows(w_o[i][ntok:], MEM_HEADS, MEM_DIM, LANES, MEM_DIM).astype(BF16))
    w.update(_mem_q_tables(g_qn_mem[i]))
    return w


def _pad_rows(wm, heads, dim, slab, offset):
    c = wm.shape[1]
    out = jnp.pad(wm.reshape(heads, dim, c), ((0, 0), (offset, slab - dim - offset), (0, 0)))
    return out.reshape(heads * slab, c)


def _prep_dil(i, j, g_attn, w_o, w_in_b, g_qn_b, g_kn_b, g_qn_mem):
    wi = w_in_b[j]
    gh = N_GROUPS * DIL_HEADS
    nqk = gh * DIL_QK
    wq = _pad_heads(wi[:, :nqk], gh, DIL_QK, LANES)
    wk = wi[:, nqk:2 * nqk].reshape(-1, gh, DIL_QK)
    wv = wi[:, 2 * nqk:2 * nqk + gh * DIL_V].reshape(-1, gh, DIL_V)
    wkv = jnp.concatenate([wk, wv], axis=2).reshape(-1, gh * DIL_ROW)
    wmq = _pad_heads(wi[:, 2 * nqk + gh * DIL_V:], MEM_HEADS, MEM_DIM, LANES)
    w_in = jnp.concatenate([wq, wkv, wmq], axis=1).astype(BF16)
    sq, sqt, invq = _seg_mats(QW, [(k * LANES, DIL_QK) for k in range(gh)])
    sk, skt, invk = _seg_mats(KVW, [(k * DIL_ROW, DIL_QK) for k in range(gh)])
    gq = _pad_heads(jnp.repeat(g_qn_b[j], DIL_HEADS, axis=0).reshape(1, nqk), gh, DIL_QK, LANES) * DIL_SCALE
    gk = _pad_heads(jnp.repeat(g_kn_b[j], DIL_HEADS, axis=0).reshape(1, nqk), gh, DIL_QK, DIL_ROW)
    vmask = _pad_heads(jnp.ones((1, gh * DIL_V), F32), gh, DIL_V, DIL_ROW, DIL_QK)
    ntok = DIL_HEADS * DIL_V
    w = dict(g_attn=g_attn[i][None], w_in=w_in, Sq=sq, SqT=sqt, invq=invq, gq=gq,
             Sk=sk, SkT=skt, invk=invk, gk=gk, vmask=vmask,
             wo_tok=_pad_rows(w_o[i][:ntok], DIL_HEADS, DIL_V, DIL_ROW, DIL_QK).astype(BF16),
             wo_mem=_pad_rows(w_o[i][ntok:], MEM_HEADS, MEM_DIM, LANES, MEM_DIM).astype(BF16))
    w.update(_mem_q_tables(g_qn_mem[i]))
    return w


def _prep_mem(i, g_mem, w_mem_kv, g_kn_mem):
    sk, skt, invk = _seg_mats(MEM_W, [(hd * LANES, MEM_DIM) for hd in range(MEM_HEADS)])
    gk = _pad_heads(jnp.tile(g_kn_mem[i], MEM_HEADS)[None, :], MEM_HEADS, MEM_DIM, LANES)
    vmask = _pad_heads(jnp.ones((1, MEM_HEADS * MEM_DIM), F32), MEM_HEADS, MEM_DIM, LANES, MEM_DIM)
    return dict(g_mem=g_mem[i][None], w_mem=w_mem_kv[i].astype(BF16), Sk=sk, SkT=skt, invk=invk, gk=gk, vmask=vmask)


def _prep_ffn(i, g_ffn, w_up_b, conv_w, conv_b, w_down_b):
    return dict(layer=i, g_ffn=g_ffn[i][None], w_up=w_up_b, conv_w=conv_w[i],
                conv_b=conv_b[i][None], w_down=w_down_b)


def _rel_bucket(dist):
    max_exact = N_BUCKETS // 2
    dd = jnp.maximum(dist.astype(F32), 1.0)
    large = max_exact + (jnp.log(dd / max_exact) / math.log(MAX_DISTANCE / max_exact)
                         * (N_BUCKETS - max_exact)).astype(jnp.int32)
    large = jnp.minimum(large, N_BUCKETS - 1)
    return jnp.where(dist < max_exact, dist, large)


def _dil_bias_tables(rel_bias, tq, t_s):
    band, samp = [], []

    def toeplitz(rev, rows, cols):
        period = cols + rows + 1
        f = jnp.concatenate([rev, jnp.full((DIL_HEADS, period - DIL_KEYS), NEG, F32)], axis=1)
        return jnp.tile(f, (1, rows))[:, :rows * (period - 1)].reshape(DIL_HEADS, rows, period - 1)[:, :, :cols]

    srow = jnp.arange(8)[None, None, None, :] == jnp.arange(DIL_HEADS)[:, None, None, None]
    for g in range(N_GROUPS):
        bk = _rel_bucket(DIL_RATES[g] * jnp.arange(DIL_KEYS, dtype=jnp.int32))
        b = rel_bias[bk][:, g * DIL_HEADS:(g + 1) * DIL_HEADS].T.astype(F32)
        rev = b[:, ::-1]
        band.append(toeplitz(rev, tq, 2 * tq))
        nq, ntok = _dil_classes(DIL_RATES[g], t_s)
        tab = jnp.where(srow, toeplitz(rev, nq, ntok)[..., None], NEG).reshape(DIL_HEADS, nq, ntok * 8)
        tab = jnp.pad(tab.transpose(1, 0, 2), ((0, 0), (0, 8 - DIL_HEADS), (0, 0)), constant_values=NEG)
        samp.append(tab.reshape(nq * 8, ntok * 8))
    return band, samp


def kernel(x_prompt, x_sample, cache_mla, state_win0, state_win1, state_win2, cache_mem, state_conv, page_table, mem_prompt, rel_bias, g_attn, w_o, w_in_a, g_q_a, w_q_b, g_kv_a, w_uk, w_uv, g_qn_nope, g_qn_pe, g_kn_pe, w_in_b, g_qn_b, g_kn_b, g_mem, w_mem_kv, g_qn_mem, g_kn_mem, g_ffn, w_up, conv_w, conv_b, w_down):
    state_wins = (state_win0, state_win1, state_win2)
    n, t_p, d = x_prompt.shape
    b, t_s, _ = x_sample.shape
    n_mem = mem_prompt.shape[1]
    d_ff = w_down.shape[1]
    past = page_table.shape[1] * cache_mla.shape[2]
    mp, ms = n * t_p, b * t_s
    tm_p = min(256, t_p)
    tm_s = min(256, ms)
    tq_band = DIL_KEYS - 1
    assert t_s >= 2 and t_p % tm_p == 0 and ms % tm_s == 0 and tm_s % t_s == 0
    assert all(sw.shape[2] == wdw and wdw == (DIL_KEYS - 1) * r
               for sw, wdw, r in zip(state_wins, DIL_WINDOWS, DIL_RATES))

    xp = x_prompt.reshape(mp, d)
    xs = x_sample.reshape(ms, d)
    mem2 = mem_prompt.reshape(n * n_mem, d)
    pos_p = jnp.arange(t_p)
    pos_s = jnp.tile(past + jnp.arange(t_s), tm_s // t_s)
    band_bias, samp_bias = _dil_bias_tables(rel_bias, tq_band, t_s)
    cache_rows = cache_mem.reshape(cache_mem.shape[0], b, n_mem * MEM_HEADS, 2 * MEM_DIM)

    w_up_b, w_down_b = w_up.astype(BF16), w_down.astype(BF16)
    mem_out, conv_p_out, conv_s_out = [], [], []
    for i in range(2):
        j = i // 2
        if i % 2 == 0:
            w = _prep_mla(i, j, g_attn, w_o, w_in_a, g_q_a, w_q_b, g_kv_a, w_uk, w_uv, g_qn_mem)
            cq, sq, tk = _mla_tables(pos_p, g_qn_nope[j], g_qn_pe[j], g_kn_pe[j])
            q_p, rows_p, mq_p, k_p, v_p = _proj_mla(xp, True, tm_p, w, cq, sq, tk)
            cq, sq, tk = _mla_tables(pos_s, g_qn_nope[j], g_qn_pe[j], g_kn_pe[j])
            qc_s, rows_s, mq_s = _proj_mla(xs, False, tm_s, w, cq, sq, tk)
            hw = MLA_HEADS * LANES
            tok_p = _mla_flash(q_p.reshape(n, t_p, hw), k_p.reshape(n, t_p, hw), v_p.reshape(n, t_p, hw),
                               min(512, t_p)).reshape(mp, -1)
            q3 = qc_s.reshape(b, t_s, MLA_HEADS, MLA_QCAT).transpose(0, 2, 1, 3).reshape(b, MLA_HEADS * t_s, MLA_QCAT)
            ctx = _mla_sample_attn(q3, rows_s.reshape(b, t_s, MLA_ROW), cache_mla[j].transpose(0, 2, 1),
                                   page_table, min(MLA_PAGES_PER_STEP, page_table.shape[1]))
            ctx = ctx.reshape(b, MLA_HEADS, t_s, KV_LORA).transpose(0, 2, 1, 3).reshape(ms, MLA_HEADS * KV_LORA)
            tok_s = _mla_uv(ctx, w['uv_pairs'], tm_s)
            mla_rows_p = rows_p.reshape(1, n, t_p, MLA_ROW)
            mla_rows_s = rows_s.reshape(1, b, t_s, MLA_ROW)
        else:
            w = _prep_dil(i, j, g_attn, w_o, w_in_b, g_qn_b, g_kn_b, g_qn_mem)
            res = _proj_dil(xp.reshape(n, t_p, d), True, tm_p, w)
            mq_p = res[0].reshape(mp, MEM_W)
            nc = DIL_ROW // LANES
            os_, mls = [], []
            for g, dil in enumerate(DIL_RATES):
                l = t_p // dil
                o, ml = _band_attn(res[4 + g].reshape(n * dil, l, DIL_HEADS * LANES),
                                   res[7 + g].reshape(n * dil, l, DIL_SLAB), band_bias[g], tq_band,
                                   max(k for k in (1, 2, 4) if l % (k * tq_band) == 0))
                os_.append(o.reshape(n, dil, l, DIL_SLAB))
                mls.append(ml.reshape(n, dil, l, LANES))
            tok_p = _merge(os_, mls, DIL_RATES, n, t_p, tm_p).reshape(mp, DIL_SLAB)
            win_p = [wt.reshape(n, wt.shape[1], nc, DIL_HEADS, LANES).transpose(0, 1, 3, 2, 4)
                     .reshape(1, n, wt.shape[1], DIL_HEADS, DIL_ROW) for wt in res[1:4]]
            q_s, kv_s, mq_s = _proj_dil(xs.reshape(b, t_s, d), False, tm_s, w)
            nc = DIL_ROW // LANES
            q_t = jnp.pad(q_s.astype(F32).reshape(b, t_s, N_GROUPS, DIL_HEADS, LANES),
                          ((0, 0), (0, 0), (0, 0), (0, 8 - DIL_HEADS), (0, 0))).transpose(2, 0, 1, 3, 4)
            kv_t = kv_s.reshape(b, t_s, N_GROUPS, DIL_HEADS, nc, LANES).transpose(2, 0, 1, 4, 3, 5)
            kv_t = kv_t.reshape(N_GROUPS, b, t_s, 8, LANES)
            os_, mls, win_s = [], [], []
            for g, dil in enumerate(DIL_RATES):
                buf = state_wins[g][j]
                wb = buf.shape[1]
                buf_t = buf.reshape(b, wb, DIL_HEADS, nc, LANES).transpose(0, 1, 3, 2, 4).reshape(b, wb, 8, LANES)
                o, ml, st = _dil_sample(q_t[g], kv_t[g], buf_t, samp_bias[g], dil)
                os_.append(o.reshape(ms, 8, LANES))
                mls.append(ml.reshape(ms, 8, LANES))
                st = st.reshape(b, wb, nc, DIL_HEADS, LANES).transpose(0, 1, 3, 2, 4)
                win_s.append(st.reshape(1, b, wb, DIL_HEADS, DIL_ROW))
            tok_s = _merge_tiles(os_, mls, tm_s).reshape(ms, nc, DIL_HEADS, LANES).transpose(0, 2, 1, 3)
            tok_s = tok_s.reshape(ms, DIL_SLAB).astype(BF16)

        wm = _prep_mem(i, g_mem, w_mem_kv, g_kn_mem)
        mkv = _mem_kv(mem2, wm, min(256, n * n_mem))
        mem_out.append(mkv.reshape(n, n_mem, MEM_HEADS, 2 * MEM_DIM))
        mo_p = _mem_attn(mq_p.reshape(n, t_p, MEM_W), mkv.reshape(n, n_mem, MEM_W), min(512, t_p)).reshape(mp, MEM_W)
        mo_s = _mem_attn_rows(mq_s.reshape(b, t_s * MEM_HEADS, LANES), cache_rows, i, min(8, b)).reshape(ms, MEM_W)
        xp = _out_proj(xp, tok_p, mo_p, w['wo_tok'], w['wo_mem'], 2 * tm_p if mp % (2 * tm_p) == 0 else tm_p)
        xs = _out_proj(xs, tok_s, mo_s, w['wo_tok'], w['wo_mem'], tm_s)

        wf = _prep_ffn(i, g_ffn, w_up_b, conv_w, conv_b, w_down_b)
        xp, tails = _ffn(xp, t_p, wf, tm_p)
        conv_p_out.append(tails.reshape(n, t_p // tm_p, 8, d_ff)[:, -1, 6:, :])
        prev = state_conv[i]
        zeros = jnp.zeros((b, t_s - 2, d_ff), F32)
        p2 = jnp.concatenate([prev, zeros], axis=1).reshape(ms, d_ff)
        p1 = jnp.concatenate([prev[:, 1:], zeros, jnp.zeros((b, 1, d_ff), F32)], axis=1).reshape(ms, d_ff)
        xs, gates = _ffn(xs, t_s, wf, tm_s, prev=(p1, p2))
        conv_s_out.append(gates.reshape(b, t_s, d_ff)[:, t_s - 2:, :])

    return (xp.reshape(n, t_p, d), xs.reshape(b, t_s, d), mla_rows_p, mla_rows_s,
            win_p[0], win_p[1], win_p[2], win_s[0], win_s[1], win_s[2],
            jnp.stack(mem_out), jnp.stack(conv_p_out), jnp.stack(conv_s_out))
```

```python
import functools
import math

import jax
import jax.numpy as jnp
import numpy as np
from jax import lax
from jax.experimental import pallas as pl
from jax.experimental.pallas import tpu as pltpu

F32 = jnp.float32
BF16 = jnp.bfloat16

EPS = 1e-6
LANES = 128
VMEM_LIMIT = 48 * 1024 * 1024

MLA_HEADS = 12
NOPE_DIM = 64
ROPE_DIM = 32
MLA_V_DIM = 64
Q_LORA = 256
KV_LORA = 256
MLA_ROW = KV_LORA + ROPE_DIM
ROPE_THETA = 10000.0
MLA_SCALE = 1.0 / math.sqrt(NOPE_DIM + ROPE_DIM)
MLA_QCAT = 384
MLA_PAGES_PER_STEP = 128
MLA_PAGE_CHUNK = 16
DIL_SAMPLE_STEP_TOKENS = 1024
FLASH_HEAD_GROUP = 4

DIL_WINDOWS = (128, 512, 2048)
DIL_RATES = (1, 4, 16)
N_GROUPS = 3
DIL_HEADS = 4
DIL_QK = 64
DIL_V = 192
DIL_KEYS = DIL_WINDOWS[0] // DIL_RATES[0] + 1
DIL_SCALE = 1.0 / math.sqrt(DIL_QK)
DIL_ROW = DIL_QK + DIL_V
DIL_SLAB = DIL_HEADS * DIL_ROW

MEM_HEADS = 4
MEM_DIM = 64
MEM_SCALE = 1.0 / math.sqrt(MEM_DIM)
MEM_W = MEM_HEADS * LANES

N_BUCKETS = 32
MAX_DISTANCE = 2048
NEG = -1e30


def _cparams(*sem):
    return pltpu.CompilerParams(dimension_semantics=sem, vmem_limit_bytes=VMEM_LIMIT)


def _dot(a, b):
    return jnp.dot(a, b, preferred_element_type=F32)


def _dot_nt(a, b):
    return lax.dot_general(a, b, (((1,), (1,)), ((), ())), preferred_element_type=F32)


def _rms_rows(x, g):
    return x * lax.rsqrt(jnp.mean(x * x, axis=-1, keepdims=True) + EPS) * g


def _seg_scale(x, s_ref, st2_ref, inv_ref):
    ss = _dot((x * x).astype(BF16), s_ref[...])
    r = lax.rsqrt(ss * inv_ref[...] + EPS)
    hi = r.astype(BF16)
    lo = (r - hi.astype(F32)).astype(BF16)
    return _dot(jnp.concatenate([hi, lo], axis=1), st2_ref[...])


def _cols(vals, width):
    rows = vals[0].shape[0]
    lane = lax.broadcasted_iota(jnp.int32, (rows, width), 1)
    out = jnp.zeros((rows, width), F32)
    for k, v in enumerate(vals):
        out = jnp.where(lane == k, v, out)
    return out


def _full(shape):
    return pl.BlockSpec(shape, lambda *_: (0,) * len(shape))


def _proj_mla_kernel(prompt, x_ref, ga_ref, win_ref, gqa_ref, wq_ref, wqs_ref, gkv_ref,
                     cq_ref, sq_ref, tk_ref, s_ref, st_ref, inv_ref,
                     sm_ref, smt_ref, invm_ref, gm_ref, *rest):
    if prompt:
        wkc_ref, wkp_ref, wv_ref, vone_ref, q_out, rows_out, mq_out, k_out, v_out = rest
    else:
        wcat_ref, q_out, rows_out, mq_out = rest
    h = _rms_rows(x_ref[...], ga_ref[...]).astype(BF16)
    z = _dot(h, win_ref[...])
    qa = _rms_rows(z[:, :Q_LORA], gqa_ref[...]).astype(BF16)
    q = _dot(qa, wq_ref[...])
    qs = _dot(qa, wqs_ref[...])
    scale = _seg_scale(q, s_ref, st_ref, inv_ref)
    cq = jnp.concatenate([cq_ref[...]] * MLA_HEADS, axis=1)
    sq = jnp.concatenate([sq_ref[...]] * MLA_HEADS, axis=1)
    qf = ((q * cq + qs * sq) * scale).astype(BF16)
    c = _rms_rows(z[:, Q_LORA:Q_LORA + KV_LORA], gkv_ref[...])
    zk = z[:, 1024:1152]
    kp = zk[:, :ROPE_DIM]
    rk = lax.rsqrt(jnp.mean(kp * kp, axis=-1, keepdims=True) + EPS)
    t = zk * tk_ref[...] * rk
    kr = t + pltpu.roll(t, LANES - ROPE_DIM, 1)
    rows_out[:, :KV_LORA] = c
    rows_out[:, KV_LORA:] = kr[:, :ROPE_DIM]
    mq = z[:, 512:1024]
    mscale = _seg_scale(mq, sm_ref, smt_ref, invm_ref)
    mq_out[...] = (mq * mscale * gm_ref[...]).astype(BF16)
    if prompt:
        q_out[...] = qf
        cb = c.astype(BF16)
        k_out[...] = (_dot(cb, wkc_ref[...]) + _dot(kr.astype(BF16), wkp_ref[...])).astype(BF16)
        v_out[...] = (_dot(cb, wv_ref[...]) + vone_ref[...]).astype(BF16)
    else:
        for hd in range(MLA_HEADS):
            q_out[:, hd * MLA_QCAT:(hd + 1) * MLA_QCAT] = _dot(
                qf[:, hd * LANES:(hd + 1) * LANES], wcat_ref[hd]).astype(BF16)


def _proj_mla(x, prompt, tm, w, cq, sq, tk):
    m, d = x.shape
    hw = MLA_HEADS * LANES
    row = lambda width: pl.BlockSpec((tm, width), lambda i: (i, 0))
    args = [x, w['g_attn'], w['w_in'], w['g_q_a'], w['wq'], w['wq_sw'], w['g_kv_a'],
            cq, sq, tk, w['S'], w['ST'], w['inv'], w['Sm'], w['SmT'], w['invm'], w['gm']]
    ptiles = cq.shape[0] // tm
    pos = pl.BlockSpec((tm, LANES), lambda i: (i % ptiles, 0))
    specs = [row(d)] + [_full(a.shape) for a in args[1:7]] + [pos, pos, pos] \
        + [_full(a.shape) for a in args[10:]]
    if prompt:
        extra = [w['wk_c'], w['wk_p'], w['wv'], w['vone']]
        outs = [jax.ShapeDtypeStruct((m, hw), BF16), jax.ShapeDtypeStruct((m, MLA_ROW), F32),
                jax.ShapeDtypeStruct((m, MEM_W), BF16), jax.ShapeDtypeStruct((m, hw), BF16),
                jax.ShapeDtypeStruct((m, hw), BF16)]
        ospecs = [row(hw), row(MLA_ROW), row(MEM_W), row(hw), row(hw)]
    else:
        extra = [w['wcat']]
        outs = [jax.ShapeDtypeStruct((m, MLA_HEADS * MLA_QCAT), BF16),
                jax.ShapeDtypeStruct((m, MLA_ROW), F32), jax.ShapeDtypeStruct((m, MEM_W), BF16)]
        ospecs = [row(MLA_HEADS * MLA_QCAT), row(MLA_ROW), row(MEM_W)]
    return pl.pallas_call(
        functools.partial(_proj_mla_kernel, prompt),
        grid=(m // tm,),
        in_specs=specs + [_full(a.shape) for a in extra],
        out_specs=ospecs, out_shape=outs,
        compiler_params=_cparams("parallel"),
        name="proj_mla_prompt" if prompt else "proj_mla_sample",
    )(*args, *extra)


def _flash_kernel(tq, qi_ref, kj_ref, q_ref, k_ref, v_ref, o_ref, m_scr, acc_scr):
    i, j = qi_ref[pl.program_id(1)], kj_ref[pl.program_id(1)]

    @pl.when(j == 0)
    def _():
        m_scr[...] = jnp.full(m_scr.shape, NEG, F32)
        acc_scr[...] = jnp.zeros(acc_scr.shape, F32)

    def step(masked):
        if masked:
            keep = lax.broadcasted_iota(jnp.int32, (tq, tq), 1) <= lax.broadcasted_iota(jnp.int32, (tq, tq), 0)
        for h0 in range(0, MLA_HEADS, FLASH_HEAD_GROUP):
            heads = range(h0, h0 + FLASH_HEAD_GROUP)
            scores = [_dot_nt(q_ref[0, :, hd * LANES:(hd + 1) * LANES], k_ref[0, :, hd * LANES:(hd + 1) * LANES])
                      for hd in heads]
            for hd, s in zip(heads, scores):
                if masked:
                    s = jnp.where(keep, s, NEG)
                m_old = m_scr[hd]
                m_new = jnp.maximum(m_old, jnp.max(s, axis=-1, keepdims=True))
                alpha = jnp.exp(m_old - m_new)
                p = jnp.exp(s - jnp.concatenate([m_new] * (tq // LANES), axis=1))
                acc_scr[hd] = alpha * acc_scr[hd] + _dot(p.astype(BF16), v_ref[0, :, hd * LANES:(hd + 1) * LANES])
                m_scr[hd] = m_new

    @pl.when(j < i)
    def _():
        step(False)

    @pl.when(j == i)
    def _():
        step(True)
        low = lax.broadcasted_iota(jnp.int32, (tq, LANES), 1) < MLA_V_DIM
        for pr in range(MLA_HEADS // 2):
            ae, ao = acc_scr[2 * pr], acc_scr[2 * pr + 1]
            o_ref[0, :, pr * LANES:(pr + 1) * LANES] = jnp.where(
                low, ae / ae[:, MLA_V_DIM:MLA_V_DIM + 1], ao / ao[:, 0:1]).astype(BF16)


def _mla_flash(q, k, v, tq):
    n, t, hw = q.shape
    nq = t // tq
    pairs = [(i, j) for i in range(nq) for j in range(i + 1)]
    qi = jnp.asarray([p[0] for p in pairs], jnp.int32)
    kj = jnp.asarray([p[1] for p in pairs], jnp.int32)
    return pl.pallas_call(
        functools.partial(_flash_kernel, tq),
        grid_spec=pltpu.PrefetchScalarGridSpec(
            num_scalar_prefetch=2,
            grid=(n, len(pairs)),
            in_specs=[pl.BlockSpec((1, tq, hw), lambda b, s, qi, kj: (b, qi[s], 0)),
                      pl.BlockSpec((1, tq, hw), lambda b, s, qi, kj: (b, kj[s], 0)),
                      pl.BlockSpec((1, tq, hw), lambda b, s, qi, kj: (b, kj[s], 0))],
            out_specs=pl.BlockSpec((1, tq, MLA_HEADS * MLA_V_DIM), lambda b, s, qi, kj: (b, qi[s], 0)),
            scratch_shapes=[pltpu.VMEM((MLA_HEADS, tq, LANES), F32), pltpu.VMEM((MLA_HEADS, tq, LANES), F32)]),
        out_shape=jax.ShapeDtypeStruct((n, t, MLA_HEADS * MLA_V_DIM), BF16),
        compiler_params=_cparams("parallel", "arbitrary"),
        name="mla_flash_prompt",
    )(qi, kj, q, k, v)


def _mla_sample_kernel(npg, t_s, pt_ref, q_ref, new_ref, *rest):
    pages, (o_ref, m_scr, l_scr, acc_scr) = rest[:npg], rest[npg:]
    c = pl.program_id(1)

    @pl.when(c == 0)
    def _():
        m_scr[...] = jnp.full(m_scr.shape, NEG, F32)
        l_scr[...] = jnp.zeros(l_scr.shape, F32)
        acc_scr[...] = jnp.zeros(acc_scr.shape, F32)

    q = q_ref[0][:, :MLA_ROW]

    def update(state, s, pv):
        m_old, l_old, acc = state
        m_new = jnp.maximum(m_old, jnp.max(s, axis=-1, keepdims=True))
        alpha = jnp.exp(m_old - m_new)
        p = jnp.exp(s - m_new[:, 0:1])
        l_new = alpha * l_old + jnp.sum(p, axis=-1, keepdims=True)
        acc = jnp.concatenate([alpha] * (KV_LORA // LANES), axis=1) * acc + pv(p.astype(BF16))
        return m_new, l_new, acc

    def load(c0):
        return jnp.concatenate([pg[0].astype(BF16) for pg in pages[c0:c0 + MLA_PAGE_CHUNK]], axis=1)

    starts = list(range(0, npg, MLA_PAGE_CHUNK))
    state = (m_scr[...], l_scr[...], acc_scr[...])
    kts = {0: load(starts[0])}
    scores = {0: _dot(q, kts[0])}
    if len(starts) > 1:
        kts[1] = load(starts[1])
    for ch in range(len(starts)):
        if ch + 1 < len(starts):
            scores[ch + 1] = _dot(q, kts[ch + 1])
        if ch + 2 < len(starts):
            kts[ch + 2] = load(starts[ch + 2])
        state = update(state, scores.pop(ch), lambda p, kt=kts.pop(ch): _dot_nt(p, kt[:KV_LORA]))
    m_scr[...], l_scr[...], acc_scr[...] = state

    @pl.when(c == pl.num_programs(1) - 1)
    def _():
        nb = new_ref[0].astype(BF16)
        s = _dot_nt(q, nb)
        rows = q.shape[0]
        tq = lax.broadcasted_iota(jnp.int32, (rows, t_s), 0) % t_s
        s = jnp.where(lax.broadcasted_iota(jnp.int32, (rows, t_s), 1) <= tq, s, NEG)
        _, l_fin, acc = update((m_scr[...], l_scr[...], acc_scr[...]), s, lambda p: _dot(p, nb[:, :KV_LORA]))
        o_ref[0] = (acc / jnp.concatenate([l_fin] * (KV_LORA // LANES), axis=1)).astype(BF16)


def _mla_sample_attn(q3, rows3, pool_t, page_table, npg):
    b, rows, _ = q3.shape
    t_s = rows3.shape[1]
    n_pages = page_table.shape[1]
    psz = pool_t.shape[2]

    def page_spec(k):
        return pl.BlockSpec((1, MLA_ROW, psz), lambda s, c, pt: (pt[s * n_pages + c * npg + k], 0, 0))

    return pl.pallas_call(
        functools.partial(_mla_sample_kernel, npg, t_s),
        grid_spec=pltpu.PrefetchScalarGridSpec(
            num_scalar_prefetch=1,
            grid=(b, n_pages // npg),
            in_specs=[pl.BlockSpec((1, rows, MLA_QCAT), lambda s, c, pt: (s, 0, 0)),
                      pl.BlockSpec((1, t_s, MLA_ROW), lambda s, c, pt: (s, 0, 0))]
            + [page_spec(k) for k in range(npg)],
            out_specs=pl.BlockSpec((1, rows, KV_LORA), lambda s, c, pt: (s, 0, 0)),
            scratch_shapes=[pltpu.VMEM((rows, LANES), F32), pltpu.VMEM((rows, LANES), F32),
                            pltpu.VMEM((rows, KV_LORA), F32)]),
        out_shape=jax.ShapeDtypeStruct((b, rows, KV_LORA), BF16),
        compiler_params=_cparams("parallel", "arbitrary"),
        name="mla_sample_attn",
    )(page_table.reshape(-1), q3, rows3, *([pool_t] * npg))


def _uv_kernel(ctx_ref, w_ref, o_ref):
    for pr in range(MLA_HEADS // 2):
        o_ref[:, pr * LANES:(pr + 1) * LANES] = _dot(
            ctx_ref[:, pr * 2 * KV_LORA:(pr + 1) * 2 * KV_LORA], w_ref[pr]).astype(BF16)


def _mla_uv(ctx, w_pairs, tm):
    m = ctx.shape[0]
    return pl.pallas_call(
        _uv_kernel, grid=(m // tm,),
        in_specs=[pl.BlockSpec((tm, ctx.shape[1]), lambda i: (i, 0)), _full(w_pairs.shape)],
        out_specs=pl.BlockSpec((tm, MLA_HEADS * MLA_V_DIM), lambda i: (i, 0)),
        out_shape=jax.ShapeDtypeStruct((m, MLA_HEADS * MLA_V_DIM), BF16),
        compiler_params=_cparams("parallel"), name="mla_sample_uv",
    )(ctx, w_pairs)


def _mem_kv_kernel(x_ref, g_ref, w_ref, s_ref, st_ref, inv_ref, gk_ref, vm_ref, o_ref):
    h = _rms_rows(x_ref[...], g_ref[...]).astype(BF16)
    kv = _dot(h, w_ref[...])
    scale = _seg_scale(kv, s_ref, st_ref, inv_ref) * gk_ref[...] + vm_ref[...]
    o_ref[...] = kv * scale


def _mem_kv(mem2, w, tm):
    m, d = mem2.shape
    args = [mem2, w['g_mem'], w['w_mem'], w['Sk'], w['SkT'], w['invk'], w['gk'], w['vmask']]
    return pl.pallas_call(
        _mem_kv_kernel, grid=(m // tm,),
        in_specs=[pl.BlockSpec((tm, d), lambda i: (i, 0))] + [_full(a.shape) for a in args[1:]],
        out_specs=pl.BlockSpec((tm, MEM_W), lambda i: (i, 0)),
        out_shape=jax.ShapeDtypeStruct((m, MEM_W), F32),
        compiler_params=_cparams("parallel"), name="mem_kv",
    )(*args)


def _mem_attn_body(q, kv):
    outs = []
    scores = [_dot_nt(q[:, hd * LANES:(hd + 1) * LANES], kv[:, hd * LANES:(hd + 1) * LANES])
              for hd in range(MEM_HEADS)]
    for hd, s in enumerate(scores):
        sl = slice(hd * LANES, (hd + 1) * LANES)
        p = jnp.exp(s - jnp.max(s, axis=-1, keepdims=True))
        p = p / jnp.sum(p, axis=-1, keepdims=True)
        outs.append(_dot(p.astype(BF16), kv[:, sl]))
    return jnp.concatenate(outs, axis=-1)


def _mem_attn_kernel(q_ref, kv_ref, o_ref):
    o_ref[0] = _mem_attn_body(q_ref[0], kv_ref[0].astype(BF16)).astype(BF16)


def _mem_attn(q3, kv3, tq):
    n, t, _ = q3.shape
    return pl.pallas_call(
        _mem_attn_kernel, grid=(n, t // tq),
        in_specs=[pl.BlockSpec((1, tq, MEM_W), lambda b, i: (b, i, 0)),
                  pl.BlockSpec((1, kv3.shape[1], MEM_W), lambda b, i: (b, 0, 0))],
        out_specs=pl.BlockSpec((1, tq, MEM_W), lambda b, i: (b, i, 0)),
        out_shape=jax.ShapeDtypeStruct((n, t, MEM_W), BF16),
        compiler_params=_cparams("parallel", "parallel"), name="mem_attn",
    )(q3, kv3)


def _mem_rows_kernel(sb, q_ref, kv_ref, o_ref):
    nq, nkv = q_ref.shape[1], kv_ref.shape[2]
    same = (lax.broadcasted_iota(jnp.int32, (nq, nkv), 0) % MEM_HEADS
            == lax.broadcasted_iota(jnp.int32, (nq, nkv), 1) % MEM_HEADS)
    kvs = [kv_ref[0, s].astype(BF16) for s in range(sb)]
    scores = [_dot_nt(q_ref[s], kvs[s]) for s in range(sb)]
    for s in range(sb):
        sc = jnp.where(same, scores[s], NEG)
        p = jnp.exp(sc - jnp.max(sc, axis=-1, keepdims=True))
        p = p / jnp.sum(p, axis=-1, keepdims=True)
        o_ref[s] = _dot(p.astype(BF16), kvs[s]).astype(BF16)


def _mem_attn_rows(q3, cache4, layer, sb):
    b, nq, _ = q3.shape
    return pl.pallas_call(
        functools.partial(_mem_rows_kernel, sb), grid=(b // sb,),
        in_specs=[pl.BlockSpec((sb, nq, LANES), lambda i: (i, 0, 0)),
                  pl.BlockSpec((1, sb, cache4.shape[2], LANES), lambda i: (layer, i, 0, 0))],
        out_specs=pl.BlockSpec((sb, nq, LANES), lambda i: (i, 0, 0)),
        out_shape=jax.ShapeDtypeStruct((b, nq, LANES), BF16),
        compiler_params=_cparams("parallel"), name="mem_attn_sample",
    )(q3, cache4)


def _out_proj_kernel(x_ref, a_ref, b_ref, wa_ref, wb_ref, o_ref):
    o_ref[...] = x_ref[...] + _dot(a_ref[...], wa_ref[...]) + _dot(b_ref[...], wb_ref[...])


def _out_proj(x, a, b, wa, wb, tm):
    m, d = x.shape
    row = lambda width: pl.BlockSpec((tm, width), lambda i: (i, 0))
    return pl.pallas_call(
        _out_proj_kernel, grid=(m // tm,),
        in_specs=[row(d), row(a.shape[1]), row(b.shape[1]), _full(wa.shape), _full(wb.shape)],
        out_specs=row(d), out_shape=jax.ShapeDtypeStruct((m, d), F32),
        compiler_params=_cparams("parallel"), name="out_proj",
    )(x, a, b, wa, wb)


def _ffn_kernel(seq_tiles, t_s, d_ff, x_ref, g_ref, wu_ref, cw_ref, cb_ref, wd_ref, *rest):
    if seq_tiles:
        o_ref, st_ref, g_scr = rest
    else:
        p1_ref, p2_ref, o_ref, st_ref = rest
    x = x_ref[...]
    tm = x.shape[0]
    h = _rms_rows(x, g_ref[...]).astype(BF16)
    u = _dot(h, wu_ref[...])
    gate, val = u[:, :d_ff], u[:, d_ff:]
    if seq_tiles:
        @pl.when(pl.program_id(0) % seq_tiles == 0)
        def _():
            g_scr[0:8, :] = jnp.zeros((8, d_ff), F32)
        g_scr[8:, :] = gate
        g1 = g_scr[7:tm + 7, :]
        g2 = g_scr[6:tm + 6, :]
        tail = gate[tm - 8:, :]
        g_scr[0:8, :] = tail
        st_ref[...] = tail
    else:
        tpos = lax.broadcasted_iota(jnp.int32, (tm, 1), 0) % t_s
        g1 = jnp.where(tpos >= 1, pltpu.roll(gate, 1, 0), p1_ref[...])
        g2 = jnp.where(tpos >= 2, pltpu.roll(gate, 2, 0), p2_ref[...])
        st_ref[...] = gate
    cw = cw_ref[...]
    conv = cb_ref[...] + g2 * cw[0:1] + g1 * cw[1:2] + gate * cw[2:3]
    act = (conv * (1.0 / (1.0 + jnp.exp(-conv))) * val).astype(BF16)
    o_ref[...] = x + _dot(act, wd_ref[...])


def _ffn(x, seq_len, w, tm, prev=None):
    m, d = x.shape
    d_ff = w['w_down'].shape[1]
    layer = w['layer']
    row = lambda width: pl.BlockSpec((tm, width), lambda i: (i, 0))
    single = dict(pipeline_mode=pl.Buffered(1))
    wspecs = [_full(w['g_ffn'].shape),
              pl.BlockSpec((None,) + w['w_up'].shape[1:], lambda i: (layer, 0, 0), **single),
              _full(w['conv_w'].shape), _full(w['conv_b'].shape),
              pl.BlockSpec((None,) + w['w_down'].shape[1:], lambda i: (layer, 0, 0), **single)]
    wargs = [w['g_ffn'], w['w_up'], w['conv_w'], w['conv_b'], w['w_down']]
    if prev is None:
        seq_tiles = seq_len // tm
        return pl.pallas_call(
            functools.partial(_ffn_kernel, seq_tiles, 0, d_ff), grid=(m // tm,),
            in_specs=[row(d)] + wspecs,
            out_specs=[row(d), pl.BlockSpec((8, d_ff), lambda i: (i, 0))],
            out_shape=[jax.ShapeDtypeStruct((m, d), F32),
                       jax.ShapeDtypeStruct((m // tm * 8, d_ff), F32)],
            scratch_shapes=[pltpu.VMEM((tm + 8, d_ff), F32)],
            compiler_params=_cparams("arbitrary"), name="ffn_prompt",
        )(x, *wargs)
    p1, p2 = prev
    return pl.pallas_call(
        functools.partial(_ffn_kernel, 0, seq_len, d_ff), grid=(m // tm,),
        in_specs=[row(d)] + wspecs + [row(d_ff), row(d_ff)],
        out_specs=[row(d), row(d_ff)],
        out_shape=[jax.ShapeDtypeStruct((m, d), F32), jax.ShapeDtypeStruct((m, d_ff), F32)],
        compiler_params=_cparams("parallel"), name="ffn_sample",
    )(x, *wargs, p1, p2)


QW = N_GROUPS * DIL_HEADS * LANES
KVW = N_GROUPS * DIL_SLAB


def _proj_dil_kernel(prompt, win_skips, x_ref, ga_ref, win_ref, sq_ref, sqt_ref, invq_ref, gq_ref,
                     sk_ref, skt_ref, invk_ref, gk_ref, vm_ref,
                     sm_ref, smt_ref, invm_ref, gm_ref, *rest):
    x = x_ref[0] if prompt else x_ref[...]
    h = _rms_rows(x, ga_ref[...]).astype(BF16)
    z = _dot(h, win_ref[...])
    q = z[:, :QW]
    qn = q * _seg_scale(q, sq_ref, sqt_ref, invq_ref) * gq_ref[...]
    kv = z[:, QW:QW + KVW]
    kvn = kv * (_seg_scale(kv, sk_ref, skt_ref, invk_ref) * gk_ref[...] + vm_ref[...])
    mq = z[:, QW + KVW:]
    mqn = (mq * _seg_scale(mq, sm_ref, smt_ref, invm_ref) * gm_ref[...]).astype(BF16)
    if not prompt:
        q_out, kv_out, mq_out = rest
        q_out[...] = qn.astype(BF16)
        kv_out[...] = kvn
        mq_out[...] = mqn
        return
    mq_out, win_outs = rest[0], rest[1:4]
    q_outs, kvb_outs, q_scr, kv_scr = rest[4:7], rest[7:10], rest[10], rest[11]
    mq_out[0] = mqn
    for c in range(QW // LANES):
        q_scr[c] = qn[:, c * LANES:(c + 1) * LANES]
    for c in range(KVW // LANES):
        kv_scr[c] = kvn[:, c * LANES:(c + 1) * LANES]
    tm = q.shape[0]
    qc, kc = DIL_HEADS, DIL_SLAB // LANES
    for g in range(N_GROUPS):
        rows = win_outs[g].shape[1]

        @pl.when(pl.program_id(1) >= win_skips[g])
        def _(g=g, rows=rows):
            for s in range(8):
                chunk = g * kc + (s % DIL_HEADS) * (DIL_ROW // LANES) + s // DIL_HEADS
                win_outs[g][0, :, s, :] = kv_scr[chunk, tm - rows:tm, :]
    for g, d in enumerate(DIL_RATES):
        for r in range(d):
            rows = pl.ds(r, tm // d, stride=d) if d > 1 else slice(None)
            for c in range(qc):
                q_outs[g][0, r, :, c * LANES:(c + 1) * LANES] = q_scr[g * qc + c, rows, :].astype(BF16)
            for c in range(kc):
                kvb_outs[g][0, r, :, c * LANES:(c + 1) * LANES] = kv_scr[g * kc + c, rows, :].astype(BF16)


def _proj_dil(x3, prompt, tm, w):
    n, t, d = x3.shape
    args = [x3, w['g_attn'], w['w_in'], w['Sq'], w['SqT'], w['invq'], w['gq'],
            w['Sk'], w['SkT'], w['invk'], w['gk'], w['vmask'], w['Sm'], w['SmT'], w['invm'], w['gm']]
    if not prompt:
        x2 = x3.reshape(n * t, d)
        m = n * t
        row = lambda width: pl.BlockSpec((tm, width), lambda i: (i, 0))
        return pl.pallas_call(
            functools.partial(_proj_dil_kernel, False, None), grid=(m // tm,),
            in_specs=[row(d)] + [_full(a.shape) for a in args[1:]],
            out_specs=[row(QW), row(KVW), row(MEM_W)],
            out_shape=[jax.ShapeDtypeStruct((m, QW), BF16), jax.ShapeDtypeStruct((m, KVW), F32),
                       jax.ShapeDtypeStruct((m, MEM_W), BF16)],
            compiler_params=_cparams("parallel"), name="proj_dil_sample",
        )(x2, *args[1:])
    row3 = lambda width: pl.BlockSpec((1, tm, width), lambda b, i: (b, i, 0))
    outs = [jax.ShapeDtypeStruct((n, t, MEM_W), BF16)]
    ospecs = [row3(MEM_W)]
    skips = []
    for wdw in DIL_WINDOWS:
        tail = min(wdw, t)
        rows = min(tail, tm)
        skip = (t - tail) // tm
        skips.append(skip)
        outs.append(jax.ShapeDtypeStruct((n, tail, 8, LANES), F32))
        ospecs.append(pl.BlockSpec((1, rows, 8, LANES),
                                   lambda b, i, skip=skip, last=tail // rows - 1:
                                   (b, jnp.clip(i - skip, 0, last), 0, 0)))
    for width in (DIL_HEADS * LANES, DIL_SLAB):
        for dil in DIL_RATES:
            outs.append(jax.ShapeDtypeStruct((n, dil, t // dil, width), BF16))
            ospecs.append(pl.BlockSpec((1, dil, tm // dil, width), lambda b, i: (b, 0, i, 0)))
    return pl.pallas_call(
        functools.partial(_proj_dil_kernel, True, tuple(skips)), grid=(n, t // tm),
        in_specs=[row3(d)] + [_full(a.shape) for a in args[1:]],
        out_specs=ospecs, out_shape=outs,
        scratch_shapes=[pltpu.VMEM((QW // LANES, tm, LANES), F32), pltpu.VMEM((KVW // LANES, tm, LANES), F32)],
        compiler_params=_cparams("parallel", "arbitrary"), name="proj_dil_prompt",
    )(*args)


def _band_kernel(tq, nqb, q_ref, kp_ref, kc_ref, bias_ref, o_ref, ml_ref):
    i = pl.program_id(1)
    kcat = jnp.concatenate([kp_ref[0], kc_ref[0]], axis=0)
    first = jnp.logical_and(i == 0, lax.broadcasted_iota(jnp.int32, (tq, 2 * tq), 1) < tq)
    work = [(j, hd) for j in range(nqb) for hd in range(DIL_HEADS)]
    scores = [_dot_nt(q_ref[0, j * tq:(j + 1) * tq, hd * LANES:(hd + 1) * LANES],
                      kcat[j * tq:(j + 2) * tq, hd * DIL_ROW:hd * DIL_ROW + LANES]) for j, hd in work]
    stats = [[] for _ in range(nqb)]
    for (j, hd), s in zip(work, scores):
        s = s + bias_ref[hd]
        if j == 0:
            s = jnp.where(first, NEG, s)
        m = jnp.max(s, axis=-1, keepdims=True)
        p = jnp.exp(s - m)
        l = jnp.sum(p, axis=-1, keepdims=True)
        o_ref[0, j * tq:(j + 1) * tq, hd * DIL_ROW:(hd + 1) * DIL_ROW] = _dot(
            (p / l).astype(BF16), kcat[j * tq:(j + 2) * tq, hd * DIL_ROW:(hd + 1) * DIL_ROW]).astype(BF16)
        stats[j].append((m, l))
    for j in range(nqb):
        ml_ref[0, j * tq:(j + 1) * tq] = _cols([m for m, _ in stats[j]] + [l for _, l in stats[j]], LANES)


def _band_attn(q, kv, bias, tq, nqb):
    s, l, _ = q.shape
    tb = tq * nqb
    return pl.pallas_call(
        functools.partial(_band_kernel, tq, nqb), grid=(s, l // tb),
        in_specs=[pl.BlockSpec((1, tb, DIL_HEADS * LANES), lambda b, i: (b, i, 0)),
                  pl.BlockSpec((1, tq, DIL_SLAB), lambda b, i: (b, jnp.maximum(i * nqb - 1, 0), 0)),
                  pl.BlockSpec((1, tb, DIL_SLAB), lambda b, i: (b, i, 0)),
                  _full(bias.shape)],
        out_specs=[pl.BlockSpec((1, tb, DIL_SLAB), lambda b, i: (b, i, 0)),
                   pl.BlockSpec((1, tb, LANES), lambda b, i: (b, i, 0))],
        out_shape=[jax.ShapeDtypeStruct((s, l, DIL_SLAB), BF16),
                   jax.ShapeDtypeStruct((s, l, LANES), F32)],
        compiler_params=_cparams("parallel", "parallel"), name="dil_band_attn",
    )(q, kv, kv, bias)


def _merge_kernel(rates, *refs):
    ng = len(rates)
    o_refs, ml_refs, out_ref = refs[:ng], refs[ng:2 * ng], refs[2 * ng]
    scr = refs[2 * ng + 1:]
    nch = DIL_SLAB // LANES
    chunks, mls = [], []
    for g, d in enumerate(rates):
        if d == 1:
            chunks.append([o_refs[g][0, 0, :, c * LANES:(c + 1) * LANES].astype(F32) for c in range(nch)])
            mls.append(ml_refs[g][0, 0])
        else:
            o_scr, ml_scr = scr[2 * g], scr[2 * g + 1]
            tm = ml_scr.shape[0]
            for r in range(d):
                rows = pl.ds(r, tm // d, stride=d)
                for c in range(nch):
                    o_scr[c, rows, :] = o_refs[g][0, r, :, c * LANES:(c + 1) * LANES].astype(F32)
                ml_scr[rows, :] = ml_refs[g][0, r]
            chunks.append([o_scr[c] for c in range(nch)])
            mls.append(ml_scr[...])
    m_all = functools.reduce(jnp.maximum, [ml[:, :DIL_HEADS] for ml in mls])
    es = [ml[:, DIL_HEADS:2 * DIL_HEADS] * jnp.exp(ml[:, :DIL_HEADS] - m_all) for ml in mls]
    tot = functools.reduce(jnp.add, es)
    ws = [e / tot for e in es]
    for c in range(nch):
        hd = c // (DIL_ROW // LANES)
        acc = None
        for g in range(ng):
            term = chunks[g][c] * ws[g][:, hd:hd + 1]
            acc = term if acc is None else acc + term
        out_ref[0, :, c * LANES:(c + 1) * LANES] = acc.astype(BF16)


def _merge(os_, mls, rates, n, t, tm):
    specs = [pl.BlockSpec((1, d, tm // d, DIL_SLAB), lambda b, i: (b, 0, i, 0)) for d in rates] \
        + [pl.BlockSpec((1, d, tm // d, LANES), lambda b, i: (b, 0, i, 0)) for d in rates]
    scratch = []
    for d in rates:
        scratch += [pltpu.VMEM((DIL_SLAB // LANES, tm, LANES), F32), pltpu.VMEM((tm, LANES), F32)]
    return pl.pallas_call(
        functools.partial(_merge_kernel, tuple(rates)), grid=(n, t // tm),
        in_specs=specs,
        out_specs=pl.BlockSpec((1, tm, DIL_SLAB), lambda b, i: (b, i, 0)),
        out_shape=jax.ShapeDtypeStruct((n, t, DIL_SLAB), BF16),
        scratch_shapes=scratch,
        compiler_params=_cparams("parallel", "parallel"), name="dil_merge",
    )(*os_, *mls)


def _dil_sample_kernel(d, wb, t_s, sb, q_ref, new_ref, buf_ref, bias_ref, o_ref, ml_ref, st_ref, *scr):
    nk = DIL_KEYS - 1
    news, srcs = [], []
    for sq in range(sb):
        new = new_ref[sq]
        st_ref[sq, 0:wb - t_s] = buf_ref[sq, t_s:wb]
        st_ref[sq, wb - t_s:wb] = new
        if d < t_s:
            src = scr[0].at[sq]
            src[0:wb] = buf_ref[sq]
            src[wb:wb + t_s] = new
        else:
            src = buf_ref.at[sq]
        news.append(new)
        srcs.append(src)
    top = lax.broadcasted_iota(jnp.int32, (8, LANES), 0) < DIL_HEADS
    lane = lax.broadcasted_iota(jnp.int32, (8, LANES), 1)
    bias = bias_ref[...]
    nq, ntok = _dil_classes(d, t_s)

    def both_halves(col):
        x = jnp.where(top, jnp.broadcast_to(col, (8, LANES)), 0.0)
        return x + pltpu.roll(x, DIL_HEADS, 0)

    work = [(sq, r) for sq in range(sb) for r in range(t_s // nq)]
    toks, scores, stats, outs = [], [], [], []
    for sq, r in work:
        src = srcs[sq]
        if d < t_s:
            a3 = src[pl.ds(r, ntok, stride=d)] if d > 1 else src[pl.ds(r, ntok)]
        else:
            pad = jnp.zeros((ntok - nk - 1, 8, LANES), F32)
            a3 = jnp.concatenate([src[pl.ds(r, nk, stride=d)], news[sq][r:r + 1], pad], axis=0)
        a = a3.reshape(ntok * 8, LANES).astype(BF16)
        qc = q_ref[sq, pl.ds(r, nq, stride=d)] if nq > 1 else q_ref[sq, r:r + 1]
        qc = qc.reshape(nq * 8, LANES).astype(BF16)
        toks.append(a)
        scores.append(_dot_nt(qc, a) + bias)
    for w in range(len(work)):
        s = scores[w]
        m = jnp.max(s, axis=-1, keepdims=True)
        p = jnp.exp(s - m)
        l = jnp.sum(p, axis=-1, keepdims=True)
        pn = p / l
        stats.append((m, l))
        p2 = jnp.concatenate([pn, pltpu.roll(pn, DIL_HEADS, 1)], axis=0).astype(BF16)
        outs.append(_dot(p2, toks[w]))
    for w, (sq, r) in enumerate(work):
        o2, (m, l) = outs[w], stats[w]
        for k in range(nq):
            i = r + k * d
            lo, hi = o2[8 * k:8 * k + 8], o2[8 * (nq + k):8 * (nq + k) + 8]
            o_ref[sq, i] = jnp.where(top, lo, pltpu.roll(hi, DIL_HEADS, 0))
            ml_ref[sq, i] = jnp.where(lane == 0, both_halves(m[8 * k:8 * k + 8]),
                                      jnp.where(lane == 1, both_halves(l[8 * k:8 * k + 8]), 0.0))


def _dil_classes(d, t_s):
    nq = max(t_s // d, 1)
    ntok = DIL_KEYS - 1 + nq
    return nq, ntok + ntok % 2


def _dil_sample(qg, newg, buf, bias, d):
    b, t_s = qg.shape[:2]
    wb = buf.shape[1]
    sb = max(1, min(b, DIL_SAMPLE_STEP_TOKENS // wb))
    blk = lambda r: pl.BlockSpec((sb, r, 8, LANES), lambda s: (s, 0, 0, 0))
    tiles = jax.ShapeDtypeStruct((b, t_s, 8, LANES), F32)
    return pl.pallas_call(
        functools.partial(_dil_sample_kernel, d, wb, t_s, sb), grid=(b // sb,),
        in_specs=[blk(t_s), blk(t_s), blk(wb), _full(bias.shape)],
        out_specs=[blk(t_s), blk(t_s), blk(wb)],
        out_shape=[tiles, tiles, jax.ShapeDtypeStruct(buf.shape, F32)],
        scratch_shapes=[pltpu.VMEM((sb, wb + t_s, 8, LANES), F32)] if d < t_s else [],
        compiler_params=_cparams("parallel"), name="dil_sample_attn",
    )(qg, newg, buf, bias)


def _merge_tiles_kernel(ng, *refs):
    o_refs, ml_refs, out_ref = refs[:ng], refs[ng:2 * ng], refs[2 * ng]
    ms = [r[...][:, :, 0:1] for r in ml_refs]
    ls = [r[...][:, :, 1:2] for r in ml_refs]
    m_all = functools.reduce(jnp.maximum, ms)
    es = [l * jnp.exp(m - m_all) for m, l in zip(ms, ls)]
    tot = functools.reduce(jnp.add, es)
    out_ref[...] = functools.reduce(jnp.add, [o[...] * (e / tot) for o, e in zip(o_refs, es)])


def _merge_tiles(os_, mls, tr):
    r = os_[0].shape[0]
    spec = pl.BlockSpec((tr, 8, LANES), lambda i: (i, 0, 0))
    return pl.pallas_call(
        functools.partial(_merge_tiles_kernel, len(os_)), grid=(r // tr,),
        in_specs=[spec] * (2 * len(os_)), out_specs=spec,
        out_shape=jax.ShapeDtypeStruct((r, 8, LANES), F32),
        compiler_params=_cparams("parallel"), name="dil_merge_sample",
    )(*os_, *mls)


def _seg_mats(width, segments):
    s = np.zeros((width, LANES), np.float32)
    inv = np.ones((1, LANES), np.float32)
    for k, (a, n) in enumerate(segments):
        s[a:a + n, k] = 1.0
        inv[0, k] = 1.0 / n
    return jnp.asarray(s, BF16), jnp.asarray(np.concatenate([s.T, s.T]), BF16), jnp.asarray(inv)


def _pad_heads(wm, heads, dim, slab, offset=0):
    r = wm.shape[0]
    out = jnp.pad(wm.reshape(r, heads, dim), ((0, 0), (0, 0), (offset, slab - dim - offset)))
    return out.reshape(r, heads * slab)


def _mem_q_tables(g_qn_mem):
    seg = [(hd * LANES, MEM_DIM) for hd in range(MEM_HEADS)]
    sm, smt, invm = _seg_mats(MEM_W, seg)
    gm = _pad_heads(jnp.tile(g_qn_mem, MEM_HEADS)[None, :], MEM_HEADS, MEM_DIM, LANES) * MEM_SCALE
    return dict(Sm=sm, SmT=smt, invm=invm, gm=gm)


def _mla_tables(pos, g_qn_nope, g_qn_pe, g_kn_pe):
    hr = ROPE_DIM // 2
    inv = ROPE_THETA ** (-jnp.arange(0, ROPE_DIM, 2, dtype=F32) / ROPE_DIM)
    z = lambda k: jnp.zeros((k,), F32)
    rest = LANES - NOPE_DIM - ROPE_DIM
    fq = jnp.concatenate([z(NOPE_DIM), inv, inv, z(rest)])
    cq_amp = jnp.concatenate([g_qn_nope, g_qn_pe, z(rest)]) * MLA_SCALE
    sq_amp = jnp.concatenate([z(NOPE_DIM), -g_qn_pe[hr:], g_qn_pe[:hr], z(rest)]) * MLA_SCALE
    fk = jnp.concatenate([inv, inv, inv, inv, z(LANES - 2 * ROPE_DIM)])
    ck_amp = jnp.concatenate([g_kn_pe, z(LANES - ROPE_DIM)])
    sk_amp = jnp.concatenate([z(ROPE_DIM), -g_kn_pe[hr:], g_kn_pe[:hr], z(LANES - 2 * ROPE_DIM)])
    p = pos.astype(F32)[:, None]
    aq, ak = p * fq[None, :], p * fk[None, :]
    return (jnp.cos(aq) * cq_amp[None, :], jnp.sin(aq) * sq_amp[None, :],
            jnp.cos(ak) * ck_amp[None, :] + jnp.sin(ak) * sk_amp[None, :])


def _prep_mla(i, j, g_attn, w_o, w_in_a, g_q_a, w_q_b, g_kv_a, w_uk, w_uv, g_qn_mem):
    d = w_in_a.shape[1]
    hr = ROPE_DIM // 2
    wi = w_in_a[j]
    o1, o2, o3 = Q_LORA, Q_LORA + KV_LORA, Q_LORA + KV_LORA + ROPE_DIM
    kpe = wi[:, o2:o3]
    w_in = jnp.concatenate([wi[:, :o2], _pad_heads(wi[:, o3:], MEM_HEADS, MEM_DIM, LANES),
                            kpe, jnp.concatenate([kpe[:, hr:], kpe[:, :hr]], axis=1),
                            jnp.zeros((d, LANES - 2 * ROPE_DIM), F32)], axis=1).astype(BF16)
    hq = NOPE_DIM + ROPE_DIM
    wq3 = w_q_b[j].reshape(Q_LORA, MLA_HEADS, hq)
    wq = _pad_heads(w_q_b[j], MLA_HEADS, hq, LANES).astype(BF16)
    sw = jnp.concatenate([wq3[:, :, NOPE_DIM + hr:], wq3[:, :, NOPE_DIM:NOPE_DIM + hr]], axis=2)
    wq_sw = _pad_heads(sw.reshape(Q_LORA, MLA_HEADS * ROPE_DIM), MLA_HEADS, ROPE_DIM, LANES, NOPE_DIM).astype(BF16)
    seg = []
    for hd in range(MLA_HEADS):
        seg += [(hd * LANES, NOPE_DIM), (hd * LANES + NOPE_DIM, ROPE_DIM)]
    s, st, inv = _seg_mats(MLA_HEADS * LANES, seg)
    wk_c = _pad_heads(w_uk[j].reshape(KV_LORA, MLA_HEADS * NOPE_DIM), MLA_HEADS, NOPE_DIM, LANES).astype(BF16)
    eye = np.zeros((LANES, MLA_HEADS, LANES), np.float32)
    for r in range(ROPE_DIM):
        eye[r, :, NOPE_DIM + r] = 1.0
    wk_p = jnp.asarray(eye.reshape(LANES, MLA_HEADS * LANES), BF16)
    wv4 = w_uv[j].reshape(KV_LORA, MLA_HEADS // 2, 2, MLA_V_DIM)
    lo = ((0, 0), (0, 0), (0, LANES - MLA_V_DIM))
    hi = ((0, 0), (0, 0), (LANES - MLA_V_DIM, 0))
    wv = jnp.stack([jnp.pad(wv4[:, :, 0], lo), jnp.pad(wv4[:, :, 1], hi)], axis=2)
    wv = wv.reshape(KV_LORA, MLA_HEADS * LANES).astype(BF16)
    vone = np.zeros((1, MLA_HEADS, LANES), np.float32)
    vone[0, 0::2, MLA_V_DIM] = 1.0
    vone[0, 1::2, 0] = 1.0
    vone = jnp.asarray(vone.reshape(1, MLA_HEADS * LANES))
    pe_rows = np.zeros((ROPE_DIM, MLA_QCAT), np.float32)
    pe_rows[np.arange(ROPE_DIM), KV_LORA + np.arange(ROPE_DIM)] = 1.0
    wcat = jnp.concatenate([
        jnp.pad(jnp.transpose(w_uk[j], (1, 2, 0)), ((0, 0), (0, 0), (0, MLA_QCAT - KV_LORA))),
        jnp.broadcast_to(jnp.asarray(pe_rows)[None], (MLA_HEADS, ROPE_DIM, MLA_QCAT)),
        jnp.zeros((MLA_HEADS, LANES - hq, MLA_QCAT), F32)], axis=1)
    wvp = jnp.transpose(wv4, (1, 2, 0, 3))
    uvp = jnp.concatenate([jnp.pad(wvp[:, 0], lo), jnp.pad(wvp[:, 1], hi)], axis=1)
    ntok = MLA_HEADS * MLA_V_DIM
    w = dict(g_attn=g_attn[i][None], w_in=w_in, g_q_a=g_q_a[j][None], wq=wq, wq_sw=wq_sw,
             g_kv_a=g_kv_a[j][None], S=s, ST=st, inv=inv, wk_c=wk_c, wk_p=wk_p, wv=wv, vone=vone,
             wcat=wcat.astype(BF16), uv_pairs=uvp.astype(BF16),
             wo_tok=w_o[i][:ntok].astype(BF16),
             wo_mem=_pad_rows(w_o[i][ntok:], MEM_HEADS, MEM_DIM, LANES, MEM_DIM).astype(BF16))
    w.update(_mem_q_tables(g_qn_mem[i]))
    return w


def _pad_rows(wm, heads, dim, slab, offset):
    c = wm.shape[1]
    out = jnp.pad(wm.reshape(heads, dim, c), ((0, 0), (offset, slab - dim - offset), (0, 0)))
    return out.reshape(heads * slab, c)


def _prep_dil(i, j, g_attn, w_o, w_in_b, g_qn_b, g_kn_b, g_qn_mem):
    wi = w_in_b[j]
    gh = N_GROUPS * DIL_HEADS
    nqk = gh * DIL_QK
    wq = _pad_heads(wi[:, :nqk], gh, DIL_QK, LANES)
    wk = wi[:, nqk:2 * nqk].reshape(-1, gh, DIL_QK)
    wv = wi[:, 2 * nqk:2 * nqk + gh * DIL_V].reshape(-1, gh, DIL_V)
    wkv = jnp.concatenate([wk, wv], axis=2).reshape(-1, gh * DIL_ROW)
    wmq = _pad_heads(wi[:, 2 * nqk + gh * DIL_V:], MEM_HEADS, MEM_DIM, LANES)
    w_in = jnp.concatenate([wq, wkv, wmq], axis=1).astype(BF16)
    sq, sqt, invq = _seg_mats(QW, [(k * LANES, DIL_QK) for k in range(gh)])
    sk, skt, invk = _seg_mats(KVW, [(k * DIL_ROW, DIL_QK) for k in range(gh)])
    gq = _pad_heads(jnp.repeat(g_qn_b[j], DIL_HEADS, axis=0).reshape(1, nqk), gh, DIL_QK, LANES) * DIL_SCALE
    gk = _pad_heads(jnp.repeat(g_kn_b[j], DIL_HEADS, axis=0).reshape(1, nqk), gh, DIL_QK, DIL_ROW)
    vmask = _pad_heads(jnp.ones((1, gh * DIL_V), F32), gh, DIL_V, DIL_ROW, DIL_QK)
    ntok = DIL_HEADS * DIL_V
    w = dict(g_attn=g_attn[i][None], w_in=w_in, Sq=sq, SqT=sqt, invq=invq, gq=gq,
             Sk=sk, SkT=skt, invk=invk, gk=gk, vmask=vmask,
             wo_tok=_pad_rows(w_o[i][:ntok], DIL_HEADS, DIL_V, DIL_ROW, DIL_QK).astype(BF16),
             wo_mem=_pad_rows(w_o[i][ntok:], MEM_HEADS, MEM_DIM, LANES, MEM_DIM).astype(BF16))
    w.update(_mem_q_tables(g_qn_mem[i]))
    return w


def _prep_mem(i, g_mem, w_mem_kv, g_kn_mem):
    sk, skt, invk = _seg_mats(MEM_W, [(hd * LANES, MEM_DIM) for hd in range(MEM_HEADS)])
    gk = _pad_heads(jnp.tile(g_kn_mem[i], MEM_HEADS)[None, :], MEM_HEADS, MEM_DIM, LANES)
    vmask = _pad_heads(jnp.ones((1, MEM_HEADS * MEM_DIM), F32), MEM_HEADS, MEM_DIM, LANES, MEM_DIM)
    return dict(g_mem=g_mem[i][None], w_mem=w_mem_kv[i].astype(BF16), Sk=sk, SkT=skt, invk=invk, gk=gk, vmask=vmask)


def _prep_ffn(i, g_ffn, w_up_b, conv_w, conv_b, w_down_b):
    return dict(layer=i, g_ffn=g_ffn[i][None], w_up=w_up_b, conv_w=conv_w[i],
                conv_b=conv_b[i][None], w_down=w_down_b)


def _rel_bucket(dist):
    max_exact = N_BUCKETS // 2
    dd = jnp.maximum(dist.astype(F32), 1.0)
    large = max_exact + (jnp.log(dd / max_exact) / math.log(MAX_DISTANCE / max_exact)
                         * (N_BUCKETS - max_exact)).astype(jnp.int32)
    large = jnp.minimum(large, N_BUCKETS - 1)
    return jnp.where(dist < max_exact, dist, large)


def _dil_bias_tables(rel_bias, tq, t_s):
    band, samp = [], []

    def toeplitz(rev, rows, cols):
        period = cols + rows + 1
        f = jnp.concatenate([rev, jnp.full((DIL_HEADS, period - DIL_KEYS), NEG, F32)], axis=1)
        return jnp.tile(f, (1, rows))[:, :rows * (period - 1)].reshape(DIL_HEADS, rows, period - 1)[:, :, :cols]

    srow = jnp.arange(8)[None, None, None, :] == jnp.arange(DIL_HEADS)[:, None, None, None]
    for g in range(N_GROUPS):
        bk = _rel_bucket(DIL_RATES[g] * jnp.arange(DIL_KEYS, dtype=jnp.int32))
        b = rel_bias[bk][:, g * DIL_HEADS:(g + 1) * DIL_HEADS].T.astype(F32)
        rev = b[:, ::-1]
        band.append(toeplitz(rev, tq, 2 * tq))
        nq, ntok = _dil_classes(DIL_RATES[g], t_s)
        tab = jnp.where(srow, toeplitz(rev, nq, ntok)[..., None], NEG).reshape(DIL_HEADS, nq, ntok * 8)
        tab = jnp.pad(tab.transpose(1, 0, 2), ((0, 0), (0, 8 - DIL_HEADS), (0, 0)), constant_values=NEG)
        samp.append(tab.reshape(nq * 8, ntok * 8))
    return band, samp


def kernel(x_prompt, x_sample, cache_mla, state_win0, state_win1, state_win2, cache_mem, state_conv, page_table, mem_prompt, rel_bias, g_attn, w_o, w_in_a, g_q_a, w_q_b, g_kv_a, w_uk, w_uv, g_qn_nope, g_qn_pe, g_kn_pe, w_in_b, g_qn_b, g_kn_b, g_mem, w_mem_kv, g_qn_mem, g_kn_mem, g_ffn, w_up, conv_w, conv_b, w_down):
    state_wins = (state_win0, state_win1, state_win2)
    n, t_p, d = x_prompt.shape
    b, t_s, _ = x_sample.shape
    n_mem = mem_prompt.shape[1]
    d_ff = w_down.shape[1]
    past = page_table.shape[1] * cache_mla.shape[2]
    mp, ms = n * t_p, b * t_s
    tm_p = min(256, t_p)
    tm_s = min(256, ms)
    tq_band = DIL_KEYS - 1
    assert t_s >= 2 and t_p % tm_p == 0 and ms % tm_s == 0 and tm_s % t_s == 0
    assert all(sw.shape[2] == wdw and wdw == (DIL_KEYS - 1) * r
               for sw, wdw, r in zip(state_wins, DIL_WINDOWS, DIL_RATES))

    xp = x_prompt.reshape(mp, d)
    xs = x_sample.reshape(ms, d)
    mem2 = mem_prompt.reshape(n * n_mem, d)
    pos_p = jnp.arange(t_p)
    pos_s = jnp.tile(past + jnp.arange(t_s), tm_s // t_s)
    band_bias, samp_bias = _dil_bias_tables(rel_bias, tq_band, t_s)
    cache_rows = cache_mem.reshape(cache_mem.shape[0], b, n_mem * MEM_HEADS, 2 * MEM_DIM)

    w_up_b, w_down_b = w_up.astype(BF16), w_down.astype(BF16)
    mem_out, conv_p_out, conv_s_out = [], [], []
    for i in range(2):
        j = i // 2
        if i % 2 == 0:
            w = _prep_mla(i, j, g_attn, w_o, w_in_a, g_q_a, w_q_b, g_kv_a, w_uk, w_uv, g_qn_mem)
            cq, sq, tk = _mla_tables(pos_p, g_qn_nope[j], g_qn_pe[j], g_kn_pe[j])
            q_p, rows_p, mq_p, k_p, v_p = _proj_mla(xp, True, tm_p, w, cq, sq, tk)
            cq, sq, tk = _mla_tables(pos_s, g_qn_nope[j], g_qn_pe[j], g_kn_pe[j])
            qc_s, rows_s, mq_s = _proj_mla(xs, False, tm_s, w, cq, sq, tk)
            hw = MLA_HEADS * LANES
            tok_p = _mla_flash(q_p.reshape(n, t_p, hw), k_p.reshape(n, t_p, hw), v_p.reshape(n, t_p, hw),
                               min(512, t_p)).reshape(mp, -1)
            q3 = qc_s.reshape(b, t_s, MLA_HEADS, MLA_QCAT).transpose(0, 2, 1, 3).reshape(b, MLA_HEADS * t_s, MLA_QCAT)
            ctx = _mla_sample_attn(q3, rows_s.reshape(b, t_s, MLA_ROW), cache_mla[j].transpose(0, 2, 1),
                                   page_table, min(MLA_PAGES_PER_STEP, page_table.shape[1]))
            ctx = ctx.reshape(b, MLA_HEADS, t_s, KV_LORA).transpose(0, 2, 1, 3).reshape(ms, MLA_HEADS * KV_LORA)
            tok_s = _mla_uv(ctx, w['uv_pairs'], tm_s)
            mla_rows_p = rows_p.reshape(1, n, t_p, MLA_ROW)
            mla_rows_s = rows_s.reshape(1, b, t_s, MLA_ROW)
        else:
            w = _prep_dil(i, j, g_attn, w_o, w_in_b, g_qn_b, g_kn_b, g_qn_mem)
            res = _proj_dil(xp.reshape(n, t_p, d), True, tm_p, w)
            mq_p = res[0].reshape(mp, MEM_W)
            nc = DIL_ROW // LANES
            os_, mls = [], []
            for g, dil in enumerate(DIL_RATES):
                l = t_p // dil
                o, ml = _band_attn(res[4 + g].reshape(n * dil, l, DIL_HEADS * LANES),
                                   res[7 + g].reshape(n * dil, l, DIL_SLAB), band_bias[g], tq_band,
                                   max(k for k in (1, 2, 4) if l % (k * tq_band) == 0))
                os_.append(o.reshape(n, dil, l, DIL_SLAB))
                mls.append(ml.reshape(n, dil, l, LANES))
            tok_p = _merge(os_, mls, DIL_RATES, n, t_p, tm_p).reshape(mp, DIL_SLAB)
            win_p = [wt.reshape(n, wt.shape[1], nc, DIL_HEADS, LANES).transpose(0, 1, 3, 2, 4)
                     .reshape(1, n, wt.shape[1], DIL_HEADS, DIL_ROW) for wt in res[1:4]]
            q_s, kv_s, mq_s = _proj_dil(xs.reshape(b, t_s, d), False, tm_s, w)
            nc = DIL_ROW // LANES
            q_t = jnp.pad(q_s.astype(F32).reshape(b, t_s, N_GROUPS, DIL_HEADS, LANES),
                          ((0, 0), (0, 0), (0, 0), (0, 8 - DIL_HEADS), (0, 0))).transpose(2, 0, 1, 3, 4)
            kv_t = kv_s.reshape(b, t_s, N_GROUPS, DIL_HEADS, nc, LANES).transpose(2, 0, 1, 4, 3, 5)
            kv_t = kv_t.reshape(N_GROUPS, b, t_s, 8, LANES)
            os_, mls, win_s = [], [], []
            for g, dil in enumerate(DIL_RATES):
                buf = state_wins[g][j]
                wb = buf.shape[1]
                buf_t = buf.reshape(b, wb, DIL_HEADS, nc, LANES).transpose(0, 1, 3, 2, 4).reshape(b, wb, 8, LANES)
                o, ml, st = _dil_sample(q_t[g], kv_t[g], buf_t, samp_bias[g], dil)
                os_.append(o.reshape(ms, 8, LANES))
                mls.append(ml.reshape(ms, 8, LANES))
                st = st.reshape(b, wb, nc, DIL_HEADS, LANES).transpose(0, 1, 3, 2, 4)
                win_s.append(st.reshape(1, b, wb, DIL_HEADS, DIL_ROW))
            tok_s = _merge_tiles(os_, mls, tm_s).reshape(ms, nc, DIL_HEADS, LANES).transpose(0, 2, 1, 3)
            tok_s = tok_s.reshape(ms, DIL_SLAB).astype(BF16)

        wm = _prep_mem(i, g_mem, w_mem_kv, g_kn_mem)
        mkv = _mem_kv(mem2, wm, min(256, n * n_mem))
        mem_out.append(mkv.reshape(n, n_mem, MEM_HEADS, 2 * MEM_DIM))
        mo_p = _mem_attn(mq_p.reshape(n, t_p, MEM_W), mkv.reshape(n, n_mem, MEM_W), min(512, t_p)).reshape(mp, MEM_W)
        mo_s = _mem_attn_rows(mq_s.reshape(b, t_s * MEM_HEADS, LANES), cache_rows, i, min(8, b)).reshape(ms, MEM_W)
        xp = _out_proj(xp, tok_p, mo_p, w['wo_tok'], w['wo_mem'], 2 * tm_p if mp % (2 * tm_p) == 0 else tm_p)
        xs = _out_proj(xs, tok_s, mo_s, w['wo_tok'], w['wo_mem'], tm_s)

        wf = _prep_ffn(i, g_ffn, w_up_b, conv_w, conv_b, w_down_b)
        tm_f = 2 * tm_p if t_p % (2 * tm_p) == 0 else tm_p
        xp, tails = _ffn(xp, t_p, wf, tm_f)
        conv_p_out.append(tails.reshape(n, t_p // tm_f, 8, d_ff)[:, -1, 6:, :])
        prev = state_conv[i]
        zeros = jnp.zeros((b, t_s - 2, d_ff), F32)
        p2 = jnp.concatenate([prev, zeros], axis=1).reshape(ms, d_ff)
        p1 = jnp.concatenate([prev[:, 1:], zeros, jnp.zeros((b, 1, d_ff), F32)], axis=1).reshape(ms, d_ff)
        xs, gates = _ffn(xs, t_s, wf, tm_s, prev=(p1, p2))
        conv_s_out.append(gates.reshape(b, t_s, d_ff)[:, t_s - 2:, :])

    return (xp.reshape(n, t_p, d), xs.reshape(b, t_s, d), mla_rows_p, mla_rows_s,
            win_p[0], win_p[1], win_p[2], win_s[0], win_s[1], win_s[2],
            jnp.stack(mem_out), jnp.stack(conv_p_out), jnp.stack(conv_s_out))
```
